```python
import jax, jax.numpy as jnp
from jax import lax
import numpy as np

D_MODEL = 2048
BATCH = 1
SEQ = 16384
DEPTH = 2

NSA_HEADS = 8
NSA_GROUPS = 2
NSA_HPG = NSA_HEADS // NSA_GROUPS
NSA_HD = 128
CMP_LEN = 32
CMP_STRIDE = 16
CMP_HID = 256
SLC_LEN = 64
SLC_TOPK = 16
WINDOW = 512
Q_BLOCK = 128
ML_HEADS = 4
ML_DQK = 128
ML_DV = 256
ML_CHUNK = 64
ML_CONV = 4
D_FF = 5632
FFN_CONV = 3
ALPHA = (2 * DEPTH) ** 0.25
BETA = (8 * DEPTH) ** -0.25
LN_EPS = 1e-5
NEG_INF = -1e30
IN_SIZES = (NSA_HEADS * NSA_HD, 3 * 2 * NSA_GROUPS * NSA_HD, 3 * NSA_HEADS,
            2 * ML_HEADS * ML_DQK, ML_HEADS * ML_DV, 2 * ML_HEADS, ML_HEADS * ML_DV, 2 * D_MODEL)
IN_COLS = (NSA_HEADS * NSA_HD + 3 * 2 * NSA_GROUPS * NSA_HD + 3 * NSA_HEADS
           + 2 * ML_HEADS * ML_DQK + ML_HEADS * ML_DV + 2 * ML_HEADS + ML_HEADS * ML_DV + 2 * D_MODEL)

kernel_name = "nsa_mlstm_convffn_hybrid"


def layer_norm(x, gain=None, bias=None):
    xf = x.astype(jnp.float32)
    mu = jnp.mean(xf, axis=-1, keepdims=True)
    var = jnp.mean(jnp.square(xf - mu), axis=-1, keepdims=True)
    y = (xf - mu) * lax.rsqrt(var + LN_EPS)
    if gain is not None:
        y = y * gain.astype(jnp.float32) + bias.astype(jnp.float32)
    return y.astype(x.dtype)


def masked_softmax(s, mask):
    p = jax.nn.softmax(jnp.where(mask, s, NEG_INF), axis=-1)
    return jnp.where(mask, p, 0.0)


def causal_dwconv(x, w, b):
    k, ch = w.shape
    y = lax.conv_general_dilated(x, w[:, None, :].astype(x.dtype), window_strides=(1,),
                                 padding=[(k - 1, 0)], dimension_numbers=('NWC', 'WIO', 'NWC'),
                                 feature_group_count=ch)
    return y + b.astype(x.dtype)


def compress_blocks(k, pe, w1, w2):
    b, s, g, hd = k.shape
    n_ch = s // CMP_STRIDE
    r = CMP_LEN // CMP_STRIDE
    ch = k.reshape(b, n_ch, CMP_STRIDE, g, hd)
    blocks = jnp.concatenate([ch[:, j:n_ch - r + 1 + j] for j in range(r)], axis=2)
    blocks = blocks + pe[None, None, :, None, :].astype(k.dtype)
    flat = blocks.transpose(0, 1, 3, 2, 4).reshape(b, n_ch - r + 1, g, CMP_LEN * hd)
    return jax.nn.gelu(flat @ w1) @ w2


def nsa_attention(q, kv, gate_raw, cmp_pe, cmp_w1, cmp_w2):
    b, s, _ = q.shape
    q = q.reshape(b, s, NSA_GROUPS, NSA_HPG, NSA_HD)
    kv = kv.reshape(b, s, 3, 2, NSA_GROUPS, NSA_HD)
    gates = jax.nn.sigmoid(gate_raw).reshape(b, s, NSA_GROUPS, NSA_HPG, 3)
    scale = NSA_HD ** -0.5

    kc = compress_blocks(kv[:, :, 0, 0], cmp_pe[0], cmp_w1[0], cmp_w2[0])
    vc = compress_blocks(kv[:, :, 0, 1], cmp_pe[1], cmp_w1[1], cmp_w2[1])
    n_cmp = kc.shape[1]
    n_slc = s // SLC_LEN
    ks = kv[:, :, 1, 0].reshape(b, n_slc, SLC_LEN, NSA_GROUPS, NSA_HD).transpose(0, 3, 1, 2, 4)
    vs = kv[:, :, 1, 1].reshape(b, n_slc, SLC_LEN, NSA_GROUPS, NSA_HD).transpose(0, 3, 1, 2, 4)
    pad = ((0, 0), (WINDOW, 0), (0, 0), (0, 0))
    kw = jnp.pad(kv[:, :, 2, 0], pad)
    vw = jnp.pad(kv[:, :, 2, 1], pad)

    c_start = jnp.arange(n_cmp) * CMP_STRIDE
    s_start = jnp.arange(n_slc) * SLC_LEN
    overlap = ((c_start[:, None] < s_start[None, :] + SLC_LEN)
               & (c_start[:, None] + CMP_LEN > s_start[None, :])).astype(jnp.float32)
    n_sel = min(SLC_TOPK, n_slc)
    n_blk = s // Q_BLOCK
    q_blocks = q.reshape(b, n_blk, Q_BLOCK, NSA_GROUPS, NSA_HPG, NSA_HD).swapaxes(0, 1)
    g_blocks = gates.reshape(b, n_blk, Q_BLOCK, NSA_GROUPS, NSA_HPG, 3).swapaxes(0, 1)
    bidx = jnp.arange(b)[:, None, None, None]
    gidx = jnp.arange(NSA_GROUPS)[None, :, None, None]
    slc_ids = jnp.arange(n_slc)

    def block(args):
        blk, qb, gb = args
        t = blk * Q_BLOCK + jnp.arange(Q_BLOCK)
        s_c = jnp.einsum('bqghd,bngd->bghqn', qb, kc, preferred_element_type=jnp.float32) * scale
        mask_c = (c_start + CMP_LEN - 1)[None, :] <= t[:, None]
        p_c = masked_softmax(s_c, mask_c)
        o_c = jnp.einsum('bghqn,bngd->bqghd', p_c, vc)
        imp = jnp.sum(p_c, axis=2) @ overlap
        cur = t // SLC_LEN
        forced = (slc_ids[None, :] == 0) | (slc_ids[None, :] == cur[:, None]) | (slc_ids[None, :] == cur[:, None] - 1)
        valid = s_start[None, :] <= t[:, None]
        score = jnp.where(forced, 1e6, jnp.where(valid, imp, -1.0))
        idx = lax.top_k(score, n_sel)[1]
        k_sel = ks[bidx, gidx, idx]
        v_sel = vs[bidx, gidx, idx].reshape(b, NSA_GROUPS, Q_BLOCK, n_sel * SLC_LEN, NSA_HD)
        s_s = jnp.einsum('bqghd,bgqnld->bghqnl', qb, k_sel, preferred_element_type=jnp.float32) * scale
        s_s = s_s.reshape(b, NSA_GROUPS, NSA_HPG, Q_BLOCK, n_sel * SLC_LEN)
        pos = idx[..., None] * SLC_LEN + jnp.arange(SLC_LEN)
        mask_s = (pos <= t[None, None, :, None, None]).reshape(b, NSA_GROUPS, 1, Q_BLOCK, n_sel * SLC_LEN)
        p_s = masked_softmax(s_s, mask_s)
        o_s = jnp.einsum('bghqm,bgqmd->bqghd', p_s, v_sel)
        kwb = lax.dynamic_slice_in_dim(kw, blk * Q_BLOCK, WINDOW + Q_BLOCK, axis=1)
        vwb = lax.dynamic_slice_in_dim(vw, blk * Q_BLOCK, WINDOW + Q_BLOCK, axis=1)
        kpos = blk * Q_BLOCK - WINDOW + jnp.arange(WINDOW + Q_BLOCK)
        mask_w = (kpos[None, :] <= t[:, None]) & (kpos[None, :] > t[:, None] - WINDOW) & (kpos[None, :] >= 0)
        s_w = jnp.einsum('bqghd,bkgd->bghqk', qb, kwb, preferred_element_type=jnp.float32) * scale
        p_w = masked_softmax(s_w, mask_w)
        o_w = jnp.einsum('bghqk,bkgd->bqghd', p_w, vwb)
        gbf = gb.astype(jnp.float32)
        return gbf[..., 0:1] * o_c + gbf[..., 1:2] * o_s + gbf[..., 2:3] * o_w

    out = lax.map(block, (jnp.arange(n_blk), q_blocks, g_blocks))
    return out.swapaxes(0, 1).reshape(b, s, NSA_HEADS * NSA_HD).astype(q.dtype)


def mlstm(qk, v, if_raw, o_raw, conv_w, conv_b, gate_b, norm_g):
    b, s, _ = qk.shape
    qk = jax.nn.silu(causal_dwconv(qk, conv_w, conv_b))
    q, k = jnp.split(qk, 2, axis=-1)
    gi = (if_raw + gate_b).astype(jnp.float32)
    i_pre = gi[..., :ML_HEADS]
    log_f = jax.nn.log_sigmoid(gi[..., ML_HEADS:])
    nc = s // ML_CHUNK

    def chunks(a, d):
        return a.reshape(b, nc, ML_CHUNK, ML_HEADS, d).transpose(1, 0, 3, 2, 4).astype(jnp.float32)

    def gchunks(a):
        return a.reshape(b, nc, ML_CHUNK, ML_HEADS).transpose(1, 0, 3, 2)

    qc = chunks(q, ML_DQK)
    kc = chunks(k, ML_DQK) * (ML_DQK ** -0.5)
    vc = chunks(v, ML_DV)
    causal = jnp.tril(jnp.ones((ML_CHUNK, ML_CHUNK), dtype=bool))

    def body(carry, xs):
        c_st, n_st, m_st = carry
        qt, kt, vt, ig, lf = xs
        bcum = jnp.cumsum(lf, axis=-1)
        dmat = jnp.where(causal, bcum[..., :, None] - bcum[..., None, :] + ig[..., None, :], NEG_INF)
        inter = bcum + m_st[..., None]
        m_t = jnp.maximum(inter, jnp.max(dmat, axis=-1))
        w_inter = jnp.exp(inter - m_t)
        smat = jnp.einsum('bhtd,bhsd->bhts', qt, kt) * jnp.exp(dmat - m_t[..., None])
        num = w_inter[..., None] * jnp.einsum('bhtd,bhvd->bhtv', qt, c_st) + jnp.einsum('bhts,bhsv->bhtv', smat, vt)
        den = w_inter * jnp.einsum('bhtd,bhd->bht', qt, n_st) + jnp.sum(smat, axis=-1)
        h = num / jnp.maximum(jnp.abs(den), jnp.exp(-m_t))[..., None]
        b_last = bcum[..., -1]
        a = b_last[..., None] - bcum + ig
        m_new = jnp.maximum(b_last + m_st, jnp.max(a, axis=-1))
        decay = jnp.exp(b_last + m_st - m_new)
        wa = jnp.exp(a - m_new[..., None])
        c_new = decay[..., None, None] * c_st + jnp.einsum('bhs,bhsv,bhsd->bhvd', wa, vt, kt)
        n_new = decay[..., None] * n_st + jnp.einsum('bhs,bhsd->bhd', wa, kt)
        return (c_new, n_new, m_new), h

    init = (jnp.zeros((b, ML_HEADS, ML_DV, ML_DQK), jnp.float32),
            jnp.zeros((b, ML_HEADS, ML_DQK), jnp.float32),
            jnp.zeros((b, ML_HEADS), jnp.float32))
    _, h = lax.scan(body, init, (qc, kc, vc, gchunks(i_pre), gchunks(log_f)))
    h = h.transpose(1, 0, 3, 2, 4).reshape(b, s, ML_HEADS, ML_DV)
    mu = jnp.mean(h, axis=-1, keepdims=True)
    var = jnp.mean(jnp.square(h - mu), axis=-1, keepdims=True)
    h = ((h - mu) * lax.rsqrt(var + LN_EPS)).reshape(b, s, ML_HEADS * ML_DV) * norm_g.astype(jnp.float32)
    return (h * jax.nn.sigmoid(o_raw.astype(jnp.float32))).astype(qk.dtype)


def token_mixer(h, w_in, cmp_pe, cmp_w1, cmp_w2, ml_conv_w, ml_conv_b, ml_gate_b, ml_norm_g,
                w_br_nsa, w_br_ml, w_o):
    proj = h @ w_in
    offs = np.cumsum(IN_SIZES)[:-1].tolist()
    nsa_q, nsa_kv, nsa_g, ml_qk, ml_v, ml_if, ml_o, merge_g = jnp.split(proj, offs, axis=-1)
    y_nsa = nsa_attention(nsa_q, nsa_kv, nsa_g, cmp_pe, cmp_w1, cmp_w2)
    y_ml = mlstm(ml_qk, ml_v, ml_if, ml_o, ml_conv_w, ml_conv_b, ml_gate_b, ml_norm_g)
    g_nsa, g_ml = jnp.split(jax.nn.sigmoid(merge_g), 2, axis=-1)
    merged = g_nsa * (y_nsa @ w_br_nsa) + g_ml * (y_ml @ w_br_ml)
    return merged @ w_o


def conv_ffn(h, w_up, conv_w, conv_b, w_down):
    a, g = jnp.split(h @ w_up, 2, axis=-1)
    a = causal_dwconv(a, conv_w, conv_b)
    return (jax.nn.silu(a) * g) @ w_down


def setup_inputs(seed: int = 0) -> dict:
    key = jax.random.key(seed)
    ks = jax.random.split(key, 24)

    def nrm(k, shape, scale):
        return jax.random.normal(k, shape, jnp.float32) * scale

    d = D_MODEL
    ml_qk_w = 2 * ML_HEADS * ML_DQK
    f_bias = jnp.linspace(3.0, 6.0, ML_HEADS)[None, :] + nrm(ks[9], (DEPTH, ML_HEADS), 0.1)
    i_bias = nrm(ks[10], (DEPTH, ML_HEADS), 0.1)
    return {
        "x": nrm(ks[0], (BATCH, SEQ, d), 1.0),
        "c": nrm(ks[1], (BATCH, d), 1.0),
        "w_ada": nrm(ks[2], (DEPTH, d, 6 * d), d ** -0.5),
        "b_ada": nrm(ks[3], (DEPTH, 6 * d), 0.01),
        "w_in": nrm(ks[4], (DEPTH, d, IN_COLS), d ** -0.5),
        "cmp_pe": nrm(ks[5], (DEPTH, 2, CMP_LEN, NSA_HD), 0.02),
        "cmp_w1": nrm(ks[6], (DEPTH, 2, CMP_LEN * NSA_HD, CMP_HID), (CMP_LEN * NSA_HD) ** -0.5),
        "cmp_w2": nrm(ks[7], (DEPTH, 2, CMP_HID, NSA_HD), CMP_HID ** -0.5),
        "ml_conv_w": nrm(ks[8], (DEPTH, ML_CONV, ml_qk_w), ML_CONV ** -0.5),
        "ml_conv_b": nrm(ks[11], (DEPTH, ml_qk_w), 0.01),
        "ml_gate_b": jnp.concatenate([i_bias, f_bias], axis=-1),
        "ml_norm_g": 1.0 + nrm(ks[12], (DEPTH, ML_HEADS * ML_DV), 0.02),
        "w_br_nsa": nrm(ks[13], (DEPTH, NSA_HEADS * NSA_HD, d), (NSA_HEADS * NSA_HD) ** -0.5),
        "w_br_ml": nrm(ks[14], (DEPTH, ML_HEADS * ML_DV, d), (ML_HEADS * ML_DV) ** -0.5),
        "w_o": nrm(ks[15], (DEPTH, d, d), BETA * d ** -0.5),
        "w_up": nrm(ks[16], (DEPTH, d, 2 * D_FF), d ** -0.5),
        "ffn_conv_w": nrm(ks[17], (DEPTH, FFN_CONV, D_FF), FFN_CONV ** -0.5),
        "ffn_conv_b": nrm(ks[18], (DEPTH, D_FF), 0.01),
        "w_down": nrm(ks[19], (DEPTH, D_FF, d), BETA * D_FF ** -0.5),
        "ln_g": 1.0 + nrm(ks[20], (DEPTH, 2, d), 0.02),
        "ln_b": nrm(ks[21], (DEPTH, 2, d), 0.01),
    }


def reference(x, c, w_ada, b_ada, w_in, cmp_pe, cmp_w1, cmp_w2, ml_conv_w, ml_conv_b, ml_gate_b,
              ml_norm_g, w_br_nsa, w_br_ml, w_o, w_up, ffn_conv_w, ffn_conv_b, w_down, ln_g, ln_b):
    c_act = jax.nn.silu(c)
    for l in range(DEPTH):
        mod = (c_act @ w_ada[l] + b_ada[l])[:, None, :]
        sh1, sc1, g1, sh2, sc2, g2 = jnp.split(mod, 6, axis=-1)
        h = layer_norm(x) * (1.0 + sc1) + sh1
        y = token_mixer(h, w_in[l], cmp_pe[l], cmp_w1[l], cmp_w2[l], ml_conv_w[l], ml_conv_b[l],
                        ml_gate_b[l], ml_norm_g[l], w_br_nsa[l], w_br_ml[l], w_o[l])
        x = layer_norm(ALPHA * x + g1 * y, ln_g[l, 0], ln_b[l, 0])
        h = layer_norm(x) * (1.0 + sc2) + sh2
        y = conv_ffn(h, w_up[l], ffn_conv_w[l], ffn_conv_b[l], w_down[l])
        x = layer_norm(ALPHA * x + g2 * y, ln_g[l, 1], ln_b[l, 1])
    return x
```

```python
import functools
import math

import jax
import jax.numpy as jnp
from jax import lax
from jax.experimental import pallas as pl
from jax.experimental.pallas import tpu as pltpu

F32 = jnp.float32
BF16 = jnp.bfloat16

D_MODEL = 2048
DEPTH = 2
NSA_HEADS = 8
NSA_GROUPS = 2
NSA_HPG = NSA_HEADS // NSA_GROUPS
NSA_HD = 128
CMP_LEN = 32
CMP_STRIDE = 16
CMP_HID = 256
SLC_LEN = 64
SLC_TOPK = 16
WINDOW = 512
Q_BLOCK = 128
ML_HEADS = 4
ML_DQK = 128
ML_DV = 256
ML_CHUNK = 64
ML_CONV = 4
D_FF = 5632
FFN_CONV = 3
ALPHA = (2 * DEPTH) ** 0.25
LN_EPS = 1e-5
NEG_INF = -1e30

V7X_VMEM_BYTES = 64 * 1024 * 1024
VMEM_LIMIT = 56 * 1024 * 1024

_NSA_Q = NSA_HEADS * NSA_HD
_NSA_KV = 3 * 2 * NSA_GROUPS * NSA_HD
_NSA_G = 3 * NSA_HEADS
_ML_QK = 2 * ML_HEADS * ML_DQK
_ML_V = ML_HEADS * ML_DV
_ML_IF = 2 * ML_HEADS
_CMP_COLS = 2 * NSA_GROUPS * NSA_HD
_F_MERGE = 0
_F_CMP = 2 * D_MODEL
_F_MLQK = _F_CMP + _CMP_COLS
_F_MLO = _F_MLQK + _ML_QK
_F_SMALL = _F_MLO + _ML_V
_F_COLS = _F_SMALL + 4 * 128
_B_Q = 0
_B_KV = _NSA_Q
_B_MLV = _B_KV + 8 * NSA_HD
_B_COLS = _B_MLV + _ML_V

SEL_TILE = 512
SEL_BIG = 2.0 ** 60


def _cparams(sem):
    return pltpu.CompilerParams(dimension_semantics=sem, vmem_limit_bytes=VMEM_LIMIT)


def _ln_rows(x):
    mu = jnp.mean(x, axis=-1, keepdims=True)
    xc = x - mu
    var = jnp.mean(xc * xc, axis=-1, keepdims=True)
    return xc * lax.rsqrt(var + LN_EPS)


def _sigmoid(x):
    return 1.0 / (1.0 + jnp.exp(-x))


def _silu(x):
    return x * _sigmoid(x)


def _gelu_tanh(x):
    c = math.sqrt(2.0 / math.pi)
    return x * (0.5 * (1.0 + jnp.tanh(c * (x + 0.044715 * (x * x * x)))))


def _log_sigmoid(x):
    return jnp.minimum(x, 0.0) - jnp.log1p(jnp.exp(-jnp.abs(x)))


def _dot(a, b):
    return jnp.dot(a, b, preferred_element_type=F32)


def _dot_nt(a, b):
    return lax.dot_general(a, b, (((1,), (1,)), ((), ())), preferred_element_type=F32)


def _dot_tn(a, b):
    return lax.dot_general(a, b, (((0,), (0,)), ((), ())), preferred_element_type=F32)


def _masked_softmax(s, mask):
    s = jnp.where(mask, s, NEG_INF)
    m = jnp.max(s, axis=-1, keepdims=True)
    e = jnp.where(mask, jnp.exp(s - m), 0.0)
    den = jnp.sum(e, axis=-1, keepdims=True)
    return e / jnp.where(den > 0.0, den, 1.0)


def _mod_kernel(c_ref, w_ref, b_ref, o_ref):
    ca = _silu(c_ref[...])
    o = _dot(ca.astype(BF16), w_ref[0].astype(BF16))
    o_ref[0] = o[0:1] + b_ref[0]


def _modulation(c, w_ada, b_ada):
    depth, d, n = w_ada.shape
    tn = 1024
    c8 = jnp.broadcast_to(c, (8, d))
    return pl.pallas_call(
        _mod_kernel,
        grid=(depth, n // tn),
        in_specs=[
            pl.BlockSpec((8, d), lambda l, j: (0, 0)),
            pl.BlockSpec((1, d, tn), lambda l, j: (l, 0, j)),
            pl.BlockSpec((1, 1, tn), lambda l, j: (l, 0, j)),
        ],
        out_specs=pl.BlockSpec((1, 1, tn), lambda l, j: (l, 0, j)),
        out_shape=jax.ShapeDtypeStruct((depth, 1, n), F32),
        compiler_params=_cparams(("parallel", "parallel")),
        name="adaln_mod",
    )(c8, w_ada, b_ada.reshape(depth, 1, n))


def _inproj_kernel(x_ref, sc_ref, sh_ref, w_ref, o_ref, h_scr):
    @pl.when(pl.program_id(1) == 0)
    def _():
        h = _ln_rows(x_ref[...]) * (1.0 + sc_ref[...]) + sh_ref[...]
        h_scr[...] = h.astype(BF16)

    o_ref[...] = _dot(h_scr[...], w_ref[...]).astype(o_ref.dtype)


def _inproj(x, sc, sh, w, out_dtype, tm, tn):
    s, d = x.shape
    n = w.shape[1]
    return pl.pallas_call(
        _inproj_kernel,
        grid=(s // tm, n // tn),
        in_specs=[
            pl.BlockSpec((tm, d), lambda i, j: (i, 0)),
            pl.BlockSpec((1, d), lambda i, j: (0, 0)),
            pl.BlockSpec((1, d), lambda i, j: (0, 0)),
            pl.BlockSpec((d, tn), lambda i, j: (0, j)),
        ],
        out_specs=pl.BlockSpec((tm, tn), lambda i, j: (i, j)),
        out_shape=jax.ShapeDtypeStruct((s, n), out_dtype),
        scratch_shapes=[pltpu.VMEM((tm, d), BF16)],
        compiler_params=_cparams(("parallel", "arbitrary")),
        name="in_proj",
    )(x, sc, sh, w)


def _split_w_in(w):
    d = w.shape[0]
    o = 0
    q = w[:, o:o + _NSA_Q]; o += _NSA_Q
    kv = w[:, o:o + _NSA_KV]; o += _NSA_KV
    g = w[:, o:o + _NSA_G]; o += _NSA_G
    mlqk = w[:, o:o + _ML_QK]; o += _ML_QK
    mlv = w[:, o:o + _ML_V]; o += _ML_V
    mlif = w[:, o:o + _ML_IF]; o += _ML_IF
    mlo = w[:, o:o + _ML_V]; o += _ML_V
    merge = w[:, o:o + 2 * D_MODEL]
    per_g = 3 * NSA_HPG
    z = lambda n: jnp.zeros((d, n), w.dtype)
    wb = jnp.concatenate([q, kv[:, _CMP_COLS:], mlv], axis=1)
    wf = jnp.concatenate(
        [merge, kv[:, :_CMP_COLS], mlqk, mlo,
         g[:, :per_g], z(128 - per_g), g[:, per_g:], z(128 - per_g), mlif, z(128 - _ML_IF), z(128)],
        axis=1)
    return wb.astype(BF16), wf.astype(BF16)


def _compress_kernel(a_ref, pe_ref, w1_ref, w2_ref, o_ref):
    a = a_ref[0]
    pe = pe_ref[0]
    half = a.shape[1]
    n = a.shape[0]
    p = _dot((a + pe[:, :half]).astype(BF16), w1_ref[0, :half, :].astype(BF16))
    q = _dot((a + pe[:, half:]).astype(BF16), w1_ref[0, half:, :].astype(BF16))
    pre = p + pltpu.roll(q, n - 1, 0)
    g = _gelu_tanh(pre)
    o_ref[0] = _dot(g.astype(BF16), w2_ref[0].astype(BF16)).astype(o_ref.dtype)


def _compress(a4, pe, w1, w2):
    four, n, wdt = a4.shape
    return pl.pallas_call(
        _compress_kernel,
        grid=(four,),
        in_specs=[
            pl.BlockSpec((1, n, wdt), lambda j: (j, 0, 0)),
            pl.BlockSpec((1, 1, 2 * wdt), lambda j: (j // NSA_GROUPS, 0, 0)),
            pl.BlockSpec((1, 2 * wdt, CMP_HID), lambda j: (j // NSA_GROUPS, 0, 0)),
            pl.BlockSpec((1, CMP_HID, NSA_HD), lambda j: (j // NSA_GROUPS, 0, 0)),
        ],
        out_specs=pl.BlockSpec((1, n, NSA_HD), lambda j: (j, 0, 0)),
        out_shape=jax.ShapeDtypeStruct((four, n, NSA_HD), BF16),
        compiler_params=_cparams(("parallel",)),
        name="nsa_compress",
    )(a4, pe, w1, w2)


def _nsa_kernel(q_ref, kc_ref, vc_ref, ks_ref, vs_ref, kw_ref, vw_ref, gate_ref, ov_ref, et_ref,
                o_ref, *, n_slc):
    i = pl.program_id(1)
    rows = NSA_HPG * Q_BLOCK
    scale = NSA_HD ** -0.5
    qt = q_ref[...]
    qs = jnp.concatenate([qt[:, h * NSA_HD:(h + 1) * NSA_HD] for h in range(NSA_HPG)], axis=0)
    t = i * Q_BLOCK + (lax.broadcasted_iota(jnp.int32, (rows, 1), 0) & (Q_BLOCK - 1))
    t1 = t[:Q_BLOCK]

    kc = kc_ref[0]
    ncp = kc.shape[0]
    s_c = _dot_nt(qs, kc) * scale
    cidx = lax.broadcasted_iota(jnp.int32, (rows, ncp), 1)
    p_c = _masked_softmax(s_c, cidx * CMP_STRIDE + (CMP_LEN - 1) <= t)
    o_c = _dot(p_c.astype(BF16), vc_ref[0])

    psum = (p_c[0:Q_BLOCK] + p_c[Q_BLOCK:2 * Q_BLOCK]
            + p_c[2 * Q_BLOCK:3 * Q_BLOCK] + p_c[3 * Q_BLOCK:4 * Q_BLOCK])
    p_hi = psum.astype(BF16)
    p_lo = (psum - p_hi.astype(F32)).astype(BF16)
    ov = ov_ref[...]
    imp = _dot(p_hi, ov) + _dot(p_lo, ov)
    wsl = imp.shape[1]
    bid = lax.broadcasted_iota(jnp.int32, (Q_BLOCK, wsl), 1)
    cur = t1 // SLC_LEN
    forced = (bid == 0) | (bid == cur) | (bid == cur - 1)
    valid = bid * SLC_LEN <= t1
    score = jnp.where(forced, 1e6, jnp.where(valid, imp, -1.0))
    score = jnp.where(bid < n_slc, score, -3.0)
    nselb = jnp.full((Q_BLOCK, wsl), -SEL_BIG, F32)
    bidf = bid.astype(F32)
    for _ in range(SLC_TOPK):
        mx = jnp.max(score, axis=-1, keepdims=True)
        first = jnp.min(jnp.where(score == mx, bidf, float(wsl)), axis=-1, keepdims=True)
        hit = bidf == first
        nselb = jnp.where(hit, 0.0, nselb)
        score = jnp.where(hit, -2.0, score)

    tk = et_ref.shape[0]
    bpt = tk // SLC_LEN
    et = et_ref[...]
    lane = lax.broadcasted_iota(jnp.int32, (Q_BLOCK, 128), 1)

    def sel_body(j, carry):
        m, l, acc = carry
        k0 = pl.multiple_of(j * tk, tk)
        kt = ks_ref[pl.ds(k0, tk), :]
        vt = vs_ref[pl.ds(k0, tk), :]
        shift = lax.rem(wsl - j * bpt, wsl)
        nb = pltpu.roll(nselb, shift, 1)[:, :128]
        nb = jnp.where(lane < bpt, nb, 0.0).astype(BF16)
        qa = jnp.concatenate([qs, jnp.concatenate([nb] * NSA_HPG, axis=0)], axis=1)
        ka = jnp.concatenate([kt, et], axis=1)
        s = _dot_nt(qa, ka) * scale
        kpos = k0 + lax.broadcasted_iota(jnp.int32, (1, tk), 1)
        s = jnp.where(kpos <= t, s, NEG_INF)
        m_new = jnp.maximum(m, jnp.max(s, axis=-1, keepdims=True))
        alpha = jnp.exp(m - m_new)
        p = jnp.exp(s - m_new)
        l = alpha * l + jnp.sum(p, axis=-1, keepdims=True)
        acc = alpha * acc + _dot(p.astype(BF16), vt)
        return m_new, l, acc

    n_t = (i * Q_BLOCK + Q_BLOCK + tk - 1) // tk
    init = (jnp.full((rows, 1), NEG_INF, F32), jnp.zeros((rows, 1), F32),
            jnp.zeros((rows, NSA_HD), F32))
    _, l_s, acc_s = lax.fori_loop(0, n_t, sel_body, init)
    o_s = acc_s / l_s

    wlen = WINDOW + Q_BLOCK
    ks0 = pl.multiple_of(jnp.maximum(i - WINDOW // Q_BLOCK, 0) * Q_BLOCK, Q_BLOCK)
    s_w = _dot_nt(qs, kw_ref[pl.ds(ks0, wlen), :]) * scale
    kpos = ks0 + lax.broadcasted_iota(jnp.int32, (1, wlen), 1)
    p_w = _masked_softmax(s_w, (kpos <= t) & (kpos > t - WINDOW))
    o_w = _dot(p_w.astype(BF16), vw_ref[pl.ds(ks0, wlen), :])

    gates = _sigmoid(gate_ref[...])
    for h in range(NSA_HPG):
        r = slice(h * Q_BLOCK, (h + 1) * Q_BLOCK)
        out = (gates[:, 3 * h:3 * h + 1] * o_c[r] + gates[:, 3 * h + 1:3 * h + 2] * o_s[r]
               + gates[:, 3 * h + 2:3 * h + 3] * o_w[r])
        o_ref[:, h * NSA_HD:(h + 1) * NSA_HD] = out.astype(o_ref.dtype)


def _nsa_attention(projb, projf, kcv, single_buffer=True):
    s = projb.shape[0]
    n_slc = s // SLC_LEN
    assert n_slc >= SLC_TOPK and s >= WINDOW + Q_BLOCK
    ncp = kcv.shape[1]
    wsl = max(n_slc, 128)
    tk = min(SEL_TILE, s)
    hd = NSA_HD
    c_start = jnp.arange(ncp)[:, None] * CMP_STRIDE
    s_start = jnp.arange(wsl)[None, :] * SLC_LEN
    ov = ((c_start < s_start + SLC_LEN) & (c_start + CMP_LEN > s_start)
          & (jnp.arange(wsl)[None, :] < n_slc)).astype(BF16)
    et = (jnp.arange(tk)[:, None] // SLC_LEN == jnp.arange(128)[None, :]).astype(BF16)
    resident = dict(pipeline_mode=pl.Buffered(1)) if single_buffer else {}
    kvb = _B_KV // hd
    gb = _F_SMALL // 128
    return pl.pallas_call(
        functools.partial(_nsa_kernel, n_slc=n_slc),
        grid=(NSA_GROUPS, s // Q_BLOCK),
        in_specs=[
            pl.BlockSpec((Q_BLOCK, NSA_HPG * hd), lambda g, i: (i, g)),
            pl.BlockSpec((1, ncp, hd), lambda g, i: (g, 0, 0)),
            pl.BlockSpec((1, ncp, hd), lambda g, i: (NSA_GROUPS + g, 0, 0)),
            pl.BlockSpec((s, hd), lambda g, i: (0, kvb + g), **resident),
            pl.BlockSpec((s, hd), lambda g, i: (0, kvb + 2 + g), **resident),
            pl.BlockSpec((s, hd), lambda g, i: (0, kvb + 4 + g), **resident),
            pl.BlockSpec((s, hd), lambda g, i: (0, kvb + 6 + g), **resident),
            pl.BlockSpec((Q_BLOCK, 128), lambda g, i: (i, gb + g)),
            pl.BlockSpec((ncp, wsl), lambda g, i: (0, 0)),
            pl.BlockSpec((tk, 128), lambda g, i: (0, 0)),
        ],
        out_specs=pl.BlockSpec((Q_BLOCK, NSA_HPG * hd), lambda g, i: (i, g)),
        out_shape=jax.ShapeDtypeStruct((s, NSA_HEADS * hd), BF16),
        compiler_params=_cparams(("parallel", "arbitrary")),
        name="nsa_attention",
    )(projb, kcv, kcv, projb, projb, projb, projb, projf, ov, et)


def _mlstm_kernel(q_ref, k_ref, v_ref, o0_ref, o1_ref, gc_ref, gr_ref, cw_ref, cb_ref, ng_ref,
                  y_ref, extq, extk, c_st, n_st, m_st, *, cpb):
    step = pl.program_id(0)
    rows = cpb * ML_CHUNK
    half = ML_HEADS * ML_DQK
    L = ML_CHUNK

    @pl.when(step == 0)
    def _():
        extq[0:8, :] = jnp.zeros((8, half), F32)
        extk[0:8, :] = jnp.zeros((8, half), F32)
        c_st[...] = jnp.zeros_like(c_st)
        n_st[...] = jnp.zeros_like(n_st)
        m_st[...] = jnp.zeros_like(m_st)

    extq[8:8 + rows, :] = q_ref[...]
    extk[8:8 + rows, :] = k_ref[...]
    cw = cw_ref[...]
    cb = cb_ref[...]

    def conv(ext, lo):
        acc = cb[:, lo:lo + half]
        for j in range(ML_CONV):
            off = 8 - (ML_CONV - 1) + j
            acc = acc + cw[j:j + 1, lo:lo + half] * ext[off:off + rows, :]
        return _silu(acc)

    qa = conv(extq, 0)
    ka = conv(extk, half) * (ML_DQK ** -0.5)
    extq[0:8, :] = extq[rows:rows + 8, :]
    extk[0:8, :] = extk[rows:rows + 8, :]

    gcol = gc_ref[...]
    grow = gr_ref[...]
    lf_col_all = _log_sigmoid(gcol[:, ML_HEADS:])
    lf_row_all = _log_sigmoid(grow[ML_HEADS:, :])
    ri = lax.broadcasted_iota(jnp.int32, (L, L), 0)
    ci = lax.broadcasted_iota(jnp.int32, (L, L), 1)
    tri = ri >= ci
    ng = ng_ref[...]

    for c in range(cpb):
        r0 = c * L
        for h in range(ML_HEADS):
            qh = qa[r0:r0 + L, h * ML_DQK:(h + 1) * ML_DQK]
            kh = ka[r0:r0 + L, h * ML_DQK:(h + 1) * ML_DQK]
            vh = v_ref[r0:r0 + L, h * ML_DV:(h + 1) * ML_DV]
            ig_col = gcol[r0:r0 + L, h:h + 1]
            ig_row = grow[h:h + 1, r0:r0 + L]
            lf_col = lf_col_all[r0:r0 + L, h:h + 1]
            lf_row = lf_row_all[h:h + 1, r0:r0 + L]
            bcum_col = jnp.sum(jnp.where(tri, lf_row, 0.0), axis=1, keepdims=True)
            bcum_row = jnp.sum(jnp.where(ri <= ci, lf_col, 0.0), axis=0, keepdims=True)
            b_last = jnp.sum(lf_row, axis=1, keepdims=True)
            m_old = m_st[h][:, 0:1]
            ct = c_st[h]
            nrow = n_st[h]

            dmat = jnp.where(tri, bcum_col - bcum_row + ig_row, NEG_INF)
            inter = bcum_col + m_old
            m_t = jnp.maximum(inter, jnp.max(dmat, axis=-1, keepdims=True))
            w_inter = jnp.exp(inter - m_t)
            qb = qh.astype(BF16)
            smat = _dot_nt(qb, kh.astype(BF16)) * jnp.exp(dmat - m_t)
            num = w_inter * _dot(qb, ct.astype(BF16)) + _dot(smat.astype(BF16), vh)
            den = (w_inter * jnp.sum(qh * nrow, axis=-1, keepdims=True)
                   + jnp.sum(smat, axis=-1, keepdims=True))
            hout = num / jnp.maximum(jnp.abs(den), jnp.exp(-m_t))

            a_col = b_last - bcum_col + ig_col
            a_row = b_last - bcum_row + ig_row
            m_new = jnp.maximum(b_last + m_old, jnp.max(a_row, axis=-1, keepdims=True))
            decay = jnp.exp(b_last + m_old - m_new)
            kw = kh * jnp.exp(a_col - m_new)
            c_st[h] = decay * ct + _dot_tn(kw.astype(BF16), vh)
            n_st[h] = decay * nrow + jnp.sum(kw, axis=0, keepdims=True)
            m_st[h] = jnp.broadcast_to(m_new, (1, 128))

            o_ref = o0_ref if h < ML_HEADS // 2 else o1_ref
            oc = (h % (ML_HEADS // 2)) * ML_DV
            og = _sigmoid(o_ref[r0:r0 + L, oc:oc + ML_DV])
            yn = _ln_rows(hout) * ng[:, h * ML_DV:(h + 1) * ML_DV]
            y_ref[r0:r0 + L, h * ML_DV:(h + 1) * ML_DV] = (yn * og).astype(y_ref.dtype)


def _mlstm(projb, projf, gates_col, gates_row, conv_w, conv_b, norm_g, cpb):
    s = projb.shape[0]
    rows = cpb * ML_CHUNK
    half = ML_HEADS * ML_DQK
    dv = ML_HEADS * ML_DV
    qblk = _F_MLQK // half
    oblk = _F_MLO // half
    return pl.pallas_call(
        functools.partial(_mlstm_kernel, cpb=cpb),
        grid=(s // rows,),
        in_specs=[
            pl.BlockSpec((rows, half), lambda j: (j, qblk)),
            pl.BlockSpec((rows, half), lambda j: (j, qblk + 1)),
            pl.BlockSpec((rows, dv), lambda j: (j, _B_MLV // dv)),
            pl.BlockSpec((rows, half), lambda j: (j, oblk)),
            pl.BlockSpec((rows, half), lambda j: (j, oblk + 1)),
            pl.BlockSpec((rows, 2 * ML_HEADS), lambda j: (j, 0)),
            pl.BlockSpec((2 * ML_HEADS, rows), lambda j: (0, j)),
            pl.BlockSpec((ML_CONV, 2 * half), lambda j: (0, 0)),
            pl.BlockSpec((1, 2 * half), lambda j: (0, 0)),
            pl.BlockSpec((1, dv), lambda j: (0, 0)),
        ],
        out_specs=pl.BlockSpec((rows, dv), lambda j: (j, 0)),
        out_shape=jax.ShapeDtypeStruct((s, dv), BF16),
        scratch_shapes=[
            pltpu.VMEM((rows + 8, half), F32),
            pltpu.VMEM((rows + 8, half), F32),
            pltpu.VMEM((ML_HEADS, ML_DQK, ML_DV), F32),
            pltpu.VMEM((ML_HEADS, 1, ML_DQK), F32),
            pltpu.VMEM((ML_HEADS, 1, 128), F32),
        ],
        compiler_params=_cparams(("arbitrary",)),
        name="mlstm",
    )(projf, projf, projb, projf, projf, gates_col, gates_row, conv_w, conv_b, norm_g)


def _merge_kernel(yn_ref, ym_ref, g0_ref, g1_ref, x_ref, wn_ref, wm_ref, wo_ref, gate_ref,
                  lg_ref, lb_ref, o_ref):
    a = _dot(yn_ref[...], wn_ref[...])
    b = _dot(ym_ref[...], wm_ref[...])
    merged = _sigmoid(g0_ref[...]) * a + _sigmoid(g1_ref[...]) * b
    y = _dot(merged.astype(BF16), wo_ref[...])
    z = ALPHA * x_ref[...] + gate_ref[...] * y
    o_ref[...] = _ln_rows(z) * lg_ref[...] + lb_ref[...]


def _merge_outproj(y_nsa, y_ml, projf, x, wn, wm, wo, gate, ln_g, ln_b, tm, single_buffer=True):
    s, d = x.shape
    resident = dict(pipeline_mode=pl.Buffered(1)) if single_buffer else {}
    mb = _F_MERGE // d
    row = lambda i: (0, 0)
    return pl.pallas_call(
        _merge_kernel,
        grid=(s // tm,),
        in_specs=[
            pl.BlockSpec((tm, y_nsa.shape[1]), lambda i: (i, 0)),
            pl.BlockSpec((tm, y_ml.shape[1]), lambda i: (i, 0)),
            pl.BlockSpec((tm, d), lambda i: (i, mb)),
            pl.BlockSpec((tm, d), lambda i: (i, mb + 1)),
            pl.BlockSpec((tm, d), lambda i: (i, 0)),
            pl.BlockSpec(wn.shape, row, **resident),
            pl.BlockSpec(wm.shape, row, **resident),
            pl.BlockSpec(wo.shape, row, **resident),
            pl.BlockSpec((1, d), row),
            pl.BlockSpec((1, d), row),
            pl.BlockSpec((1, d), row),
        ],
        out_specs=pl.BlockSpec((tm, d), lambda i: (i, 0)),
        out_shape=jax.ShapeDtypeStruct((s, d), F32),
        compiler_params=_cparams(("parallel",)),
        name="merge_outproj",
    )(y_nsa, y_ml, projf, projf, x, wn, wm, wo, gate, ln_g, ln_b)


def _ffn_kernel(x_ref, xh_ref, sc_ref, sh_ref, gate_ref, wa_ref, wg_ref, cw_ref, cb_ref, wd_ref,
                lg_ref, lb_ref, o_ref, h_scr, a_scr, acc, *, halo):
    i = pl.program_id(0)
    f = pl.program_id(1)
    tm = x_ref.shape[0]

    @pl.when(f == 0)
    def _():
        mod = lambda v: (_ln_rows(v) * (1.0 + sc_ref[...]) + sh_ref[...]).astype(BF16)
        h_scr[0:halo, :] = mod(xh_ref[...])
        h_scr[halo:halo + tm, :] = mod(x_ref[...])
        acc[...] = jnp.zeros_like(acc)

    hx = h_scr[...]
    a_ext = _dot(hx, wa_ref[...])
    rid = lax.broadcasted_iota(jnp.int32, (halo + tm, 1), 0)
    a_scr[...] = jnp.where((rid >= halo) | (i > 0), a_ext, 0.0)
    cw = cw_ref[...]
    conv = cb_ref[...]
    for j in range(FFN_CONV):
        off = halo - (FFN_CONV - 1) + j
        conv = conv + cw[j:j + 1, :] * a_scr[off:off + tm, :]
    g = _dot(h_scr[halo:halo + tm, :], wg_ref[...])
    act = (_silu(conv) * g).astype(BF16)
    acc[...] += _dot(act, wd_ref[...])

    @pl.when(f == pl.num_programs(1) - 1)
    def _():
        z = ALPHA * x_ref[...] + gate_ref[...] * acc[...]
        o_ref[...] = _ln_rows(z) * lg_ref[...] + lb_ref[...]


def _conv_ffn(x, sc, sh, gate, w_up, conv_w, conv_b, w_down, ln_g, ln_b, tm, tf):
    s, d = x.shape
    dff = w_down.shape[0]
    halo = 16
    nf = dff // tf
    row = lambda i, f: (0, 0)
    return pl.pallas_call(
        functools.partial(_ffn_kernel, halo=halo),
        grid=(s // tm, nf),
        in_specs=[
            pl.BlockSpec((tm, d), lambda i, f: (i, 0)),
            pl.BlockSpec((halo, d), lambda i, f: (jnp.maximum(i * (tm // halo) - 1, 0), 0)),
            pl.BlockSpec((1, d), row),
            pl.BlockSpec((1, d), row),
            pl.BlockSpec((1, d), row),
            pl.BlockSpec((d, tf), lambda i, f: (0, f)),
            pl.BlockSpec((d, tf), lambda i, f: (0, nf + f)),
            pl.BlockSpec((FFN_CONV, tf), lambda i, f: (0, f)),
            pl.BlockSpec((1, tf), lambda i, f: (0, f)),
            pl.BlockSpec((tf, d), lambda i, f: (f, 0)),
            pl.BlockSpec((1, d), row),
            pl.BlockSpec((1, d), row),
        ],
        out_specs=pl.BlockSpec((tm, d), lambda i, f: (i, 0)),
        out_shape=jax.ShapeDtypeStruct((s, d), F32),
        scratch_shapes=[
            pltpu.VMEM((halo + tm, d), BF16),
            pltpu.VMEM((halo + tm, tf), F32),
            pltpu.VMEM((tm, d), F32),
        ],
        compiler_params=_cparams(("parallel", "arbitrary")),
        name="conv_ffn",
    )(x, x, sc, sh, gate, w_up, w_up, conv_w, conv_b, w_down, ln_g, ln_b)


def _token_mixer(x, sc, sh, gate, w_in, cmp_pe, cmp_w1, cmp_w2, ml_conv_w, ml_conv_b, ml_gate_b,
                 ml_norm_g, w_br_nsa, w_br_ml, w_o, ln_g, ln_b):
    s, d = x.shape
    wb, wf = _split_w_in(w_in)
    tm = min(1024, s)
    projb = _inproj(x, sc, sh, wb, BF16, tm, 1024)
    projf = _inproj(x, sc, sh, wf, F32, tm, 1024)
    nch = s // CMP_STRIDE
    a4 = projf[:, _F_CMP:_F_CMP + _CMP_COLS].reshape(nch, CMP_STRIDE, 2 * NSA_GROUPS, NSA_HD)
    a4 = a4.transpose(2, 0, 1, 3).reshape(2 * NSA_GROUPS, nch, CMP_STRIDE * NSA_HD)
    kcv = _compress(a4, cmp_pe.reshape(2, 1, CMP_LEN * NSA_HD), cmp_w1, cmp_w2)
    y_nsa = _nsa_attention(projb, projf, kcv)
    ifo = _F_SMALL + 2 * 128
    gates_col = projf[:, ifo:ifo + _ML_IF] + ml_gate_b[None, :]
    y_ml = _mlstm(projb, projf, gates_col, gates_col.T, ml_conv_w, ml_conv_b[None, :],
                  ml_norm_g[None, :], cpb=min(4, s // ML_CHUNK))
    return _merge_outproj(y_nsa, y_ml, projf, x, w_br_nsa.astype(BF16), w_br_ml.astype(BF16),
                          w_o.astype(BF16), gate, ln_g, ln_b, tm=min(256, s))


def _forward(x, c, w_ada, b_ada, w_in, cmp_pe, cmp_w1, cmp_w2, ml_conv_w, ml_conv_b, ml_gate_b,
             ml_norm_g, w_br_nsa, w_br_ml, w_o, w_up, ffn_conv_w, ffn_conv_b, w_down, ln_g, ln_b):
    b, s, d = x.shape
    assert b == 1 and d == D_MODEL
    depth = w_ada.shape[0]
    mod = _modulation(c, w_ada, b_ada)
    xs = x[0]
    for l in range(depth):
        sh1, sc1, g1, sh2, sc2, g2 = [mod[l, :, k * d:(k + 1) * d] for k in range(6)]
        xs = _token_mixer(xs, sc1, sh1, g1, w_in[l], cmp_pe[l], cmp_w1[l], cmp_w2[l], ml_conv_w[l],
                          ml_conv_b[l], ml_gate_b[l], ml_norm_g[l], w_br_nsa[l], w_br_ml[l], w_o[l],
                          ln_g[l, 0][None, :], ln_b[l, 0][None, :])
        xs = _conv_ffn(xs, sc2, sh2, g2, w_up[l].astype(BF16), ffn_conv_w[l], ffn_conv_b[l][None, :],
                       w_down[l].astype(BF16), ln_g[l, 1][None, :], ln_b[l, 1][None, :],
                       tm=min(512, s), tf=512)
    return xs[None]


def kernel(x, c, w_ada, b_ada, w_in, cmp_pe, cmp_w1, cmp_w2, ml_conv_w, ml_conv_b, ml_gate_b, ml_norm_g, w_br_nsa, w_br_ml, w_o, w_up, ffn_conv_w, ffn_conv_b, w_down, ln_g, ln_b):
    return _forward(x, c, w_ada, b_ada, w_in, cmp_pe, cmp_w1, cmp_w2, ml_conv_w, ml_conv_b,
                    ml_gate_b, ml_norm_g, w_br_nsa, w_br_ml, w_o, w_up, ffn_conv_w, ffn_conv_b,
                    w_down, ln_g, ln_b)
```

```python
import functools
import math

import jax
import jax.numpy as jnp
from jax import lax
from jax.experimental import pallas as pl
from jax.experimental.pallas import tpu as pltpu

F32 = jnp.float32
BF16 = jnp.bfloat16

D_MODEL = 2048
DEPTH = 2
NSA_HEADS = 8
NSA_GROUPS = 2
NSA_HPG = NSA_HEADS // NSA_GROUPS
NSA_HD = 128
CMP_LEN = 32
CMP_STRIDE = 16
CMP_HID = 256
SLC_LEN = 64
SLC_TOPK = 16
WINDOW = 512
Q_BLOCK = 128
ML_HEADS = 4
ML_DQK = 128
ML_DV = 256
ML_CHUNK = 64
ML_CONV = 4
D_FF = 5632
FFN_CONV = 3
ALPHA = (2 * DEPTH) ** 0.25
LN_EPS = 1e-5
NEG_INF = -1e30

V7X_VMEM_BYTES = 64 * 1024 * 1024
VMEM_LIMIT = 56 * 1024 * 1024

_NSA_Q = NSA_HEADS * NSA_HD
_NSA_KV = 3 * 2 * NSA_GROUPS * NSA_HD
_NSA_G = 3 * NSA_HEADS
_ML_QK = 2 * ML_HEADS * ML_DQK
_ML_V = ML_HEADS * ML_DV
_ML_IF = 2 * ML_HEADS
_CMP_COLS = 2 * NSA_GROUPS * NSA_HD
_F_MERGE = 0
_F_CMP = 2 * D_MODEL
_F_MLQK = _F_CMP + _CMP_COLS
_F_MLO = _F_MLQK + _ML_QK
_F_SMALL = _F_MLO + _ML_V
_F_COLS = _F_SMALL + 4 * 128
_B_Q = 0
_B_KV = _NSA_Q
_B_MLV = _B_KV + 8 * NSA_HD
_B_COLS = _B_MLV + _ML_V

SEL_TILE = 512
SEL_BIG_TILES = 4
SEL_BIG = 2.0 ** 60
SEL_EXP2_GUARD = 64.0


def _cparams(sem):
    return pltpu.CompilerParams(dimension_semantics=sem, vmem_limit_bytes=VMEM_LIMIT)


def _ln_rows(x):
    mu = jnp.mean(x, axis=-1, keepdims=True)
    xc = x - mu
    var = jnp.mean(xc * xc, axis=-1, keepdims=True)
    return xc * lax.rsqrt(var + LN_EPS)


def _sigmoid(x):
    return 1.0 / (1.0 + jnp.exp(-x))


def _silu(x):
    return x * _sigmoid(x)


def _gelu_tanh(x):
    c = math.sqrt(2.0 / math.pi)
    return x * (0.5 * (1.0 + jnp.tanh(c * (x + 0.044715 * (x * x * x)))))


def _log_sigmoid(x):
    return jnp.minimum(x, 0.0) - jnp.log1p(jnp.exp(-jnp.abs(x)))


def _dot(a, b):
    return jnp.dot(a, b, preferred_element_type=F32)


def _dot_nt(a, b):
    return lax.dot_general(a, b, (((1,), (1,)), ((), ())), preferred_element_type=F32)


def _dot_tn(a, b):
    return lax.dot_general(a, b, (((0,), (0,)), ((), ())), preferred_element_type=F32)


def _masked_softmax(raw, mask, scale):
    raw = jnp.where(mask, raw, NEG_INF)
    m = jnp.max(raw, axis=-1, keepdims=True)
    e = jnp.exp2((raw - m) * (scale * math.log2(math.e)))
    den = jnp.sum(e, axis=-1, keepdims=True)
    return e * jnp.where(m > 0.5 * NEG_INF, 1.0 / den, 0.0)


def _mod_kernel(c_ref, w_ref, b_ref, o_ref):
    ca = _silu(c_ref[...])
    o = _dot(ca.astype(BF16), w_ref[0].astype(BF16))
    o_ref[0] = o[0:1] + b_ref[0]


def _modulation(c, w_ada, b_ada):
    depth, d, n = w_ada.shape
    tn = 1024
    c8 = jnp.broadcast_to(c, (8, d))
    return pl.pallas_call(
        _mod_kernel,
        grid=(depth, n // tn),
        in_specs=[
            pl.BlockSpec((8, d), lambda l, j: (0, 0)),
            pl.BlockSpec((1, d, tn), lambda l, j: (l, 0, j)),
            pl.BlockSpec((1, 1, tn), lambda l, j: (l, 0, j)),
        ],
        out_specs=pl.BlockSpec((1, 1, tn), lambda l, j: (l, 0, j)),
        out_shape=jax.ShapeDtypeStruct((depth, 1, n), F32),
        compiler_params=_cparams(("parallel", "parallel")),
        name="adaln_mod",
    )(c8, w_ada, b_ada.reshape(depth, 1, n))


def _inproj_kernel(x_ref, sc_ref, sh_ref, w_ref, o_ref, h_scr):
    @pl.when(pl.program_id(1) == 0)
    def _():
        h = _ln_rows(x_ref[...]) * (1.0 + sc_ref[...]) + sh_ref[...]
        h_scr[...] = h.astype(BF16)

    o_ref[...] = _dot(h_scr[...], w_ref[...]).astype(o_ref.dtype)


def _inproj(x, sc, sh, w, out_dtype, tm, tn):
    s, d = x.shape
    n = w.shape[1]
    return pl.pallas_call(
        _inproj_kernel,
        grid=(s // tm, n // tn),
        in_specs=[
            pl.BlockSpec((tm, d), lambda i, j: (i, 0)),
            pl.BlockSpec((1, d), lambda i, j: (0, 0)),
            pl.BlockSpec((1, d), lambda i, j: (0, 0)),
            pl.BlockSpec((d, tn), lambda i, j: (0, j)),
        ],
        out_specs=pl.BlockSpec((tm, tn), lambda i, j: (i, j)),
        out_shape=jax.ShapeDtypeStruct((s, n), out_dtype),
        scratch_shapes=[pltpu.VMEM((tm, d), BF16)],
        compiler_params=_cparams(("parallel", "arbitrary")),
        name="in_proj",
    )(x, sc, sh, w)


def _split_w_in(w):
    d = w.shape[0]
    o = 0
    q = w[:, o:o + _NSA_Q]; o += _NSA_Q
    kv = w[:, o:o + _NSA_KV]; o += _NSA_KV
    g = w[:, o:o + _NSA_G]; o += _NSA_G
    mlqk = w[:, o:o + _ML_QK]; o += _ML_QK
    mlv = w[:, o:o + _ML_V]; o += _ML_V
    mlif = w[:, o:o + _ML_IF]; o += _ML_IF
    mlo = w[:, o:o + _ML_V]; o += _ML_V
    merge = w[:, o:o + 2 * D_MODEL]
    per_g = 3 * NSA_HPG
    z = lambda n: jnp.zeros((d, n), w.dtype)
    wb = jnp.concatenate([q, kv[:, _CMP_COLS:], mlv], axis=1)
    wf = jnp.concatenate(
        [merge, kv[:, :_CMP_COLS], mlqk, mlo,
         g[:, :per_g], z(128 - per_g), g[:, per_g:], z(128 - per_g), mlif, z(128 - _ML_IF), z(128)],
        axis=1)
    return wb.astype(BF16), wf.astype(BF16)


def _compress_kernel(a_ref, pe_ref, w1_ref, w2_ref, o_ref):
    a = a_ref[0]
    pe = pe_ref[0]
    half = a.shape[1]
    n = a.shape[0]
    p = _dot((a + pe[:, :half]).astype(BF16), w1_ref[0, :half, :].astype(BF16))
    q = _dot((a + pe[:, half:]).astype(BF16), w1_ref[0, half:, :].astype(BF16))
    pre = p + pltpu.roll(q, n - 1, 0)
    g = _gelu_tanh(pre)
    o_ref[0] = _dot(g.astype(BF16), w2_ref[0].astype(BF16)).astype(o_ref.dtype)


def _compress(a4, pe, w1, w2):
    four, n, wdt = a4.shape
    return pl.pallas_call(
        _compress_kernel,
        grid=(four,),
        in_specs=[
            pl.BlockSpec((1, n, wdt), lambda j: (j, 0, 0)),
            pl.BlockSpec((1, 1, 2 * wdt), lambda j: (j // NSA_GROUPS, 0, 0)),
            pl.BlockSpec((1, 2 * wdt, CMP_HID), lambda j: (j // NSA_GROUPS, 0, 0)),
            pl.BlockSpec((1, CMP_HID, NSA_HD), lambda j: (j // NSA_GROUPS, 0, 0)),
        ],
        out_specs=pl.BlockSpec((1, n, NSA_HD), lambda j: (j, 0, 0)),
        out_shape=jax.ShapeDtypeStruct((four, n, NSA_HD), BF16),
        compiler_params=_cparams(("parallel",)),
        name="nsa_compress",
    )(a4, pe, w1, w2)


def _nsa_kernel(q_ref, kc_ref, vc_ref, ks_ref, vs_ref, kw_ref, vw_ref, gate_ref, ovt_ref, eye_ref,
                o_ref, nsel_scr, *, n_slc):
    i = pl.program_id(1)
    rows = NSA_HPG * Q_BLOCK
    scale = NSA_HD ** -0.5
    qt = q_ref[...]
    qs = jnp.concatenate([qt[:, h * NSA_HD:(h + 1) * NSA_HD] for h in range(NSA_HPG)], axis=0)
    t = i * Q_BLOCK + (lax.broadcasted_iota(jnp.int32, (rows, 1), 0) & (Q_BLOCK - 1))

    kc = kc_ref[0]
    ncp = kc.shape[0]
    cidx = lax.broadcasted_iota(jnp.int32, (rows, ncp), 1)
    last_c = (t - (CMP_LEN - 1)) // CMP_STRIDE
    p_c = _masked_softmax(_dot_nt(qs, kc), cidx <= last_c, scale)
    o_c = _dot(p_c.astype(BF16), vc_ref[0])

    psum = (p_c[0:Q_BLOCK] + p_c[Q_BLOCK:2 * Q_BLOCK]
            + p_c[2 * Q_BLOCK:3 * Q_BLOCK] + p_c[3 * Q_BLOCK:4 * Q_BLOCK])
    p_hi = psum.astype(BF16)
    p_lo = (psum - p_hi.astype(F32)).astype(BF16)
    ovt = ovt_ref[...]
    imp = _dot_nt(ovt, p_hi) + _dot_nt(ovt, p_lo)
    wsl = imp.shape[0]
    bid = lax.broadcasted_iota(jnp.int32, (wsl, Q_BLOCK), 0)
    t_row = i * Q_BLOCK + lax.broadcasted_iota(jnp.int32, (1, Q_BLOCK), 1)
    cur = t_row // SLC_LEN
    forced = (bid == 0) | (bid == cur) | (bid == cur - 1)
    valid = bid * SLC_LEN <= t_row
    score = jnp.where(forced, 1e6, jnp.where(valid, imp, -1.0))
    score = jnp.where(bid < n_slc, score, -3.0)
    nsel = jnp.full((wsl, Q_BLOCK), -SEL_BIG, F32)
    bidf = bid.astype(F32)
    for _ in range(SLC_TOPK):
        mx = jnp.max(score, axis=0, keepdims=True)
        first = jnp.min(jnp.where(score == mx, bidf, float(wsl)), axis=0, keepdims=True)
        hit = bidf == first
        nsel = jnp.where(hit, 0.0, nsel)
        score = jnp.where(hit, -2.0, score)
    nsel_scr[...] = nsel

    tk = SEL_TILE
    qa = jnp.concatenate([qs, eye_ref[...]], axis=1)
    c2 = scale * math.log2(math.e)

    def keys_aug(k0, width):
        b0 = k0 // SLC_LEN
        bias = jnp.concatenate(
            [jnp.broadcast_to(nsel_scr[pl.ds(b0 + b, 1), :], (SLC_LEN, Q_BLOCK))
             for b in range(width // SLC_LEN)], axis=0)
        return jnp.concatenate([ks_ref[pl.ds(k0, width), :], bias.astype(BF16)], axis=1)

    def vals_aug(k0, width):
        ones_col = (lax.broadcasted_iota(jnp.int32, (width, NSA_HD), 1) == 0).astype(BF16)
        return jnp.concatenate([vs_ref[pl.ds(k0, width), :], ones_col], axis=1)

    def exact_step(k0, carry, causal):
        m, acc = carry
        s = _dot_nt(qa, keys_aug(k0, tk))
        if causal:
            kpos = k0 + lax.broadcasted_iota(jnp.int32, (1, tk), 1)
            s = jnp.where(kpos <= t, s, NEG_INF)
        m_new = jnp.maximum(m, jnp.max(s, axis=-1, keepdims=True))
        p = jnp.exp2((s - m_new) * c2)
        acc = jnp.exp2((m - m_new) * c2) * acc + _dot(p.astype(BF16), vals_aug(k0, tk))
        return m_new, acc

    def fast_step(k0, width, carry):
        m, acc = carry
        k0 = pl.multiple_of(k0, tk)
        s = _dot_nt(qa, keys_aug(k0, width))
        p = jnp.exp2((s - m) * c2)
        mx = jnp.max(s, axis=-1, keepdims=True)
        m_new = jnp.maximum(m, mx)
        acc_new = (acc + _dot(p.astype(BF16), vals_aug(k0, width))) * jnp.exp2((m - m_new) * c2)

        def redo():
            c = carry
            for u in range(width // tk):
                c = exact_step(k0 + u * tk, c, False)
            return c

        return lax.cond(jnp.max((mx - m) * c2) > SEL_EXP2_GUARD, redo, lambda: (m_new, acc_new))

    n_full = (i * Q_BLOCK) // tk
    init = (jnp.full((rows, 1), NEG_INF, F32), jnp.zeros((rows, 2 * NSA_HD), F32))
    carry = exact_step(0, init, True)
    n_big = jnp.maximum(n_full - 1, 0) // SEL_BIG_TILES
    carry = lax.fori_loop(
        0, n_big, lambda jb, c: fast_step((1 + jb * SEL_BIG_TILES) * tk, SEL_BIG_TILES * tk, c), carry)
    carry = lax.fori_loop(1 + n_big * SEL_BIG_TILES, n_full, lambda j, c: fast_step(j * tk, tk, c), carry)
    _, acc_s = lax.cond(n_full > 0, lambda: exact_step(pl.multiple_of(n_full * tk, tk), carry, True),
                        lambda: carry)
    o_s = acc_s[:, :NSA_HD] / acc_s[:, NSA_HD:NSA_HD + 1]

    wlen = WINDOW + Q_BLOCK
    ks0 = pl.multiple_of(jnp.maximum(i - WINDOW // Q_BLOCK, 0) * Q_BLOCK, Q_BLOCK)
    s_w = _dot_nt(qs, kw_ref[pl.ds(ks0, wlen), :])
    kpos = ks0 + lax.broadcasted_iota(jnp.int32, (1, wlen), 1)
    p_w = _masked_softmax(s_w, (kpos <= t) & (kpos > t - WINDOW), scale)
    o_w = _dot(p_w.astype(BF16), vw_ref[pl.ds(ks0, wlen), :])

    gates = _sigmoid(gate_ref[...])
    for h in range(NSA_HPG):
        r = slice(h * Q_BLOCK, (h + 1) * Q_BLOCK)
        out = (gates[:, 3 * h:3 * h + 1] * o_c[r] + gates[:, 3 * h + 1:3 * h + 2] * o_s[r]
               + gates[:, 3 * h + 2:3 * h + 3] * o_w[r])
        o_ref[:, h * NSA_HD:(h + 1) * NSA_HD] = out.astype(o_ref.dtype)


def _nsa_attention(projb, projf, kcv, single_buffer=True):
    s = projb.shape[0]
    n_slc = s // SLC_LEN
    assert n_slc >= SLC_TOPK and s >= WINDOW + Q_BLOCK
    ncp = kcv.shape[1]
    wsl = max(n_slc, 128)
    hd = NSA_HD
    c_start = jnp.arange(ncp)[None, :] * CMP_STRIDE
    s_start = jnp.arange(wsl)[:, None] * SLC_LEN
    ovt = ((c_start < s_start + SLC_LEN) & (c_start + CMP_LEN > s_start)
           & (jnp.arange(wsl)[:, None] < n_slc)).astype(BF16)
    eye = jnp.tile(jnp.eye(Q_BLOCK, dtype=BF16), (NSA_HPG, 1))
    resident = dict(pipeline_mode=pl.Buffered(1)) if single_buffer else {}
    kvb = _B_KV // hd
    gb = _F_SMALL // 128
    return pl.pallas_call(
        functools.partial(_nsa_kernel, n_slc=n_slc),
        grid=(NSA_GROUPS, s // Q_BLOCK),
        in_specs=[
            pl.BlockSpec((Q_BLOCK, NSA_HPG * hd), lambda g, i: (i, g)),
            pl.BlockSpec((1, ncp, hd), lambda g, i: (g, 0, 0)),
            pl.BlockSpec((1, ncp, hd), lambda g, i: (NSA_GROUPS + g, 0, 0)),
            pl.BlockSpec((s, hd), lambda g, i: (0, kvb + g), **resident),
            pl.BlockSpec((s, hd), lambda g, i: (0, kvb + 2 + g), **resident),
            pl.BlockSpec((s, hd), lambda g, i: (0, kvb + 4 + g), **resident),
            pl.BlockSpec((s, hd), lambda g, i: (0, kvb + 6 + g), **resident),
            pl.BlockSpec((Q_BLOCK, 128), lambda g, i: (i, gb + g)),
            pl.BlockSpec((wsl, ncp), lambda g, i: (0, 0)),
            pl.BlockSpec((NSA_HPG * Q_BLOCK, Q_BLOCK), lambda g, i: (0, 0)),
        ],
        out_specs=pl.BlockSpec((Q_BLOCK, NSA_HPG * hd), lambda g, i: (i, g)),
        out_shape=jax.ShapeDtypeStruct((s, NSA_HEADS * hd), BF16),
        scratch_shapes=[pltpu.VMEM((wsl, Q_BLOCK), F32)],
        compiler_params=_cparams(("parallel", "arbitrary")),
        name="nsa_attention",
    )(projb, kcv, kcv, projb, projb, projb, projb, projf, ovt, eye)


def _mlstm_kernel(q_ref, k_ref, v_ref, o0_ref, o1_ref, gc_ref, gr_ref, cw_ref, cb_ref, ng_ref,
                  y_ref, extq, extk, c_st, n_st, m_st, *, cpb):
    step = pl.program_id(0)
    rows = cpb * ML_CHUNK
    half = ML_HEADS * ML_DQK
    L = ML_CHUNK

    @pl.when(step == 0)
    def _():
        extq[0:8, :] = jnp.zeros((8, half), F32)
        extk[0:8, :] = jnp.zeros((8, half), F32)
        c_st[...] = jnp.zeros_like(c_st)
        n_st[...] = jnp.zeros_like(n_st)
        m_st[...] = jnp.zeros_like(m_st)

    extq[8:8 + rows, :] = q_ref[...]
    extk[8:8 + rows, :] = k_ref[...]
    cw = cw_ref[...]
    cb = cb_ref[...]

    def conv(ext, lo):
        acc = cb[:, lo:lo + half]
        for j in range(ML_CONV):
            off = 8 - (ML_CONV - 1) + j
            acc = acc + cw[j:j + 1, lo:lo + half] * ext[off:off + rows, :]
        return _silu(acc)

    qa = conv(extq, 0)
    ka = conv(extk, half) * (ML_DQK ** -0.5)
    extq[0:8, :] = extq[rows:rows + 8, :]
    extk[0:8, :] = extk[rows:rows + 8, :]

    gcol = gc_ref[...]
    grow = gr_ref[...]
    lf_col_all = _log_sigmoid(gcol[:, ML_HEADS:])
    lf_row_all = _log_sigmoid(grow[ML_HEADS:, :])
    ri = lax.broadcasted_iota(jnp.int32, (L, L), 0)
    ci = lax.broadcasted_iota(jnp.int32, (L, L), 1)
    tri = ri >= ci
    ng = ng_ref[...]

    for c in range(cpb):
        r0 = c * L
        for h in range(ML_HEADS):
            qh = qa[r0:r0 + L, h * ML_DQK:(h + 1) * ML_DQK]
            kh = ka[r0:r0 + L, h * ML_DQK:(h + 1) * ML_DQK]
            vh = v_ref[r0:r0 + L, h * ML_DV:(h + 1) * ML_DV]
            ig_col = gcol[r0:r0 + L, h:h + 1]
            ig_row = grow[h:h + 1, r0:r0 + L]
            lf_col = lf_col_all[r0:r0 + L, h:h + 1]
            lf_row = lf_row_all[h:h + 1, r0:r0 + L]
            bcum_col = jnp.sum(jnp.where(tri, lf_row, 0.0), axis=1, keepdims=True)
            bcum_row = jnp.sum(jnp.where(ri <= ci, lf_col, 0.0), axis=0, keepdims=True)
            b_last = jnp.sum(lf_row, axis=1, keepdims=True)
            m_old = m_st[h][:, 0:1]
            ct = c_st[h]
            nrow = n_st[h]

            dmat = jnp.where(tri, bcum_col - bcum_row + ig_row, NEG_INF)
            inter = bcum_col + m_old
            m_t = jnp.maximum(inter, jnp.max(dmat, axis=-1, keepdims=True))
            w_inter = jnp.exp(inter - m_t)
            qb = qh.astype(BF16)
            smat = _dot_nt(qb, kh.astype(BF16)) * jnp.exp(dmat - m_t)
            num = w_inter * _dot(qb, ct.astype(BF16)) + _dot(smat.astype(BF16), vh)
            den = (w_inter * jnp.sum(qh * nrow, axis=-1, keepdims=True)
                   + jnp.sum(smat, axis=-1, keepdims=True))
            hout = num / jnp.maximum(jnp.abs(den), jnp.exp(-m_t))

            a_col = b_last - bcum_col + ig_col
            a_row = b_last - bcum_row + ig_row
            m_new = jnp.maximum(b_last + m_old, jnp.max(a_row, axis=-1, keepdims=True))
            decay = jnp.exp(b_last + m_old - m_new)
            kw = kh * jnp.exp(a_col - m_new)
            c_st[h] = decay * ct + _dot_tn(kw.astype(BF16), vh)
            n_st[h] = decay * nrow + jnp.sum(kw, axis=0, keepdims=True)
            m_st[h] = jnp.broadcast_to(m_new, (1, 128))

            o_ref = o0_ref if h < ML_HEADS // 2 else o1_ref
            oc = (h % (ML_HEADS // 2)) * ML_DV
            og = _sigmoid(o_ref[r0:r0 + L, oc:oc + ML_DV])
            yn = _ln_rows(hout) * ng[:, h * ML_DV:(h + 1) * ML_DV]
            y_ref[r0:r0 + L, h * ML_DV:(h + 1) * ML_DV] = (yn * og).astype(y_ref.dtype)


def _mlstm(projb, projf, gates_col, gates_row, conv_w, conv_b, norm_g, cpb):
    s = projb.shape[0]
    rows = cpb * ML_CHUNK
    half = ML_HEADS * ML_DQK
    dv = ML_HEADS * ML_DV
    qblk = _F_MLQK // half
    oblk = _F_MLO // half
    return pl.pallas_call(
        functools.partial(_mlstm_kernel, cpb=cpb),
        grid=(s // rows,),
        in_specs=[
            pl.BlockSpec((rows, half), lambda j: (j, qblk)),
            pl.BlockSpec((rows, half), lambda j: (j, qblk + 1)),
            pl.BlockSpec((rows, dv), lambda j: (j, _B_MLV // dv)),
            pl.BlockSpec((rows, half), lambda j: (j, oblk)),
            pl.BlockSpec((rows, half), lambda j: (j, oblk + 1)),
            pl.BlockSpec((rows, 2 * ML_HEADS), lambda j: (j, 0)),
            pl.BlockSpec((2 * ML_HEADS, rows), lambda j: (0, j)),
            pl.BlockSpec((ML_CONV, 2 * half), lambda j: (0, 0)),
            pl.BlockSpec((1, 2 * half), lambda j: (0, 0)),
            pl.BlockSpec((1, dv), lambda j: (0, 0)),
        ],
        out_specs=pl.BlockSpec((rows, dv), lambda j: (j, 0)),
        out_shape=jax.ShapeDtypeStruct((s, dv), BF16),
        scratch_shapes=[
            pltpu.VMEM((rows + 8, half), F32),
            pltpu.VMEM((rows + 8, half), F32),
            pltpu.VMEM((ML_HEADS, ML_DQK, ML_DV), F32),
            pltpu.VMEM((ML_HEADS, 1, ML_DQK), F32),
            pltpu.VMEM((ML_HEADS, 1, 128), F32),
        ],
        compiler_params=_cparams(("arbitrary",)),
        name="mlstm",
    )(projf, projf, projb, projf, projf, gates_col, gates_row, conv_w, conv_b, norm_g)


def _merge_kernel(yn_ref, ym_ref, g0_ref, g1_ref, x_ref, wn_ref, wm_ref, wo_ref, gate_ref,
                  lg_ref, lb_ref, o_ref):
    a = _dot(yn_ref[...], wn_ref[...])
    b = _dot(ym_ref[...], wm_ref[...])
    merged = _sigmoid(g0_ref[...]) * a + _sigmoid(g1_ref[...]) * b
    y = _dot(merged.astype(BF16), wo_ref[...])
    z = ALPHA * x_ref[...] + gate_ref[...] * y
    o_ref[...] = _ln_rows(z) * lg_ref[...] + lb_ref[...]


def _merge_outproj(y_nsa, y_ml, projf, x, wn, wm, wo, gate, ln_g, ln_b, tm, single_buffer=True):
    s, d = x.shape
    resident = dict(pipeline_mode=pl.Buffered(1)) if single_buffer else {}
    mb = _F_MERGE // d
    row = lambda i: (0, 0)
    return pl.pallas_call(
        _merge_kernel,
        grid=(s // tm,),
        in_specs=[
            pl.BlockSpec((tm, y_nsa.shape[1]), lambda i: (i, 0)),
            pl.BlockSpec((tm, y_ml.shape[1]), lambda i: (i, 0)),
            pl.BlockSpec((tm, d), lambda i: (i, mb)),
            pl.BlockSpec((tm, d), lambda i: (i, mb + 1)),
            pl.BlockSpec((tm, d), lambda i: (i, 0)),
            pl.BlockSpec(wn.shape, row, **resident),
            pl.BlockSpec(wm.shape, row, **resident),
            pl.BlockSpec(wo.shape, row, **resident),
            pl.BlockSpec((1, d), row),
            pl.BlockSpec((1, d), row),
            pl.BlockSpec((1, d), row),
        ],
        out_specs=pl.BlockSpec((tm, d), lambda i: (i, 0)),
        out_shape=jax.ShapeDtypeStruct((s, d), F32),
        compiler_params=_cparams(("parallel",)),
        name="merge_outproj",
    )(y_nsa, y_ml, projf, projf, x, wn, wm, wo, gate, ln_g, ln_b)


def _ffn_kernel(x_ref, xh_ref, sc_ref, sh_ref, gate_ref, wa_ref, wg_ref, cw_ref, cb_ref, wd_ref,
                lg_ref, lb_ref, o_ref, h_scr, a_scr, acc, *, halo):
    i = pl.program_id(0)
    f = pl.program_id(1)
    tm = x_ref.shape[0]

    @pl.when(f == 0)
    def _():
        mod = lambda v: (_ln_rows(v) * (1.0 + sc_ref[...]) + sh_ref[...]).astype(BF16)
        h_scr[0:halo, :] = mod(xh_ref[...])
        h_scr[halo:halo + tm, :] = mod(x_ref[...])
        acc[...] = jnp.zeros_like(acc)

    hx = h_scr[...]
    a_ext = _dot(hx, wa_ref[...])
    rid = lax.broadcasted_iota(jnp.int32, (halo + tm, 1), 0)
    a_scr[...] = jnp.where((rid >= halo) | (i > 0), a_ext, 0.0)
    cw = cw_ref[...]
    conv = cb_ref[...]
    for j in range(FFN_CONV):
        off = halo - (FFN_CONV - 1) + j
        conv = conv + cw[j:j + 1, :] * a_scr[off:off + tm, :]
    g = _dot(h_scr[halo:halo + tm, :], wg_ref[...])
    act = (_silu(conv) * g).astype(BF16)
    acc[...] += _dot(act, wd_ref[...])

    @pl.when(f == pl.num_programs(1) - 1)
    def _():
        z = ALPHA * x_ref[...] + gate_ref[...] * acc[...]
        o_ref[...] = _ln_rows(z) * lg_ref[...] + lb_ref[...]


def _conv_ffn(x, sc, sh, gate, w_up, conv_w, conv_b, w_down, ln_g, ln_b, tm, tf):
    s, d = x.shape
    dff = w_down.shape[0]
    halo = 16
    nf = dff // tf
    row = lambda i, f: (0, 0)
    return pl.pallas_call(
        functools.partial(_ffn_kernel, halo=halo),
        grid=(s // tm, nf),
        in_specs=[
            pl.BlockSpec((tm, d), lambda i, f: (i, 0)),
            pl.BlockSpec((halo, d), lambda i, f: (jnp.maximum(i * (tm // halo) - 1, 0), 0)),
            pl.BlockSpec((1, d), row),
            pl.BlockSpec((1, d), row),
            pl.BlockSpec((1, d), row),
            pl.BlockSpec((d, tf), lambda i, f: (0, f)),
            pl.BlockSpec((d, tf), lambda i, f: (0, nf + f)),
            pl.BlockSpec((FFN_CONV, tf), lambda i, f: (0, f)),
            pl.BlockSpec((1, tf), lambda i, f: (0, f)),
            pl.BlockSpec((tf, d), lambda i, f: (f, 0)),
            pl.BlockSpec((1, d), row),
            pl.BlockSpec((1, d), row),
        ],
        out_specs=pl.BlockSpec((tm, d), lambda i, f: (i, 0)),
        out_shape=jax.ShapeDtypeStruct((s, d), F32),
        scratch_shapes=[
            pltpu.VMEM((halo + tm, d), BF16),
            pltpu.VMEM((halo + tm, tf), F32),
            pltpu.VMEM((tm, d), F32),
        ],
        compiler_params=_cparams(("parallel", "arbitrary")),
        name="conv_ffn",
    )(x, x, sc, sh, gate, w_up, w_up, conv_w, conv_b, w_down, ln_g, ln_b)


def _token_mixer(x, sc, sh, gate, w_in, cmp_pe, cmp_w1, cmp_w2, ml_conv_w, ml_conv_b, ml_gate_b,
                 ml_norm_g, w_br_nsa, w_br_ml, w_o, ln_g, ln_b):
    s, d = x.shape
    wb, wf = _split_w_in(w_in)
    tm = min(1024, s)
    projb = _inproj(x, sc, sh, wb, BF16, tm, 1024)
    projf = _inproj(x, sc, sh, wf, F32, tm, 1024)
    nch = s // CMP_STRIDE
    a4 = projf[:, _F_CMP:_F_CMP + _CMP_COLS].reshape(nch, CMP_STRIDE, 2 * NSA_GROUPS, NSA_HD)
    a4 = a4.transpose(2, 0, 1, 3).reshape(2 * NSA_GROUPS, nch, CMP_STRIDE * NSA_HD)
    kcv = _compress(a4, cmp_pe.reshape(2, 1, CMP_LEN * NSA_HD), cmp_w1, cmp_w2)
    y_nsa = _nsa_attention(projb, projf, kcv)
    ifo = _F_SMALL + 2 * 128
    gates_col = projf[:, ifo:ifo + _ML_IF] + ml_gate_b[None, :]
    y_ml = _mlstm(projb, projf, gates_col, gates_col.T, ml_conv_w, ml_conv_b[None, :],
                  ml_norm_g[None, :], cpb=min(4, s // ML_CHUNK))
    return _merge_outproj(y_nsa, y_ml, projf, x, w_br_nsa.astype(BF16), w_br_ml.astype(BF16),
                          w_o.astype(BF16), gate, ln_g, ln_b, tm=min(256, s))


def _forward(x, c, w_ada, b_ada, w_in, cmp_pe, cmp_w1, cmp_w2, ml_conv_w, ml_conv_b, ml_gate_b,
             ml_norm_g, w_br_nsa, w_br_ml, w_o, w_up, ffn_conv_w, ffn_conv_b, w_down, ln_g, ln_b):
    b, s, d = x.shape
    assert b == 1 and d == D_MODEL
    depth = w_ada.shape[0]
    mod = _modulation(c, w_ada, b_ada)
    xs = x[0]
    for l in range(depth):
        sh1, sc1, g1, sh2, sc2, g2 = [mod[l, :, k * d:(k + 1) * d] for k in range(6)]
        xs = _token_mixer(xs, sc1, sh1, g1, w_in[l], cmp_pe[l], cmp_w1[l], cmp_w2[l], ml_conv_w[l],
                          ml_conv_b[l], ml_gate_b[l], ml_norm_g[l], w_br_nsa[l], w_br_ml[l], w_o[l],
                          ln_g[l, 0][None, :], ln_b[l, 0][None, :])
        xs = _conv_ffn(xs, sc2, sh2, g2, w_up[l].astype(BF16), ffn_conv_w[l], ffn_conv_b[l][None, :],
                       w_down[l].astype(BF16), ln_g[l, 1][None, :], ln_b[l, 1][None, :],
                       tm=min(512, s), tf=512)
    return xs[None]


def kernel(x, c, w_ada, b_ada, w_in, cmp_pe, cmp_w1, cmp_w2, ml_conv_w, ml_conv_b, ml_gate_b, ml_norm_g, w_br_nsa, w_br_ml, w_o, w_up, ffn_conv_w, ffn_conv_b, w_down, ln_g, ln_b):
    return _forward(x, c, w_ada, b_ada, w_in, cmp_pe, cmp_w1, cmp_w2, ml_conv_w, ml_conv_b,
                    ml_gate_b, ml_norm_g, w_br_nsa, w_br_ml, w_o, w_up, ffn_conv_w, ffn_conv_b,
                    w_down, ln_g, ln_b)
```

```python
import functools
import math

import jax
import jax.numpy as jnp
from jax import lax
from jax.experimental import pallas as pl
from jax.experimental.pallas import tpu as pltpu

F32 = jnp.float32
BF16 = jnp.bfloat16

D_MODEL = 2048
DEPTH = 2
NSA_HEADS = 8
NSA_GROUPS = 2
NSA_HPG = NSA_HEADS // NSA_GROUPS
NSA_HD = 128
CMP_LEN = 32
CMP_STRIDE = 16
CMP_HID = 256
SLC_LEN = 64
SLC_TOPK = 16
WINDOW = 512
Q_BLOCK = 128
ML_HEADS = 4
ML_DQK = 128
ML_DV = 256
ML_CHUNK = 64
ML_CONV = 4
D_FF = 5632
FFN_CONV = 3
ALPHA = (2 * DEPTH) ** 0.25
LN_EPS = 1e-5
NEG_INF = -1e30

V7X_VMEM_BYTES = 64 * 1024 * 1024
VMEM_LIMIT = 56 * 1024 * 1024

_NSA_Q = NSA_HEADS * NSA_HD
_NSA_KV = 3 * 2 * NSA_GROUPS * NSA_HD
_NSA_G = 3 * NSA_HEADS
_ML_QK = 2 * ML_HEADS * ML_DQK
_ML_V = ML_HEADS * ML_DV
_ML_IF = 2 * ML_HEADS
_CMP_COLS = 2 * NSA_GROUPS * NSA_HD
_F_MERGE = 0
_F_CMP = 2 * D_MODEL
_F_MLQK = _F_CMP + _CMP_COLS
_F_MLO = _F_MLQK + _ML_QK
_F_SMALL = _F_MLO + _ML_V
_F_COLS = _F_SMALL + 4 * 128
_B_Q = 0
_B_KV = _NSA_Q
_B_MLV = _B_KV + 8 * NSA_HD
_B_COLS = _B_MLV + _ML_V

SEL_TILE = 512
SEL_STEP = 4096
SEL_BIG = 2.0 ** 60
SEL_EXP2_GUARD = 64.0


def _cparams(sem):
    return pltpu.CompilerParams(dimension_semantics=sem, vmem_limit_bytes=VMEM_LIMIT)


def _ln_rows(x):
    mu = jnp.mean(x, axis=-1, keepdims=True)
    xc = x - mu
    var = jnp.mean(xc * xc, axis=-1, keepdims=True)
    return xc * lax.rsqrt(var + LN_EPS)


def _sigmoid(x):
    return 1.0 / (1.0 + jnp.exp(-x))


def _silu(x):
    return x * _sigmoid(x)


def _gelu_tanh(x):
    c = math.sqrt(2.0 / math.pi)
    return x * (0.5 * (1.0 + jnp.tanh(c * (x + 0.044715 * (x * x * x)))))


def _log_sigmoid(x):
    return jnp.minimum(x, 0.0) - jnp.log1p(jnp.exp(-jnp.abs(x)))


def _dot(a, b):
    return jnp.dot(a, b, preferred_element_type=F32)


def _dot_nt(a, b):
    return lax.dot_general(a, b, (((1,), (1,)), ((), ())), preferred_element_type=F32)


def _dot_tn(a, b):
    return lax.dot_general(a, b, (((0,), (0,)), ((), ())), preferred_element_type=F32)


def _masked_softmax(raw, mask, scale):
    raw = jnp.where(mask, raw, NEG_INF)
    m = jnp.max(raw, axis=-1, keepdims=True)
    e = jnp.exp2((raw - m) * (scale * math.log2(math.e)))
    den = jnp.sum(e, axis=-1, keepdims=True)
    return e * jnp.where(m > 0.5 * NEG_INF, 1.0 / den, 0.0)


def _mod_kernel(c_ref, w_ref, b_ref, o_ref):
    ca = _silu(c_ref[...])
    o = _dot(ca.astype(BF16), w_ref[0].astype(BF16))
    o_ref[0] = o[0:1] + b_ref[0]


def _modulation(c, w_ada, b_ada):
    depth, d, n = w_ada.shape
    tn = 1024
    c8 = jnp.broadcast_to(c, (8, d))
    return pl.pallas_call(
        _mod_kernel,
        grid=(depth, n // tn),
        in_specs=[
            pl.BlockSpec((8, d), lambda l, j: (0, 0)),
            pl.BlockSpec((1, d, tn), lambda l, j: (l, 0, j)),
            pl.BlockSpec((1, 1, tn), lambda l, j: (l, 0, j)),
        ],
        out_specs=pl.BlockSpec((1, 1, tn), lambda l, j: (l, 0, j)),
        out_shape=jax.ShapeDtypeStruct((depth, 1, n), F32),
        compiler_params=_cparams(("parallel", "parallel")),
        name="adaln_mod",
    )(c8, w_ada, b_ada.reshape(depth, 1, n))


def _inproj_kernel(x_ref, sc_ref, sh_ref, w_ref, o_ref, h_scr):
    @pl.when(pl.program_id(1) == 0)
    def _():
        h = _ln_rows(x_ref[...]) * (1.0 + sc_ref[...]) + sh_ref[...]
        h_scr[...] = h.astype(BF16)

    o_ref[...] = _dot(h_scr[...], w_ref[...]).astype(o_ref.dtype)


def _inproj(x, sc, sh, w, out_dtype, tm, tn):
    s, d = x.shape
    n = w.shape[1]
    return pl.pallas_call(
        _inproj_kernel,
        grid=(s // tm, n // tn),
        in_specs=[
            pl.BlockSpec((tm, d), lambda i, j: (i, 0)),
            pl.BlockSpec((1, d), lambda i, j: (0, 0)),
            pl.BlockSpec((1, d), lambda i, j: (0, 0)),
            pl.BlockSpec((d, tn), lambda i, j: (0, j)),
        ],
        out_specs=pl.BlockSpec((tm, tn), lambda i, j: (i, j)),
        out_shape=jax.ShapeDtypeStruct((s, n), out_dtype),
        scratch_shapes=[pltpu.VMEM((tm, d), BF16)],
        compiler_params=_cparams(("parallel", "arbitrary")),
        name="in_proj",
    )(x, sc, sh, w)


def _split_w_in(w):
    d = w.shape[0]
    o = 0
    q = w[:, o:o + _NSA_Q]; o += _NSA_Q
    kv = w[:, o:o + _NSA_KV]; o += _NSA_KV
    g = w[:, o:o + _NSA_G]; o += _NSA_G
    mlqk = w[:, o:o + _ML_QK]; o += _ML_QK
    mlv = w[:, o:o + _ML_V]; o += _ML_V
    mlif = w[:, o:o + _ML_IF]; o += _ML_IF
    mlo = w[:, o:o + _ML_V]; o += _ML_V
    merge = w[:, o:o + 2 * D_MODEL]
    per_g = 3 * NSA_HPG
    z = lambda n: jnp.zeros((d, n), w.dtype)
    wb = jnp.concatenate([q, kv[:, _CMP_COLS:], mlv], axis=1)
    wf = jnp.concatenate(
        [merge, kv[:, :_CMP_COLS], mlqk, mlo,
         g[:, :per_g], z(128 - per_g), g[:, per_g:], z(128 - per_g), mlif, z(128 - _ML_IF), z(128)],
        axis=1)
    return wb.astype(BF16), wf.astype(BF16)


def _compress_kernel(a_ref, pe_ref, w1_ref, w2_ref, o_ref):
    a = a_ref[0]
    pe = pe_ref[0]
    half = a.shape[1]
    n = a.shape[0]
    p = _dot((a + pe[:, :half]).astype(BF16), w1_ref[0, :half, :].astype(BF16))
    q = _dot((a + pe[:, half:]).astype(BF16), w1_ref[0, half:, :].astype(BF16))
    pre = p + pltpu.roll(q, n - 1, 0)
    g = _gelu_tanh(pre)
    o_ref[0] = _dot(g.astype(BF16), w2_ref[0].astype(BF16)).astype(o_ref.dtype)


def _compress(a4, pe, w1, w2):
    four, n, wdt = a4.shape
    return pl.pallas_call(
        _compress_kernel,
        grid=(four,),
        in_specs=[
            pl.BlockSpec((1, n, wdt), lambda j: (j, 0, 0)),
            pl.BlockSpec((1, 1, 2 * wdt), lambda j: (j // NSA_GROUPS, 0, 0)),
            pl.BlockSpec((1, 2 * wdt, CMP_HID), lambda j: (j // NSA_GROUPS, 0, 0)),
            pl.BlockSpec((1, CMP_HID, NSA_HD), lambda j: (j // NSA_GROUPS, 0, 0)),
        ],
        out_specs=pl.BlockSpec((1, n, NSA_HD), lambda j: (j, 0, 0)),
        out_shape=jax.ShapeDtypeStruct((four, n, NSA_HD), BF16),
        compiler_params=_cparams(("parallel",)),
        name="nsa_compress",
    )(a4, pe, w1, w2)


def _nsa_kernel(q_ref, kc_ref, vc_ref, ks_ref, vs_ref, kw_ref, vw_ref, gate_ref, ovt_ref, eye_ref,
                o_ref, nsel_scr, npast_scr, *, n_slc, step):
    i = pl.program_id(1)
    rows = NSA_HPG * Q_BLOCK
    scale = NSA_HD ** -0.5
    qt = q_ref[...]
    qs = jnp.concatenate([qt[:, h * NSA_HD:(h + 1) * NSA_HD] for h in range(NSA_HPG)], axis=0)
    t = i * Q_BLOCK + (lax.broadcasted_iota(jnp.int32, (rows, 1), 0) & (Q_BLOCK - 1))

    kc = kc_ref[0]
    ncp = kc.shape[0]
    cidx = lax.broadcasted_iota(jnp.int32, (rows, ncp), 1)
    last_c = (t - (CMP_LEN - 1)) // CMP_STRIDE
    p_c = _masked_softmax(_dot_nt(qs, kc), cidx <= last_c, scale)
    o_c = _dot(p_c.astype(BF16), vc_ref[0])

    psum = (p_c[0:Q_BLOCK] + p_c[Q_BLOCK:2 * Q_BLOCK]
            + p_c[2 * Q_BLOCK:3 * Q_BLOCK] + p_c[3 * Q_BLOCK:4 * Q_BLOCK])
    p_hi = psum.astype(BF16)
    p_lo = (psum - p_hi.astype(F32)).astype(BF16)
    ovt = ovt_ref[...]
    imp = _dot_nt(ovt, p_hi) + _dot_nt(ovt, p_lo)
    wsl = imp.shape[0]
    bid = lax.broadcasted_iota(jnp.int32, (wsl, Q_BLOCK), 0)
    t_row = i * Q_BLOCK + lax.broadcasted_iota(jnp.int32, (1, Q_BLOCK), 1)
    cur = t_row // SLC_LEN
    forced = (bid == 0) | (bid == cur) | (bid == cur - 1)
    valid = bid * SLC_LEN <= t_row
    score = jnp.where(forced, 1e6, jnp.where(valid, imp, -1.0))
    score = jnp.where(bid < n_slc, score, -3.0)
    nsel = jnp.full((wsl, Q_BLOCK), -SEL_BIG, F32)
    bidf = bid.astype(F32)
    for _ in range(SLC_TOPK):
        mx = jnp.max(score, axis=0, keepdims=True)
        first = jnp.min(jnp.where(score == mx, bidf, float(wsl)), axis=0, keepdims=True)
        hit = bidf == first
        nsel = jnp.where(hit, 0.0, nsel)
        score = jnp.where(hit, -2.0, score)
    blk_d = (i * Q_BLOCK) // SLC_LEN
    nsel_scr[...] = nsel
    npast_scr[...] = jnp.where(bid < blk_d, nsel, -SEL_BIG)

    qa = jnp.concatenate([qs, eye_ref[...]], axis=1)
    c2 = scale * math.log2(math.e)

    def keys_aug(k0, width, sel_scr):
        b0 = k0 // SLC_LEN
        bias = jnp.concatenate(
            [jnp.broadcast_to(sel_scr[pl.ds(b0 + b, 1), :], (SLC_LEN, Q_BLOCK))
             for b in range(width // SLC_LEN)], axis=0)
        return jnp.concatenate([ks_ref[pl.ds(k0, width), :], bias.astype(BF16)], axis=1)

    def vals_aug(k0, width):
        ones_col = (lax.broadcasted_iota(jnp.int32, (width, NSA_HD), 1) == 0).astype(BF16)
        return jnp.concatenate([vs_ref[pl.ds(k0, width), :], ones_col], axis=1)

    k_d = pl.multiple_of(i * Q_BLOCK, Q_BLOCK)
    s_d = _dot_nt(qa, keys_aug(k_d, Q_BLOCK, nsel_scr))
    s_d = jnp.where(k_d + lax.broadcasted_iota(jnp.int32, (1, Q_BLOCK), 1) <= t, s_d, NEG_INF)
    m_d = jnp.max(s_d, axis=-1, keepdims=True)
    acc_d = _dot(jnp.exp2((s_d - m_d) * c2).astype(BF16), vals_aug(k_d, Q_BLOCK))

    def exact_step(k0, carry):
        m, acc = carry
        k0 = pl.multiple_of(k0, SEL_TILE)
        s = _dot_nt(qa, keys_aug(k0, SEL_TILE, npast_scr))
        m_new = jnp.maximum(m, jnp.max(s, axis=-1, keepdims=True))
        p = jnp.exp2((s - m_new) * c2)
        return m_new, jnp.exp2((m - m_new) * c2) * acc + _dot(p.astype(BF16), vals_aug(k0, SEL_TILE))

    def fast_step(k0, width, carry):
        m, acc, risk = carry
        k0 = pl.multiple_of(k0, width)
        s = _dot_nt(qa, keys_aug(k0, width, npast_scr))
        p = jnp.exp2((s - m) * c2)
        mx = jnp.max(s, axis=-1, keepdims=True)
        m_new = jnp.maximum(m, mx)
        acc = (acc + _dot(p.astype(BF16), vals_aug(k0, width))) * jnp.exp2((m - m_new) * c2)
        return m_new, acc, jnp.maximum(risk, mx - m)

    n_past = i * Q_BLOCK
    n_steps = n_past // step
    carry = (m_d, acc_d, jnp.zeros((rows, 1), F32))
    carry = lax.fori_loop(0, n_steps, lambda j, c: fast_step(j * step, step, c), carry)
    rem = n_past - n_steps * step
    k_r = n_steps * step
    carry = lax.cond(
        rem > step // 2, lambda: fast_step(k_r, step, carry),
        lambda: lax.cond(rem > step // 4, lambda: fast_step(k_r, step // 2, carry),
                         lambda: lax.cond(rem > 0, lambda: fast_step(k_r, step // 4, carry),
                                          lambda: carry)))
    _, acc_s, risk = carry

    def redo_exact():
        n_tiles = (n_past + SEL_TILE - 1) // SEL_TILE
        return lax.fori_loop(0, n_tiles, lambda j, c: exact_step(j * SEL_TILE, c), (m_d, acc_d))[1]

    acc_s = lax.cond(jnp.max(risk) * c2 > SEL_EXP2_GUARD, redo_exact, lambda: acc_s)
    o_s = acc_s[:, :NSA_HD] / acc_s[:, NSA_HD:NSA_HD + 1]

    wlen = WINDOW + Q_BLOCK
    ks0 = pl.multiple_of(jnp.maximum(i - WINDOW // Q_BLOCK, 0) * Q_BLOCK, Q_BLOCK)
    s_w = _dot_nt(qs, kw_ref[pl.ds(ks0, wlen), :])
    kpos = ks0 + lax.broadcasted_iota(jnp.int32, (1, wlen), 1)
    p_w = _masked_softmax(s_w, (kpos <= t) & (kpos > t - WINDOW), scale)
    o_w = _dot(p_w.astype(BF16), vw_ref[pl.ds(ks0, wlen), :])

    gates = _sigmoid(gate_ref[...])
    for h in range(NSA_HPG):
        r = slice(h * Q_BLOCK, (h + 1) * Q_BLOCK)
        out = (gates[:, 3 * h:3 * h + 1] * o_c[r] + gates[:, 3 * h + 1:3 * h + 2] * o_s[r]
               + gates[:, 3 * h + 2:3 * h + 3] * o_w[r])
        o_ref[:, h * NSA_HD:(h + 1) * NSA_HD] = out.astype(o_ref.dtype)


def _nsa_attention(projb, projf, kcv, single_buffer=True):
    s = projb.shape[0]
    n_slc = s // SLC_LEN
    step = min(SEL_STEP, s)
    assert n_slc >= SLC_TOPK and s >= WINDOW + Q_BLOCK and s % step == 0 and step % (4 * SEL_TILE) == 0
    ncp = kcv.shape[1]
    wsl = max(n_slc, 128)
    hd = NSA_HD
    c_start = jnp.arange(ncp)[None, :] * CMP_STRIDE
    s_start = jnp.arange(wsl)[:, None] * SLC_LEN
    ovt = ((c_start < s_start + SLC_LEN) & (c_start + CMP_LEN > s_start)
           & (jnp.arange(wsl)[:, None] < n_slc)).astype(BF16)
    eye = jnp.tile(jnp.eye(Q_BLOCK, dtype=BF16), (NSA_HPG, 1))
    resident = dict(pipeline_mode=pl.Buffered(1)) if single_buffer else {}
    kvb = _B_KV // hd
    gb = _F_SMALL // 128
    return pl.pallas_call(
        functools.partial(_nsa_kernel, n_slc=n_slc, step=step),
        grid=(NSA_GROUPS, s // Q_BLOCK),
        in_specs=[
            pl.BlockSpec((Q_BLOCK, NSA_HPG * hd), lambda g, i: (i, g)),
            pl.BlockSpec((1, ncp, hd), lambda g, i: (g, 0, 0)),
            pl.BlockSpec((1, ncp, hd), lambda g, i: (NSA_GROUPS + g, 0, 0)),
            pl.BlockSpec((s, hd), lambda g, i: (0, kvb + g), **resident),
            pl.BlockSpec((s, hd), lambda g, i: (0, kvb + 2 + g), **resident),
            pl.BlockSpec((s, hd), lambda g, i: (0, kvb + 4 + g), **resident),
            pl.BlockSpec((s, hd), lambda g, i: (0, kvb + 6 + g), **resident),
            pl.BlockSpec((Q_BLOCK, 128), lambda g, i: (i, gb + g)),
            pl.BlockSpec((wsl, ncp), lambda g, i: (0, 0)),
            pl.BlockSpec((NSA_HPG * Q_BLOCK, Q_BLOCK), lambda g, i: (0, 0)),
        ],
        out_specs=pl.BlockSpec((Q_BLOCK, NSA_HPG * hd), lambda g, i: (i, g)),
        out_shape=jax.ShapeDtypeStruct((s, NSA_HEADS * hd), BF16),
        scratch_shapes=[pltpu.VMEM((wsl, Q_BLOCK), F32), pltpu.VMEM((wsl, Q_BLOCK), F32)],
        compiler_params=_cparams(("parallel", "arbitrary")),
        name="nsa_attention",
    )(projb, kcv, kcv, projb, projb, projb, projb, projf, ovt, eye)


def _mlstm_kernel(q_ref, k_ref, v_ref, o0_ref, o1_ref, gc_ref, gr_ref, cw_ref, cb_ref, ng_ref,
                  y_ref, extq, extk, c_st, n_st, m_st, *, cpb):
    step = pl.program_id(0)
    rows = cpb * ML_CHUNK
    half = ML_HEADS * ML_DQK
    L = ML_CHUNK

    @pl.when(step == 0)
    def _():
        extq[0:8, :] = jnp.zeros((8, half), F32)
        extk[0:8, :] = jnp.zeros((8, half), F32)
        c_st[...] = jnp.zeros_like(c_st)
        n_st[...] = jnp.zeros_like(n_st)
        m_st[...] = jnp.zeros_like(m_st)

    extq[8:8 + rows, :] = q_ref[...]
    extk[8:8 + rows, :] = k_ref[...]
    cw = cw_ref[...]
    cb = cb_ref[...]

    def conv(ext, lo):
        acc = cb[:, lo:lo + half]
        for j in range(ML_CONV):
            off = 8 - (ML_CONV - 1) + j
            acc = acc + cw[j:j + 1, lo:lo + half] * ext[off:off + rows, :]
        return _silu(acc)

    qa = conv(extq, 0)
    ka = conv(extk, half) * (ML_DQK ** -0.5)
    extq[0:8, :] = extq[rows:rows + 8, :]
    extk[0:8, :] = extk[rows:rows + 8, :]

    gcol = gc_ref[...]
    grow = gr_ref[...]
    lf_col_all = _log_sigmoid(gcol[:, ML_HEADS:])
    lf_row_all = _log_sigmoid(grow[ML_HEADS:, :])
    ri = lax.broadcasted_iota(jnp.int32, (L, L), 0)
    ci = lax.broadcasted_iota(jnp.int32, (L, L), 1)
    tri = ri >= ci
    ng = ng_ref[...]

    for c in range(cpb):
        r0 = c * L
        for h in range(ML_HEADS):
            qh = qa[r0:r0 + L, h * ML_DQK:(h + 1) * ML_DQK]
            kh = ka[r0:r0 + L, h * ML_DQK:(h + 1) * ML_DQK]
            vh = v_ref[r0:r0 + L, h * ML_DV:(h + 1) * ML_DV]
            ig_col = gcol[r0:r0 + L, h:h + 1]
            ig_row = grow[h:h + 1, r0:r0 + L]
            lf_col = lf_col_all[r0:r0 + L, h:h + 1]
            lf_row = lf_row_all[h:h + 1, r0:r0 + L]
            bcum_col = jnp.sum(jnp.where(tri, lf_row, 0.0), axis=1, keepdims=True)
            bcum_row = jnp.sum(jnp.where(ri <= ci, lf_col, 0.0), axis=0, keepdims=True)
            b_last = jnp.sum(lf_row, axis=1, keepdims=True)
            m_old = m_st[h][:, 0:1]
            ct = c_st[h]
            nrow = n_st[h]

            dmat = jnp.where(tri, bcum_col - bcum_row + ig_row, NEG_INF)
            inter = bcum_col + m_old
            m_t = jnp.maximum(inter, jnp.max(dmat, axis=-1, keepdims=True))
            w_inter = jnp.exp(inter - m_t)
            qb = qh.astype(BF16)
            smat = _dot_nt(qb, kh.astype(BF16)) * jnp.exp(dmat - m_t)
            num = w_inter * _dot(qb, ct.astype(BF16)) + _dot(smat.astype(BF16), vh)
            den = (w_inter * jnp.sum(qh * nrow, axis=-1, keepdims=True)
                   + jnp.sum(smat, axis=-1, keepdims=True))
            hout = num / jnp.maximum(jnp.abs(den), jnp.exp(-m_t))

            a_col = b_last - bcum_col + ig_col
            a_row = b_last - bcum_row + ig_row
            m_new = jnp.maximum(b_last + m_old, jnp.max(a_row, axis=-1, keepdims=True))
            decay = jnp.exp(b_last + m_old - m_new)
            kw = kh * jnp.exp(a_col - m_new)
            c_st[h] = decay * ct + _dot_tn(kw.astype(BF16), vh)
            n_st[h] = decay * nrow + jnp.sum(kw, axis=0, keepdims=True)
            m_st[h] = jnp.broadcast_to(m_new, (1, 128))

            o_ref = o0_ref if h < ML_HEADS // 2 else o1_ref
            oc = (h % (ML_HEADS // 2)) * ML_DV
            og = _sigmoid(o_ref[r0:r0 + L, oc:oc + ML_DV])
            yn = _ln_rows(hout) * ng[:, h * ML_DV:(h + 1) * ML_DV]
            y_ref[r0:r0 + L, h * ML_DV:(h + 1) * ML_DV] = (yn * og).astype(y_ref.dtype)


def _mlstm(projb, projf, gates_col, gates_row, conv_w, conv_b, norm_g, cpb):
    s = projb.shape[0]
    rows = cpb * ML_CHUNK
    half = ML_HEADS * ML_DQK
    dv = ML_HEADS * ML_DV
    qblk = _F_MLQK // half
    oblk = _F_MLO // half
    return pl.pallas_call(
        functools.partial(_mlstm_kernel, cpb=cpb),
        grid=(s // rows,),
        in_specs=[
            pl.BlockSpec((rows, half), lambda j: (j, qblk)),
            pl.BlockSpec((rows, half), lambda j: (j, qblk + 1)),
            pl.BlockSpec((rows, dv), lambda j: (j, _B_MLV // dv)),
            pl.BlockSpec((rows, half), lambda j: (j, oblk)),
            pl.BlockSpec((rows, half), lambda j: (j, oblk + 1)),
            pl.BlockSpec((rows, 2 * ML_HEADS), lambda j: (j, 0)),
            pl.BlockSpec((2 * ML_HEADS, rows), lambda j: (0, j)),
            pl.BlockSpec((ML_CONV, 2 * half), lambda j: (0, 0)),
            pl.BlockSpec((1, 2 * half), lambda j: (0, 0)),
            pl.BlockSpec((1, dv), lambda j: (0, 0)),
        ],
        out_specs=pl.BlockSpec((rows, dv), lambda j: (j, 0)),
        out_shape=jax.ShapeDtypeStruct((s, dv), BF16),
        scratch_shapes=[
            pltpu.VMEM((rows + 8, half), F32),
            pltpu.VMEM((rows + 8, half), F32),
            pltpu.VMEM((ML_HEADS, ML_DQK, ML_DV), F32),
            pltpu.VMEM((ML_HEADS, 1, ML_DQK), F32),
            pltpu.VMEM((ML_HEADS, 1, 128), F32),
        ],
        compiler_params=_cparams(("arbitrary",)),
        name="mlstm",
    )(projf, projf, projb, projf, projf, gates_col, gates_row, conv_w, conv_b, norm_g)


def _merge_kernel(yn_ref, ym_ref, g0_ref, g1_ref, x_ref, wn_ref, wm_ref, wo_ref, gate_ref,
                  lg_ref, lb_ref, o_ref):
    a = _dot(yn_ref[...], wn_ref[...])
    b = _dot(ym_ref[...], wm_ref[...])
    merged = _sigmoid(g0_ref[...]) * a + _sigmoid(g1_ref[...]) * b
    y = _dot(merged.astype(BF16), wo_ref[...])
    z = ALPHA * x_ref[...] + gate_ref[...] * y
    o_ref[...] = _ln_rows(z) * lg_ref[...] + lb_ref[...]


def _merge_outproj(y_nsa, y_ml, projf, x, wn, wm, wo, gate, ln_g, ln_b, tm, single_buffer=True):
    s, d = x.shape
    resident = dict(pipeline_mode=pl.Buffered(1)) if single_buffer else {}
    mb = _F_MERGE // d
    row = lambda i: (0, 0)
    return pl.pallas_call(
        _merge_kernel,
        grid=(s // tm,),
        in_specs=[
            pl.BlockSpec((tm, y_nsa.shape[1]), lambda i: (i, 0)),
            pl.BlockSpec((tm, y_ml.shape[1]), lambda i: (i, 0)),
            pl.BlockSpec((tm, d), lambda i: (i, mb)),
            pl.BlockSpec((tm, d), lambda i: (i, mb + 1)),
            pl.BlockSpec((tm, d), lambda i: (i, 0)),
            pl.BlockSpec(wn.shape, row, **resident),
            pl.BlockSpec(wm.shape, row, **resident),
            pl.BlockSpec(wo.shape, row, **resident),
            pl.BlockSpec((1, d), row),
            pl.BlockSpec((1, d), row),
            pl.BlockSpec((1, d), row),
        ],
        out_specs=pl.BlockSpec((tm, d), lambda i: (i, 0)),
        out_shape=jax.ShapeDtypeStruct((s, d), F32),
        compiler_params=_cparams(("parallel",)),
        name="merge_outproj",
    )(y_nsa, y_ml, projf, projf, x, wn, wm, wo, gate, ln_g, ln_b)


def _ffn_kernel(x_ref, xh_ref, sc_ref, sh_ref, gate_ref, wa_ref, wg_ref, cw_ref, cb_ref, wd_ref,
                lg_ref, lb_ref, o_ref, h_scr, a_scr, acc, *, halo):
    i = pl.program_id(0)
    f = pl.program_id(1)
    tm = x_ref.shape[0]

    @pl.when(f == 0)
    def _():
        mod = lambda v: (_ln_rows(v) * (1.0 + sc_ref[...]) + sh_ref[...]).astype(BF16)
        h_scr[0:halo, :] = mod(xh_ref[...])
        h_scr[halo:halo + tm, :] = mod(x_ref[...])
        acc[...] = jnp.zeros_like(acc)

    hx = h_scr[...]
    a_ext = _dot(hx, wa_ref[...])
    rid = lax.broadcasted_iota(jnp.int32, (halo + tm, 1), 0)
    a_scr[...] = jnp.where((rid >= halo) | (i > 0), a_ext, 0.0)
    cw = cw_ref[...]
    conv = cb_ref[...]
    for j in range(FFN_CONV):
        off = halo - (FFN_CONV - 1) + j
        conv = conv + cw[j:j + 1, :] * a_scr[off:off + tm, :]
    g = _dot(h_scr[halo:halo + tm, :], wg_ref[...])
    act = (_silu(conv) * g).astype(BF16)
    acc[...] += _dot(act, wd_ref[...])

    @pl.when(f == pl.num_programs(1) - 1)
    def _():
        z = ALPHA * x_ref[...] + gate_ref[...] * acc[...]
        o_ref[...] = _ln_rows(z) * lg_ref[...] + lb_ref[...]


def _conv_ffn(x, sc, sh, gate, w_up, conv_w, conv_b, w_down, ln_g, ln_b, tm, tf):
    s, d = x.shape
    dff = w_down.shape[0]
    halo = 16
    nf = dff // tf
    row = lambda i, f: (0, 0)
    return pl.pallas_call(
        functools.partial(_ffn_kernel, halo=halo),
        grid=(s // tm, nf),
        in_specs=[
            pl.BlockSpec((tm, d), lambda i, f: (i, 0)),
            pl.BlockSpec((halo, d), lambda i, f: (jnp.maximum(i * (tm // halo) - 1, 0), 0)),
            pl.BlockSpec((1, d), row),
            pl.BlockSpec((1, d), row),
            pl.BlockSpec((1, d), row),
            pl.BlockSpec((d, tf), lambda i, f: (0, f)),
            pl.BlockSpec((d, tf), lambda i, f: (0, nf + f)),
            pl.BlockSpec((FFN_CONV, tf), lambda i, f: (0, f)),
            pl.BlockSpec((1, tf), lambda i, f: (0, f)),
            pl.BlockSpec((tf, d), lambda i, f: (f, 0)),
            pl.BlockSpec((1, d), row),
            pl.BlockSpec((1, d), row),
        ],
        out_specs=pl.BlockSpec((tm, d), lambda i, f: (i, 0)),
        out_shape=jax.ShapeDtypeStruct((s, d), F32),
        scratch_shapes=[
            pltpu.VMEM((halo + tm, d), BF16),
            pltpu.VMEM((halo + tm, tf), F32),
            pltpu.VMEM((tm, d), F32),
        ],
        compiler_params=_cparams(("parallel", "arbitrary")),
        name="conv_ffn",
    )(x, x, sc, sh, gate, w_up, w_up, conv_w, conv_b, w_down, ln_g, ln_b)


def _token_mixer(x, sc, sh, gate, w_in, cmp_pe, cmp_w1, cmp_w2, ml_conv_w, ml_conv_b, ml_gate_b,
                 ml_norm_g, w_br_nsa, w_br_ml, w_o, ln_g, ln_b):
    s, d = x.shape
    wb, wf = _split_w_in(w_in)
    tm = min(1024, s)
    projb = _inproj(x, sc, sh, wb, BF16, tm, 1024)
    projf = _inproj(x, sc, sh, wf, F32, tm, 1024)
    nch = s // CMP_STRIDE
    a4 = projf[:, _F_CMP:_F_CMP + _CMP_COLS].reshape(nch, CMP_STRIDE, 2 * NSA_GROUPS, NSA_HD)
    a4 = a4.transpose(2, 0, 1, 3).reshape(2 * NSA_GROUPS, nch, CMP_STRIDE * NSA_HD)
    kcv = _compress(a4, cmp_pe.reshape(2, 1, CMP_LEN * NSA_HD), cmp_w1, cmp_w2)
    y_nsa = _nsa_attention(projb, projf, kcv)
    ifo = _F_SMALL + 2 * 128
    gates_col = projf[:, ifo:ifo + _ML_IF] + ml_gate_b[None, :]
    y_ml = _mlstm(projb, projf, gates_col, gates_col.T, ml_conv_w, ml_conv_b[None, :],
                  ml_norm_g[None, :], cpb=min(4, s // ML_CHUNK))
    return _merge_outproj(y_nsa, y_ml, projf, x, w_br_nsa.astype(BF16), w_br_ml.astype(BF16),
                          w_o.astype(BF16), gate, ln_g, ln_b, tm=min(256, s))


def _forward(x, c, w_ada, b_ada, w_in, cmp_pe, cmp_w1, cmp_w2, ml_conv_w, ml_conv_b, ml_gate_b,
             ml_norm_g, w_br_nsa, w_br_ml, w_o, w_up, ffn_conv_w, ffn_conv_b, w_down, ln_g, ln_b):
    b, s, d = x.shape
    assert b == 1 and d == D_MODEL
    depth = w_ada.shape[0]
    mod = _modulation(c, w_ada, b_ada)
    xs = x[0]
    for l in range(depth):
        sh1, sc1, g1, sh2, sc2, g2 = [mod[l, :, k * d:(k + 1) * d] for k in range(6)]
        xs = _token_mixer(xs, sc1, sh1, g1, w_in[l], cmp_pe[l], cmp_w1[l], cmp_w2[l], ml_conv_w[l],
                          ml_conv_b[l], ml_gate_b[l], ml_norm_g[l], w_br_nsa[l], w_br_ml[l], w_o[l],
                          ln_g[l, 0][None, :], ln_b[l, 0][None, :])
        xs = _conv_ffn(xs, sc2, sh2, g2, w_up[l].astype(BF16), ffn_conv_w[l], ffn_conv_b[l][None, :],
                       w_down[l].astype(BF16), ln_g[l, 1][None, :], ln_b[l, 1][None, :],
                       tm=min(512, s), tf=512)
    return xs[None]


def kernel(x, c, w_ada, b_ada, w_in, cmp_pe, cmp_w1, cmp_w2, ml_conv_w, ml_conv_b, ml_gate_b, ml_norm_g, w_br_nsa, w_br_ml, w_o, w_up, ffn_conv_w, ffn_conv_b, w_down, ln_g, ln_b):
    return _forward(x, c, w_ada, b_ada, w_in, cmp_pe, cmp_w1, cmp_w2, ml_conv_w, ml_conv_b,
                    ml_gate_b, ml_norm_g, w_br_nsa, w_br_ml, w_o, w_up, ffn_conv_w, ffn_conv_b,
                    w_down, ln_g, ln_b)
```

```python
import functools
import math

import jax
import jax.numpy as jnp
from jax import lax
from jax.experimental import pallas as pl
from jax.experimental.pallas import tpu as pltpu

F32 = jnp.float32
BF16 = jnp.bfloat16

D_MODEL = 2048
DEPTH = 2
NSA_HEADS = 8
NSA_GROUPS = 2
NSA_HPG = NSA_HEADS // NSA_GROUPS
NSA_HD = 128
CMP_LEN = 32
CMP_STRIDE = 16
CMP_HID = 256
SLC_LEN = 64
SLC_TOPK = 16
WINDOW = 512
Q_BLOCK = 128
ML_HEADS = 4
ML_DQK = 128
ML_DV = 256
ML_CHUNK = 64
ML_CONV = 4
D_FF = 5632
FFN_CONV = 3
ALPHA = (2 * DEPTH) ** 0.25
LN_EPS = 1e-5
NEG_INF = -1e30

V7X_VMEM_BYTES = 64 * 1024 * 1024
VMEM_LIMIT = 56 * 1024 * 1024

_NSA_Q = NSA_HEADS * NSA_HD
_NSA_KV = 3 * 2 * NSA_GROUPS * NSA_HD
_NSA_G = 3 * NSA_HEADS
_ML_QK = 2 * ML_HEADS * ML_DQK
_ML_V = ML_HEADS * ML_DV
_ML_IF = 2 * ML_HEADS
_CMP_COLS = 2 * NSA_GROUPS * NSA_HD
_F_MERGE = 0
_F_CMP = 2 * D_MODEL
_F_MLQK = _F_CMP + _CMP_COLS
_F_MLO = _F_MLQK + _ML_QK
_F_SMALL = _F_MLO + _ML_V
_F_COLS = _F_SMALL + 4 * 128
_B_Q = 0
_B_KV = _NSA_Q
_B_MLV = _B_KV + 8 * NSA_HD
_B_COLS = _B_MLV + _ML_V

SEL_TILE = 512
SEL_STEP = 4096
SEL_BIG = 2.0 ** 60
SEL_EXP2_GUARD = 64.0


def _cparams(sem):
    return pltpu.CompilerParams(dimension_semantics=sem, vmem_limit_bytes=VMEM_LIMIT)


def _ln_rows(x):
    mu = jnp.mean(x, axis=-1, keepdims=True)
    xc = x - mu
    var = jnp.mean(xc * xc, axis=-1, keepdims=True)
    return xc * lax.rsqrt(var + LN_EPS)


def _sigmoid(x):
    return 1.0 / (1.0 + jnp.exp(-x))


def _silu(x):
    return x * _sigmoid(x)


def _gelu_tanh(x):
    c = math.sqrt(2.0 / math.pi)
    return x * (0.5 * (1.0 + jnp.tanh(c * (x + 0.044715 * (x * x * x)))))


def _log_sigmoid(x):
    return jnp.minimum(x, 0.0) - jnp.log1p(jnp.exp(-jnp.abs(x)))


def _dot(a, b):
    return jnp.dot(a, b, preferred_element_type=F32)


def _dot_nt(a, b):
    return lax.dot_general(a, b, (((1,), (1,)), ((), ())), preferred_element_type=F32)


def _dot_tn(a, b):
    return lax.dot_general(a, b, (((0,), (0,)), ((), ())), preferred_element_type=F32)


def _masked_softmax(raw, mask, scale):
    raw = jnp.where(mask, raw, NEG_INF)
    m = jnp.max(raw, axis=-1, keepdims=True)
    e = jnp.exp2((raw - m) * (scale * math.log2(math.e)))
    den = jnp.sum(e, axis=-1, keepdims=True)
    return e * jnp.where(m > 0.5 * NEG_INF, 1.0 / den, 0.0)


def _mod_kernel(c_ref, w_ref, b_ref, o_ref):
    ca = _silu(c_ref[...])
    o = _dot(ca.astype(BF16), w_ref[0].astype(BF16))
    o_ref[0] = o[0:1] + b_ref[0]


def _modulation(c, w_ada, b_ada):
    depth, d, n = w_ada.shape
    tn = 1024
    c8 = jnp.broadcast_to(c, (8, d))
    return pl.pallas_call(
        _mod_kernel,
        grid=(depth, n // tn),
        in_specs=[
            pl.BlockSpec((8, d), lambda l, j: (0, 0)),
            pl.BlockSpec((1, d, tn), lambda l, j: (l, 0, j)),
            pl.BlockSpec((1, 1, tn), lambda l, j: (l, 0, j)),
        ],
        out_specs=pl.BlockSpec((1, 1, tn), lambda l, j: (l, 0, j)),
        out_shape=jax.ShapeDtypeStruct((depth, 1, n), F32),
        compiler_params=_cparams(("parallel", "parallel")),
        name="adaln_mod",
    )(c8, w_ada, b_ada.reshape(depth, 1, n))


def _inproj_kernel(x_ref, sc_ref, sh_ref, w_ref, o_ref, h_scr):
    @pl.when(pl.program_id(1) == 0)
    def _():
        h = _ln_rows(x_ref[...]) * (1.0 + sc_ref[...]) + sh_ref[...]
        h_scr[...] = h.astype(BF16)

    o_ref[...] = _dot(h_scr[...], w_ref[...]).astype(o_ref.dtype)


def _inproj(x, sc, sh, w, out_dtype, tm, tn):
    s, d = x.shape
    n = w.shape[1]
    return pl.pallas_call(
        _inproj_kernel,
        grid=(s // tm, n // tn),
        in_specs=[
            pl.BlockSpec((tm, d), lambda i, j: (i, 0)),
            pl.BlockSpec((1, d), lambda i, j: (0, 0)),
            pl.BlockSpec((1, d), lambda i, j: (0, 0)),
            pl.BlockSpec((d, tn), lambda i, j: (0, j)),
        ],
        out_specs=pl.BlockSpec((tm, tn), lambda i, j: (i, j)),
        out_shape=jax.ShapeDtypeStruct((s, n), out_dtype),
        scratch_shapes=[pltpu.VMEM((tm, d), BF16)],
        compiler_params=_cparams(("parallel", "arbitrary")),
        name="in_proj",
    )(x, sc, sh, w)


def _split_w_in(w):
    d = w.shape[0]
    o = 0
    q = w[:, o:o + _NSA_Q]; o += _NSA_Q
    kv = w[:, o:o + _NSA_KV]; o += _NSA_KV
    g = w[:, o:o + _NSA_G]; o += _NSA_G
    mlqk = w[:, o:o + _ML_QK]; o += _ML_QK
    mlv = w[:, o:o + _ML_V]; o += _ML_V
    mlif = w[:, o:o + _ML_IF]; o += _ML_IF
    mlo = w[:, o:o + _ML_V]; o += _ML_V
    merge = w[:, o:o + 2 * D_MODEL]
    per_g = 3 * NSA_HPG
    z = lambda n: jnp.zeros((d, n), w.dtype)
    wb = jnp.concatenate([q, kv[:, _CMP_COLS:], mlv], axis=1)
    wf = jnp.concatenate(
        [merge, kv[:, :_CMP_COLS], mlqk, mlo,
         g[:, :per_g], z(128 - per_g), g[:, per_g:], z(128 - per_g), mlif, z(128 - _ML_IF), z(128)],
        axis=1)
    return wb.astype(BF16), wf.astype(BF16)


def _compress_kernel(a_ref, pe_ref, w1_ref, w2_ref, o_ref):
    a = a_ref[0]
    pe = pe_ref[0]
    half = a.shape[1]
    n = a.shape[0]
    p = _dot((a + pe[:, :half]).astype(BF16), w1_ref[0, :half, :].astype(BF16))
    q = _dot((a + pe[:, half:]).astype(BF16), w1_ref[0, half:, :].astype(BF16))
    pre = p + pltpu.roll(q, n - 1, 0)
    g = _gelu_tanh(pre)
    o_ref[0] = _dot(g.astype(BF16), w2_ref[0].astype(BF16)).astype(o_ref.dtype)


def _compress(a4, pe, w1, w2):
    four, n, wdt = a4.shape
    return pl.pallas_call(
        _compress_kernel,
        grid=(four,),
        in_specs=[
            pl.BlockSpec((1, n, wdt), lambda j: (j, 0, 0)),
            pl.BlockSpec((1, 1, 2 * wdt), lambda j: (j // NSA_GROUPS, 0, 0)),
            pl.BlockSpec((1, 2 * wdt, CMP_HID), lambda j: (j // NSA_GROUPS, 0, 0)),
            pl.BlockSpec((1, CMP_HID, NSA_HD), lambda j: (j // NSA_GROUPS, 0, 0)),
        ],
        out_specs=pl.BlockSpec((1, n, NSA_HD), lambda j: (j, 0, 0)),
        out_shape=jax.ShapeDtypeStruct((four, n, NSA_HD), BF16),
        compiler_params=_cparams(("parallel",)),
        name="nsa_compress",
    )(a4, pe, w1, w2)


def _nsa_kernel(q_ref, kc_ref, vc_ref, ks_ref, vs_ref, kw_ref, vw_ref, gate_ref, ovt_ref, eye_ref,
                o_ref, nsel_scr, npast_scr, *, step):
    i = pl.program_id(1)
    rows = NSA_HPG * Q_BLOCK
    scale = NSA_HD ** -0.5
    qt = q_ref[...]
    qs = jnp.concatenate([qt[:, h * NSA_HD:(h + 1) * NSA_HD] for h in range(NSA_HPG)], axis=0)
    t = i * Q_BLOCK + (lax.broadcasted_iota(jnp.int32, (rows, 1), 0) & (Q_BLOCK - 1))

    kc = kc_ref[0]
    ncp = kc.shape[0]
    cidx = lax.broadcasted_iota(jnp.int32, (rows, ncp), 1)
    last_c = (t - (CMP_LEN - 1)) // CMP_STRIDE
    p_c = _masked_softmax(_dot_nt(qs, kc), cidx <= last_c, scale)
    o_c = _dot(p_c.astype(BF16), vc_ref[0])

    psum = (p_c[0:Q_BLOCK] + p_c[Q_BLOCK:2 * Q_BLOCK]
            + p_c[2 * Q_BLOCK:3 * Q_BLOCK] + p_c[3 * Q_BLOCK:4 * Q_BLOCK])
    p_hi = psum.astype(BF16)
    p_lo = (psum - p_hi.astype(F32)).astype(BF16)
    ovt = ovt_ref[...]
    imp = _dot_nt(ovt, p_hi) + _dot_nt(ovt, p_lo)
    wsl = imp.shape[0]
    bid = lax.broadcasted_iota(jnp.int32, (wsl, Q_BLOCK), 0)
    t_row = i * Q_BLOCK + lax.broadcasted_iota(jnp.int32, (1, Q_BLOCK), 1)
    cur = t_row // SLC_LEN
    bidf = bid.astype(F32)
    valid = bid * SLC_LEN <= t_row
    score0 = jnp.where(bid == 0, 3e6, jnp.where(bid == cur, 2e6, jnp.where(bid == cur - 1, 1e6,
                       jnp.where(valid, imp, -1.0 - bidf))))
    taken = -1e9
    nsel0 = jnp.full((wsl, Q_BLOCK), -SEL_BIG, F32)
    score, nsel = score0, nsel0
    for _ in range(SLC_TOPK):
        hit = score == jnp.max(score, axis=0, keepdims=True)
        nsel = jnp.where(hit, 0.0, nsel)
        score = jnp.where(hit, taken, score)

    def topk_ties():
        def one(_, c):
            sc, ns = c
            mx = jnp.max(sc, axis=0, keepdims=True)
            first = jnp.min(jnp.where(sc == mx, bidf, float(wsl)), axis=0, keepdims=True)
            hit = bidf == first
            return jnp.where(hit, taken, sc), jnp.where(hit, 0.0, ns)
        return lax.fori_loop(0, SLC_TOPK, one, (score0, nsel0))[1]

    n_taken = jnp.sum(jnp.where(nsel == 0.0, 1.0, 0.0), axis=0, keepdims=True)
    nsel = lax.cond(jnp.max(n_taken) > SLC_TOPK, topk_ties, lambda: nsel)
    blk_d = (i * Q_BLOCK) // SLC_LEN
    nsel_scr[...] = nsel
    npast_scr[...] = jnp.where(bid < blk_d, nsel, -SEL_BIG)

    qa = jnp.concatenate([qs, eye_ref[...]], axis=1)
    c2 = scale * math.log2(math.e)

    def keys_aug(k0, width, sel_scr):
        b0 = k0 // SLC_LEN
        bias = jnp.concatenate(
            [jnp.broadcast_to(sel_scr[pl.ds(b0 + b, 1), :], (SLC_LEN, Q_BLOCK))
             for b in range(width // SLC_LEN)], axis=0)
        return jnp.concatenate([ks_ref[pl.ds(k0, width), :], bias.astype(BF16)], axis=1)

    def vals_aug(k0, width):
        ones_col = (lax.broadcasted_iota(jnp.int32, (width, NSA_HD), 1) == 0).astype(BF16)
        return jnp.concatenate([vs_ref[pl.ds(k0, width), :], ones_col], axis=1)

    k_d = pl.multiple_of(i * Q_BLOCK, Q_BLOCK)
    s_d = _dot_nt(qa, keys_aug(k_d, Q_BLOCK, nsel_scr))
    s_d = jnp.where(k_d + lax.broadcasted_iota(jnp.int32, (1, Q_BLOCK), 1) <= t, s_d, NEG_INF)
    m_d = jnp.max(s_d, axis=-1, keepdims=True)
    acc_d = _dot(jnp.exp2((s_d - m_d) * c2).astype(BF16), vals_aug(k_d, Q_BLOCK))

    def exact_step(k0, carry):
        m, acc = carry
        k0 = pl.multiple_of(k0, SEL_TILE)
        s = _dot_nt(qa, keys_aug(k0, SEL_TILE, npast_scr))
        m_new = jnp.maximum(m, jnp.max(s, axis=-1, keepdims=True))
        p = jnp.exp2((s - m_new) * c2)
        return m_new, jnp.exp2((m - m_new) * c2) * acc + _dot(p.astype(BF16), vals_aug(k0, SEL_TILE))

    def fast_step(k0, width, carry):
        m, acc, risk = carry
        k0 = pl.multiple_of(k0, width)
        s = _dot_nt(qa, keys_aug(k0, width, npast_scr))
        p = jnp.exp2((s - m) * c2)
        mx = jnp.max(s, axis=-1, keepdims=True)
        m_new = jnp.maximum(m, mx)
        acc = (acc + _dot(p.astype(BF16), vals_aug(k0, width))) * jnp.exp2((m - m_new) * c2)
        return m_new, acc, jnp.maximum(risk, mx - m)

    n_past = i * Q_BLOCK
    n_steps = n_past // step
    carry = (m_d, acc_d, jnp.zeros((rows, 1), F32))
    carry = lax.fori_loop(0, n_steps, lambda j, c: fast_step(j * step, step, c), carry)
    rem = n_past - n_steps * step
    k_r = n_steps * step
    carry = lax.cond(
        rem > step // 2, lambda: fast_step(k_r, step, carry),
        lambda: lax.cond(rem > step // 4, lambda: fast_step(k_r, step // 2, carry),
                         lambda: lax.cond(rem > 0, lambda: fast_step(k_r, step // 4, carry),
                                          lambda: carry)))
    _, acc_s, risk = carry

    def redo_exact():
        n_tiles = (n_past + SEL_TILE - 1) // SEL_TILE
        return lax.fori_loop(0, n_tiles, lambda j, c: exact_step(j * SEL_TILE, c), (m_d, acc_d))[1]

    acc_s = lax.cond(jnp.max(risk) * c2 > SEL_EXP2_GUARD, redo_exact, lambda: acc_s)
    o_s = acc_s[:, :NSA_HD] / acc_s[:, NSA_HD:NSA_HD + 1]

    wlen = WINDOW + Q_BLOCK
    ks0 = pl.multiple_of(jnp.maximum(i - WINDOW // Q_BLOCK, 0) * Q_BLOCK, Q_BLOCK)
    s_w = _dot_nt(qs, kw_ref[pl.ds(ks0, wlen), :])
    kpos = ks0 + lax.broadcasted_iota(jnp.int32, (1, wlen), 1)
    p_w = _masked_softmax(s_w, (kpos <= t) & (kpos > t - WINDOW), scale)
    o_w = _dot(p_w.astype(BF16), vw_ref[pl.ds(ks0, wlen), :])

    gates = _sigmoid(gate_ref[...])
    for h in range(NSA_HPG):
        r = slice(h * Q_BLOCK, (h + 1) * Q_BLOCK)
        out = (gates[:, 3 * h:3 * h + 1] * o_c[r] + gates[:, 3 * h + 1:3 * h + 2] * o_s[r]
               + gates[:, 3 * h + 2:3 * h + 3] * o_w[r])
        o_ref[:, h * NSA_HD:(h + 1) * NSA_HD] = out.astype(o_ref.dtype)


def _nsa_attention(projb, projf, kcv, single_buffer=True):
    s = projb.shape[0]
    n_slc = s // SLC_LEN
    step = min(SEL_STEP, s)
    assert n_slc >= SLC_TOPK and s >= WINDOW + Q_BLOCK and s % step == 0 and step % (4 * SEL_TILE) == 0
    ncp = kcv.shape[1]
    wsl = max(n_slc, 128)
    hd = NSA_HD
    c_start = jnp.arange(ncp)[None, :] * CMP_STRIDE
    s_start = jnp.arange(wsl)[:, None] * SLC_LEN
    ovt = ((c_start < s_start + SLC_LEN) & (c_start + CMP_LEN > s_start)
           & (jnp.arange(wsl)[:, None] < n_slc)).astype(BF16)
    eye = jnp.tile(jnp.eye(Q_BLOCK, dtype=BF16), (NSA_HPG, 1))
    resident = dict(pipeline_mode=pl.Buffered(1)) if single_buffer else {}
    kvb = _B_KV // hd
    gb = _F_SMALL // 128
    return pl.pallas_call(
        functools.partial(_nsa_kernel, step=step),
        grid=(NSA_GROUPS, s // Q_BLOCK),
        in_specs=[
            pl.BlockSpec((Q_BLOCK, NSA_HPG * hd), lambda g, i: (i, g)),
            pl.BlockSpec((1, ncp, hd), lambda g, i: (g, 0, 0)),
            pl.BlockSpec((1, ncp, hd), lambda g, i: (NSA_GROUPS + g, 0, 0)),
            pl.BlockSpec((s, hd), lambda g, i: (0, kvb + g), **resident),
            pl.BlockSpec((s, hd), lambda g, i: (0, kvb + 2 + g), **resident),
            pl.BlockSpec((s, hd), lambda g, i: (0, kvb + 4 + g), **resident),
            pl.BlockSpec((s, hd), lambda g, i: (0, kvb + 6 + g), **resident),
            pl.BlockSpec((Q_BLOCK, 128), lambda g, i: (i, gb + g)),
            pl.BlockSpec((wsl, ncp), lambda g, i: (0, 0)),
            pl.BlockSpec((NSA_HPG * Q_BLOCK, Q_BLOCK), lambda g, i: (0, 0)),
        ],
        out_specs=pl.BlockSpec((Q_BLOCK, NSA_HPG * hd), lambda g, i: (i, g)),
        out_shape=jax.ShapeDtypeStruct((s, NSA_HEADS * hd), BF16),
        scratch_shapes=[pltpu.VMEM((wsl, Q_BLOCK), F32), pltpu.VMEM((wsl, Q_BLOCK), F32)],
        compiler_params=_cparams(("parallel", "arbitrary")),
        name="nsa_attention",
    )(projb, kcv, kcv, projb, projb, projb, projb, projf, ovt, eye)


def _mlstm_kernel(q_ref, k_ref, v_ref, o0_ref, o1_ref, gc_ref, gr_ref, cw_ref, cb_ref, ng_ref,
                  y_ref, extq, extk, c_st, n_st, m_st, *, cpb):
    step = pl.program_id(0)
    rows = cpb * ML_CHUNK
    half = ML_HEADS * ML_DQK
    L = ML_CHUNK

    @pl.when(step == 0)
    def _():
        extq[0:8, :] = jnp.zeros((8, half), F32)
        extk[0:8, :] = jnp.zeros((8, half), F32)
        c_st[...] = jnp.zeros_like(c_st)
        n_st[...] = jnp.zeros_like(n_st)
        m_st[...] = jnp.zeros_like(m_st)

    extq[8:8 + rows, :] = q_ref[...]
    extk[8:8 + rows, :] = k_ref[...]
    cw = cw_ref[...]
    cb = cb_ref[...]

    def conv(ext, lo):
        acc = cb[:, lo:lo + half]
        for j in range(ML_CONV):
            off = 8 - (ML_CONV - 1) + j
            acc = acc + cw[j:j + 1, lo:lo + half] * ext[off:off + rows, :]
        return _silu(acc)

    qa = conv(extq, 0)
    ka = conv(extk, half) * (ML_DQK ** -0.5)
    extq[0:8, :] = extq[rows:rows + 8, :]
    extk[0:8, :] = extk[rows:rows + 8, :]

    gcol = gc_ref[...]
    grow = gr_ref[...]
    lf_col_all = _log_sigmoid(gcol[:, ML_HEADS:])
    lf_row_all = _log_sigmoid(grow[ML_HEADS:, :])
    ri = lax.broadcasted_iota(jnp.int32, (L, L), 0)
    ci = lax.broadcasted_iota(jnp.int32, (L, L), 1)
    tri = ri >= ci
    ng = ng_ref[...]

    state = [(c_st[h], n_st[h], m_st[h][:, 0:1]) for h in range(ML_HEADS)]
    ones_blk = jnp.ones((L, 128), BF16)
    for c in range(cpb):
        r0 = c * L
        kt_c = ka[r0:r0 + L, :].T
        for h in range(ML_HEADS):
            qh = qa[r0:r0 + L, h * ML_DQK:(h + 1) * ML_DQK]
            kh = ka[r0:r0 + L, h * ML_DQK:(h + 1) * ML_DQK]
            vh = v_ref[r0:r0 + L, h * ML_DV:(h + 1) * ML_DV]
            ig_col = gcol[r0:r0 + L, h:h + 1]
            ig_row = grow[h:h + 1, r0:r0 + L]
            lf_col = lf_col_all[r0:r0 + L, h:h + 1]
            lf_row = lf_row_all[h:h + 1, r0:r0 + L]
            bcum_col = jnp.sum(jnp.where(tri, lf_row, 0.0), axis=1, keepdims=True)
            bcum_row = jnp.sum(jnp.where(ri <= ci, lf_col, 0.0), axis=0, keepdims=True)
            b_last = jnp.sum(lf_row, axis=1, keepdims=True)
            dmat = jnp.where(tri, bcum_col - bcum_row + ig_row, NEG_INF)
            a_loc = jnp.max(dmat, axis=-1, keepdims=True)
            qb = qh.astype(BF16)
            s_loc = _dot_nt(qb, kh.astype(BF16)) * jnp.exp(dmat - a_loc)
            sv = _dot(s_loc.astype(BF16), vh)
            s_sum = _dot(s_loc.astype(BF16), ones_blk)[:, 0:1]
            a_col = b_last - bcum_col + ig_col
            a_row = b_last - bcum_row + ig_row
            a_max = jnp.max(a_row, axis=-1, keepdims=True)
            k_sum = jnp.sum(kh * jnp.exp(a_col - a_max), axis=0, keepdims=True)
            kwt = kt_c[h * ML_DQK:(h + 1) * ML_DQK, :] * jnp.exp(a_row - a_max)
            kv = _dot(kwt.astype(BF16), vh)

            ct, nrow, m_old = state[h]
            inter = bcum_col + m_old
            m_t = jnp.maximum(inter, a_loc)
            w_inter = jnp.exp(inter - m_t)
            w_loc = jnp.exp(a_loc - m_t)
            num = w_inter * _dot(qb, ct.astype(BF16)) + w_loc * sv
            qn = _dot_nt(qb, jnp.broadcast_to(nrow, (8, ML_DQK)).astype(BF16))[:, 0:1]
            den = w_inter * qn + w_loc * s_sum
            hout = num / jnp.maximum(jnp.abs(den), jnp.exp(-m_t))

            m_new = jnp.maximum(b_last + m_old, a_max)
            decay = jnp.exp(b_last + m_old - m_new)
            g_new = jnp.exp(a_max - m_new)
            state[h] = (decay * ct + g_new * kv, decay * nrow + g_new * k_sum, m_new)

            o_ref = o0_ref if h < ML_HEADS // 2 else o1_ref
            oc = (h % (ML_HEADS // 2)) * ML_DV
            og = _sigmoid(o_ref[r0:r0 + L, oc:oc + ML_DV])
            yn = _ln_rows(hout) * ng[:, h * ML_DV:(h + 1) * ML_DV]
            y_ref[r0:r0 + L, h * ML_DV:(h + 1) * ML_DV] = (yn * og).astype(y_ref.dtype)

    for h in range(ML_HEADS):
        c_st[h], n_st[h] = state[h][0], state[h][1]
        m_st[h] = jnp.broadcast_to(state[h][2], (1, 128))


def _mlstm(projb, projf, gates_col, gates_row, conv_w, conv_b, norm_g, cpb):
    s = projb.shape[0]
    rows = cpb * ML_CHUNK
    half = ML_HEADS * ML_DQK
    dv = ML_HEADS * ML_DV
    qblk = _F_MLQK // half
    oblk = _F_MLO // half
    return pl.pallas_call(
        functools.partial(_mlstm_kernel, cpb=cpb),
        grid=(s // rows,),
        in_specs=[
            pl.BlockSpec((rows, half), lambda j: (j, qblk)),
            pl.BlockSpec((rows, half), lambda j: (j, qblk + 1)),
            pl.BlockSpec((rows, dv), lambda j: (j, _B_MLV // dv)),
            pl.BlockSpec((rows, half), lambda j: (j, oblk)),
            pl.BlockSpec((rows, half), lambda j: (j, oblk + 1)),
            pl.BlockSpec((rows, 2 * ML_HEADS), lambda j: (j, 0)),
            pl.BlockSpec((2 * ML_HEADS, rows), lambda j: (0, j)),
            pl.BlockSpec((ML_CONV, 2 * half), lambda j: (0, 0)),
            pl.BlockSpec((1, 2 * half), lambda j: (0, 0)),
            pl.BlockSpec((1, dv), lambda j: (0, 0)),
        ],
        out_specs=pl.BlockSpec((rows, dv), lambda j: (j, 0)),
        out_shape=jax.ShapeDtypeStruct((s, dv), BF16),
        scratch_shapes=[
            pltpu.VMEM((rows + 8, half), F32),
            pltpu.VMEM((rows + 8, half), F32),
            pltpu.VMEM((ML_HEADS, ML_DQK, ML_DV), F32),
            pltpu.VMEM((ML_HEADS, 1, ML_DQK), F32),
            pltpu.VMEM((ML_HEADS, 1, 128), F32),
        ],
        compiler_params=_cparams(("arbitrary",)),
        name="mlstm",
    )(projf, projf, projb, projf, projf, gates_col, gates_row, conv_w, conv_b, norm_g)


def _merge_kernel(yn_ref, ym_ref, g0_ref, g1_ref, x_ref, wn_ref, wm_ref, wo_ref, gate_ref,
                  lg_ref, lb_ref, o_ref):
    a = _dot(yn_ref[...], wn_ref[...])
    b = _dot(ym_ref[...], wm_ref[...])
    merged = _sigmoid(g0_ref[...]) * a + _sigmoid(g1_ref[...]) * b
    y = _dot(merged.astype(BF16), wo_ref[...])
    z = ALPHA * x_ref[...] + gate_ref[...] * y
    o_ref[...] = _ln_rows(z) * lg_ref[...] + lb_ref[...]


def _merge_outproj(y_nsa, y_ml, projf, x, wn, wm, wo, gate, ln_g, ln_b, tm, single_buffer=True):
    s, d = x.shape
    resident = dict(pipeline_mode=pl.Buffered(1)) if single_buffer else {}
    mb = _F_MERGE // d
    row = lambda i: (0, 0)
    return pl.pallas_call(
        _merge_kernel,
        grid=(s // tm,),
        in_specs=[
            pl.BlockSpec((tm, y_nsa.shape[1]), lambda i: (i, 0)),
            pl.BlockSpec((tm, y_ml.shape[1]), lambda i: (i, 0)),
            pl.BlockSpec((tm, d), lambda i: (i, mb)),
            pl.BlockSpec((tm, d), lambda i: (i, mb + 1)),
            pl.BlockSpec((tm, d), lambda i: (i, 0)),
            pl.BlockSpec(wn.shape, row, **resident),
            pl.BlockSpec(wm.shape, row, **resident),
            pl.BlockSpec(wo.shape, row, **resident),
            pl.BlockSpec((1, d), row),
            pl.BlockSpec((1, d), row),
            pl.BlockSpec((1, d), row),
        ],
        out_specs=pl.BlockSpec((tm, d), lambda i: (i, 0)),
        out_shape=jax.ShapeDtypeStruct((s, d), F32),
        compiler_params=_cparams(("parallel",)),
        name="merge_outproj",
    )(y_nsa, y_ml, projf, projf, x, wn, wm, wo, gate, ln_g, ln_b)


def _ffn_kernel(x_ref, xh_ref, sc_ref, sh_ref, gate_ref, wa_ref, wg_ref, cw_ref, cb_ref, wd_ref,
                lg_ref, lb_ref, o_ref, h_scr, a_scr, acc, *, halo):
    i = pl.program_id(0)
    f = pl.program_id(1)
    tm = x_ref.shape[0]

    @pl.when(f == 0)
    def _():
        mod = lambda v: (_ln_rows(v) * (1.0 + sc_ref[...]) + sh_ref[...]).astype(BF16)
        h_scr[0:halo, :] = mod(xh_ref[...])
        h_scr[halo:halo + tm, :] = mod(x_ref[...])
        acc[...] = jnp.zeros_like(acc)

    hx = h_scr[...]
    a_ext = _dot(hx, wa_ref[...])
    rid = lax.broadcasted_iota(jnp.int32, (halo + tm, 1), 0)
    a_scr[...] = jnp.where((rid >= halo) | (i > 0), a_ext, 0.0)
    cw = cw_ref[...]
    conv = cb_ref[...]
    for j in range(FFN_CONV):
        off = halo - (FFN_CONV - 1) + j
        conv = conv + cw[j:j + 1, :] * a_scr[off:off + tm, :]
    g = _dot(h_scr[halo:halo + tm, :], wg_ref[...])
    act = (_silu(conv) * g).astype(BF16)
    acc[...] += _dot(act, wd_ref[...])

    @pl.when(f == pl.num_programs(1) - 1)
    def _():
        z = ALPHA * x_ref[...] + gate_ref[...] * acc[...]
        o_ref[...] = _ln_rows(z) * lg_ref[...] + lb_ref[...]


def _conv_ffn(x, sc, sh, gate, w_up, conv_w, conv_b, w_down, ln_g, ln_b, tm, tf):
    s, d = x.shape
    dff = w_down.shape[0]
    halo = 16
    nf = dff // tf
    row = lambda i, f: (0, 0)
    return pl.pallas_call(
        functools.partial(_ffn_kernel, halo=halo),
        grid=(s // tm, nf),
        in_specs=[
            pl.BlockSpec((tm, d), lambda i, f: (i, 0)),
            pl.BlockSpec((halo, d), lambda i, f: (jnp.maximum(i * (tm // halo) - 1, 0), 0)),
            pl.BlockSpec((1, d), row),
            pl.BlockSpec((1, d), row),
            pl.BlockSpec((1, d), row),
            pl.BlockSpec((d, tf), lambda i, f: (0, f)),
            pl.BlockSpec((d, tf), lambda i, f: (0, nf + f)),
            pl.BlockSpec((FFN_CONV, tf), lambda i, f: (0, f)),
            pl.BlockSpec((1, tf), lambda i, f: (0, f)),
            pl.BlockSpec((tf, d), lambda i, f: (f, 0)),
            pl.BlockSpec((1, d), row),
            pl.BlockSpec((1, d), row),
        ],
        out_specs=pl.BlockSpec((tm, d), lambda i, f: (i, 0)),
        out_shape=jax.ShapeDtypeStruct((s, d), F32),
        scratch_shapes=[
            pltpu.VMEM((halo + tm, d), BF16),
            pltpu.VMEM((halo + tm, tf), F32),
            pltpu.VMEM((tm, d), F32),
        ],
        compiler_params=_cparams(("parallel", "arbitrary")),
        name="conv_ffn",
    )(x, x, sc, sh, gate, w_up, w_up, conv_w, conv_b, w_down, ln_g, ln_b)


def _token_mixer(x, sc, sh, gate, w_in, cmp_pe, cmp_w1, cmp_w2, ml_conv_w, ml_conv_b, ml_gate_b,
                 ml_norm_g, w_br_nsa, w_br_ml, w_o, ln_g, ln_b):
    s, d = x.shape
    wb, wf = _split_w_in(w_in)
    tm = min(1024, s)
    projb = _inproj(x, sc, sh, wb, BF16, tm, 1024)
    projf = _inproj(x, sc, sh, wf, F32, tm, 1024)
    nch = s // CMP_STRIDE
    a4 = projf[:, _F_CMP:_F_CMP + _CMP_COLS].reshape(nch, CMP_STRIDE, 2 * NSA_GROUPS, NSA_HD)
    a4 = a4.transpose(2, 0, 1, 3).reshape(2 * NSA_GROUPS, nch, CMP_STRIDE * NSA_HD)
    kcv = _compress(a4, cmp_pe.reshape(2, 1, CMP_LEN * NSA_HD), cmp_w1, cmp_w2)
    y_nsa = _nsa_attention(projb, projf, kcv)
    ifo = _F_SMALL + 2 * 128
    gates_col = projf[:, ifo:ifo + _ML_IF] + ml_gate_b[None, :]
    y_ml = _mlstm(projb, projf, gates_col, gates_col.T, ml_conv_w, ml_conv_b[None, :],
                  ml_norm_g[None, :], cpb=min(4, s // ML_CHUNK))
    return _merge_outproj(y_nsa, y_ml, projf, x, w_br_nsa.astype(BF16), w_br_ml.astype(BF16),
                          w_o.astype(BF16), gate, ln_g, ln_b, tm=min(256, s))


def _forward(x, c, w_ada, b_ada, w_in, cmp_pe, cmp_w1, cmp_w2, ml_conv_w, ml_conv_b, ml_gate_b,
             ml_norm_g, w_br_nsa, w_br_ml, w_o, w_up, ffn_conv_w, ffn_conv_b, w_down, ln_g, ln_b):
    b, s, d = x.shape
    assert b == 1 and d == D_MODEL
    depth = w_ada.shape[0]
    mod = _modulation(c, w_ada, b_ada)
    xs = x[0]
    for l in range(depth):
        sh1, sc1, g1, sh2, sc2, g2 = [mod[l, :, k * d:(k + 1) * d] for k in range(6)]
        xs = _token_mixer(xs, sc1, sh1, g1, w_in[l], cmp_pe[l], cmp_w1[l], cmp_w2[l], ml_conv_w[l],
                          ml_conv_b[l], ml_gate_b[l], ml_norm_g[l], w_br_nsa[l], w_br_ml[l], w_o[l],
                          ln_g[l, 0][None, :], ln_b[l, 0][None, :])
        xs = _conv_ffn(xs, sc2, sh2, g2, w_up[l].astype(BF16), ffn_conv_w[l], ffn_conv_b[l][None, :],
                       w_down[l].astype(BF16), ln_g[l, 1][None, :], ln_b[l, 1][None, :],
                       tm=min(512, s), tf=512)
    return xs[None]


def kernel(x, c, w_ada, b_ada, w_in, cmp_pe, cmp_w1, cmp_w2, ml_conv_w, ml_conv_b, ml_gate_b, ml_norm_g, w_br_nsa, w_br_ml, w_o, w_up, ffn_conv_w, ffn_conv_b, w_down, ln_g, ln_b):
    return _forward(x, c, w_ada, b_ada, w_in, cmp_pe, cmp_w1, cmp_w2, ml_conv_w, ml_conv_b,
                    ml_gate_b, ml_norm_g, w_br_nsa, w_br_ml, w_o, w_up, ffn_conv_w, ffn_conv_b,
                    w_down, ln_g, ln_b)
```

```python
import functools
import math

import jax
import jax.numpy as jnp
from jax import lax
from jax.experimental import pallas as pl
from jax.experimental.pallas import tpu as pltpu

F32 = jnp.float32
BF16 = jnp.bfloat16

D_MODEL = 2048
DEPTH = 2
NSA_HEADS = 8
NSA_GROUPS = 2
NSA_HPG = NSA_HEADS // NSA_GROUPS
NSA_HD = 128
CMP_LEN = 32
CMP_STRIDE = 16
CMP_HID = 256
SLC_LEN = 64
SLC_TOPK = 16
WINDOW = 512
Q_BLOCK = 128
ML_HEADS = 4
ML_DQK = 128
ML_DV = 256
ML_CHUNK = 64
ML_CONV = 4
D_FF = 5632
FFN_CONV = 3
ALPHA = (2 * DEPTH) ** 0.25
LN_EPS = 1e-5
NEG_INF = -1e30

V7X_VMEM_BYTES = 64 * 1024 * 1024
VMEM_LIMIT = 56 * 1024 * 1024

_NSA_Q = NSA_HEADS * NSA_HD
_NSA_KV = 3 * 2 * NSA_GROUPS * NSA_HD
_NSA_G = 3 * NSA_HEADS
_ML_QK = 2 * ML_HEADS * ML_DQK
_ML_V = ML_HEADS * ML_DV
_ML_IF = 2 * ML_HEADS
_CMP_COLS = 2 * NSA_GROUPS * NSA_HD
_F_MERGE = 0
_F_CMP = 2 * D_MODEL
_F_MLQK = _F_CMP + _CMP_COLS
_F_MLO = _F_MLQK + _ML_QK
_F_SMALL = _F_MLO + _ML_V
_F_COLS = _F_SMALL + 4 * 128
_B_Q = 0
_B_KV = _NSA_Q
_B_MLV = _B_KV + 8 * NSA_HD
_B_COLS = _B_MLV + _ML_V

SEL_TILE = 512
SEL_STEP = 4096
SEL_BIG = 2.0 ** 60
SEL_EXP2_GUARD = 64.0


def _cparams(sem):
    return pltpu.CompilerParams(dimension_semantics=sem, vmem_limit_bytes=VMEM_LIMIT)


def _ln_rows(x):
    mu = jnp.mean(x, axis=-1, keepdims=True)
    xc = x - mu
    var = jnp.mean(xc * xc, axis=-1, keepdims=True)
    return xc * lax.rsqrt(var + LN_EPS)


def _sigmoid(x):
    return 1.0 / (1.0 + jnp.exp(-x))


def _silu(x):
    return x * _sigmoid(x)


def _gelu_tanh(x):
    c = math.sqrt(2.0 / math.pi)
    return x * (0.5 * (1.0 + jnp.tanh(c * (x + 0.044715 * (x * x * x)))))


def _log_sigmoid(x):
    return jnp.minimum(x, 0.0) - jnp.log1p(jnp.exp(-jnp.abs(x)))


def _dot(a, b):
    return jnp.dot(a, b, preferred_element_type=F32)


def _dot_nt(a, b):
    return lax.dot_general(a, b, (((1,), (1,)), ((), ())), preferred_element_type=F32)


def _dot_tn(a, b):
    return lax.dot_general(a, b, (((0,), (0,)), ((), ())), preferred_element_type=F32)


def _masked_softmax(raw, mask, scale):
    raw = jnp.where(mask, raw, NEG_INF)
    m = jnp.max(raw, axis=-1, keepdims=True)
    e = jnp.exp2((raw - m) * (scale * math.log2(math.e)))
    den = jnp.sum(e, axis=-1, keepdims=True)
    return e * jnp.where(m > 0.5 * NEG_INF, 1.0 / den, 0.0)


def _mod_kernel(c_ref, w_ref, b_ref, o_ref):
    ca = _silu(c_ref[...])
    o = _dot(ca.astype(BF16), w_ref[0].astype(BF16))
    o_ref[0] = o[0:1] + b_ref[0]


def _modulation(c, w_ada, b_ada):
    depth, d, n = w_ada.shape
    tn = 1024
    c8 = jnp.broadcast_to(c, (8, d))
    return pl.pallas_call(
        _mod_kernel,
        grid=(depth, n // tn),
        in_specs=[
            pl.BlockSpec((8, d), lambda l, j: (0, 0)),
            pl.BlockSpec((1, d, tn), lambda l, j: (l, 0, j)),
            pl.BlockSpec((1, 1, tn), lambda l, j: (l, 0, j)),
        ],
        out_specs=pl.BlockSpec((1, 1, tn), lambda l, j: (l, 0, j)),
        out_shape=jax.ShapeDtypeStruct((depth, 1, n), F32),
        compiler_params=_cparams(("parallel", "parallel")),
        name="adaln_mod",
    )(c8, w_ada, b_ada.reshape(depth, 1, n))


def _inproj_kernel(x_ref, sc_ref, sh_ref, w_ref, o_ref, h_scr):
    @pl.when(pl.program_id(1) == 0)
    def _():
        h = _ln_rows(x_ref[...]) * (1.0 + sc_ref[...]) + sh_ref[...]
        h_scr[...] = h.astype(BF16)

    o_ref[...] = _dot(h_scr[...], w_ref[...]).astype(o_ref.dtype)


def _inproj(x, sc, sh, w, out_dtype, tm, tn):
    s, d = x.shape
    n = w.shape[1]
    return pl.pallas_call(
        _inproj_kernel,
        grid=(s // tm, n // tn),
        in_specs=[
            pl.BlockSpec((tm, d), lambda i, j: (i, 0)),
            pl.BlockSpec((1, d), lambda i, j: (0, 0)),
            pl.BlockSpec((1, d), lambda i, j: (0, 0)),
            pl.BlockSpec((d, tn), lambda i, j: (0, j)),
        ],
        out_specs=pl.BlockSpec((tm, tn), lambda i, j: (i, j)),
        out_shape=jax.ShapeDtypeStruct((s, n), out_dtype),
        scratch_shapes=[pltpu.VMEM((tm, d), BF16)],
        compiler_params=_cparams(("parallel", "arbitrary")),
        name="in_proj",
    )(x, sc, sh, w)


def _split_w_in(w):
    d = w.shape[0]
    o = 0
    q = w[:, o:o + _NSA_Q]; o += _NSA_Q
    kv = w[:, o:o + _NSA_KV]; o += _NSA_KV
    g = w[:, o:o + _NSA_G]; o += _NSA_G
    mlqk = w[:, o:o + _ML_QK]; o += _ML_QK
    mlv = w[:, o:o + _ML_V]; o += _ML_V
    mlif = w[:, o:o + _ML_IF]; o += _ML_IF
    mlo = w[:, o:o + _ML_V]; o += _ML_V
    merge = w[:, o:o + 2 * D_MODEL]
    per_g = 3 * NSA_HPG
    z = lambda n: jnp.zeros((d, n), w.dtype)
    wb = jnp.concatenate([q, kv[:, _CMP_COLS:], mlv], axis=1)
    wf = jnp.concatenate(
        [merge, kv[:, :_CMP_COLS], mlqk, mlo,
         g[:, :per_g], z(128 - per_g), g[:, per_g:], z(128 - per_g), mlif, z(128 - _ML_IF), z(128)],
        axis=1)
    return wb.astype(BF16), wf.astype(BF16)


def _compress_kernel(a_ref, pe_ref, w1_ref, w2_ref, o_ref):
    n = o_ref.shape[1]
    hd = NSA_HD

    def half_sum(lo):
        acc = jnp.zeros((n, CMP_HID), F32)
        for l in range(0, CMP_STRIDE, 2):
            x = jnp.concatenate(
                [a_ref[pl.ds(l + u, n, stride=CMP_STRIDE), :] + pe_ref[0, lo + l + u:lo + l + u + 1, :]
                 for u in range(2)], axis=1)
            w = w1_ref[0, (lo + l) * hd:(lo + l + 2) * hd, :]
            acc = acc + _dot(x.astype(BF16), w.astype(BF16))
        return acc

    pre = half_sum(0) + pltpu.roll(half_sum(CMP_STRIDE), n - 1, 0)
    g = _gelu_tanh(pre)
    o_ref[0] = _dot(g.astype(BF16), w2_ref[0].astype(BF16)).astype(o_ref.dtype)


def _compress(projf, pe, w1, w2):
    s = projf.shape[0]
    n = s // CMP_STRIDE
    four = 2 * NSA_GROUPS
    return pl.pallas_call(
        _compress_kernel,
        grid=(four,),
        in_specs=[
            pl.BlockSpec((s, NSA_HD), lambda j: (0, _F_CMP // NSA_HD + j)),
            pl.BlockSpec((1, CMP_LEN, NSA_HD), lambda j: (j // NSA_GROUPS, 0, 0)),
            pl.BlockSpec((1, CMP_LEN * NSA_HD, CMP_HID), lambda j: (j // NSA_GROUPS, 0, 0)),
            pl.BlockSpec((1, CMP_HID, NSA_HD), lambda j: (j // NSA_GROUPS, 0, 0)),
        ],
        out_specs=pl.BlockSpec((1, n, NSA_HD), lambda j: (j, 0, 0)),
        out_shape=jax.ShapeDtypeStruct((four, n, NSA_HD), BF16),
        compiler_params=_cparams(("parallel",)),
        name="nsa_compress",
    )(projf, pe, w1, w2)


def _nsa_kernel(q_ref, kc_ref, vc_ref, ks_ref, vs_ref, kw_ref, vw_ref, gate_ref, ovt_ref, eye_ref,
                o_ref, nsel_scr, npast_scr, *, step):
    i = pl.program_id(1)
    rows = NSA_HPG * Q_BLOCK
    scale = NSA_HD ** -0.5
    qt = q_ref[...]
    qs = jnp.concatenate([qt[:, h * NSA_HD:(h + 1) * NSA_HD] for h in range(NSA_HPG)], axis=0)
    t = i * Q_BLOCK + (lax.broadcasted_iota(jnp.int32, (rows, 1), 0) & (Q_BLOCK - 1))

    ncp = kc_ref.shape[1]
    last_c = (t - (CMP_LEN - 1)) // CMP_STRIDE

    def cmp_branch(width):
        cidx = lax.broadcasted_iota(jnp.int32, (rows, width), 1)
        p_c = _masked_softmax(_dot_nt(qs, kc_ref[0, :width, :]), cidx <= last_c, scale)
        o = _dot(p_c.astype(BF16), vc_ref[0, :width, :])
        psum = (p_c[0:Q_BLOCK] + p_c[Q_BLOCK:2 * Q_BLOCK]
                + p_c[2 * Q_BLOCK:3 * Q_BLOCK] + p_c[3 * Q_BLOCK:4 * Q_BLOCK])
        p_hi = psum.astype(BF16)
        p_lo = (psum - p_hi.astype(F32)).astype(BF16)
        ovt = ovt_ref[:, :width]
        return o, _dot_nt(ovt, p_hi) + _dot_nt(ovt, p_lo)

    n_vis = (i * Q_BLOCK + Q_BLOCK - CMP_LEN) // CMP_STRIDE + 1
    branch = lambda: cmp_branch(ncp)
    for width in (ncp // 2, ncp // 4):
        if width % 128 == 0:
            branch = (lambda w, other: lambda: lax.cond(n_vis <= w, lambda: cmp_branch(w), other))(width, branch)
    o_c, imp = branch()
    wsl = imp.shape[0]
    bid = lax.broadcasted_iota(jnp.int32, (wsl, Q_BLOCK), 0)
    t_row = i * Q_BLOCK + lax.broadcasted_iota(jnp.int32, (1, Q_BLOCK), 1)
    cur = t_row // SLC_LEN
    bidf = bid.astype(F32)
    valid = bid * SLC_LEN <= t_row
    score0 = jnp.where(bid == 0, 3e6, jnp.where(bid == cur, 2e6, jnp.where(bid == cur - 1, 1e6,
                       jnp.where(valid, imp, -1.0 - bidf))))
    taken = -1e9
    nsel0 = jnp.full((wsl, Q_BLOCK), -SEL_BIG, F32)
    forced = score0 >= 1e6
    n_free = SLC_TOPK - (1 + jnp.where(cur >= 1, 1, 0) + jnp.where(cur >= 2, 1, 0))

    def take_max(c, lanes=None):
        sc, ns = c
        hit = sc == jnp.max(sc, axis=0, keepdims=True)
        if lanes is not None:
            hit = hit & lanes
        return jnp.where(hit, taken, sc), jnp.where(hit, 0.0, ns)

    c = (jnp.where(forced, taken, score0), jnp.where(forced, 0.0, nsel0))
    n_min = SLC_TOPK - 3
    for _ in range(n_min):
        c = take_max(c)

    def early_rounds():
        e = c
        for r in range(n_min + 1, SLC_TOPK):
            e = take_max(e, n_free >= r)
        return e

    _, nsel = lax.cond(i == 0, early_rounds, lambda: c)

    def topk_ties():
        def one(_, c):
            sc, ns = c
            mx = jnp.max(sc, axis=0, keepdims=True)
            first = jnp.min(jnp.where(sc == mx, bidf, float(wsl)), axis=0, keepdims=True)
            hit = bidf == first
            return jnp.where(hit, taken, sc), jnp.where(hit, 0.0, ns)
        return lax.fori_loop(0, SLC_TOPK, one, (score0, nsel0))[1]

    n_taken = jnp.sum(jnp.where(nsel == 0.0, 1.0, 0.0), axis=0, keepdims=True)
    nsel = lax.cond(jnp.max(n_taken) > SLC_TOPK, topk_ties, lambda: nsel)
    blk_d = (i * Q_BLOCK) // SLC_LEN
    nsel_scr[...] = nsel
    npast_scr[...] = jnp.where(bid < blk_d, nsel, -SEL_BIG)

    qa = jnp.concatenate([qs, eye_ref[...]], axis=1)
    c2 = scale * math.log2(math.e)

    def keys_aug(k0, width, sel_scr):
        b0 = k0 // SLC_LEN
        bias = jnp.concatenate(
            [jnp.broadcast_to(sel_scr[pl.ds(b0 + b, 1), :], (SLC_LEN, Q_BLOCK))
             for b in range(width // SLC_LEN)], axis=0)
        return jnp.concatenate([ks_ref[pl.ds(k0, width), :], bias.astype(BF16)], axis=1)

    def vals_aug(k0, width):
        ones_col = (lax.broadcasted_iota(jnp.int32, (width, NSA_HD), 1) == 0).astype(BF16)
        return jnp.concatenate([vs_ref[pl.ds(k0, width), :], ones_col], axis=1)

    k_d = pl.multiple_of(i * Q_BLOCK, Q_BLOCK)
    s_d = _dot_nt(qa, keys_aug(k_d, Q_BLOCK, nsel_scr))
    s_d = jnp.where(k_d + lax.broadcasted_iota(jnp.int32, (1, Q_BLOCK), 1) <= t, s_d, NEG_INF)
    m_d = jnp.max(s_d, axis=-1, keepdims=True)
    acc_d = _dot(jnp.exp2((s_d - m_d) * c2).astype(BF16), vals_aug(k_d, Q_BLOCK))

    def exact_step(k0, carry):
        m, acc = carry
        k0 = pl.multiple_of(k0, SEL_TILE)
        s = _dot_nt(qa, keys_aug(k0, SEL_TILE, npast_scr))
        m_new = jnp.maximum(m, jnp.max(s, axis=-1, keepdims=True))
        p = jnp.exp2((s - m_new) * c2)
        return m_new, jnp.exp2((m - m_new) * c2) * acc + _dot(p.astype(BF16), vals_aug(k0, SEL_TILE))

    def fast_step(k0, width, carry):
        m, acc, risk = carry
        k0 = pl.multiple_of(k0, width)
        s = _dot_nt(qa, keys_aug(k0, width, npast_scr))
        p = jnp.exp2((s - m) * c2)
        mx = jnp.max(s, axis=-1, keepdims=True)
        m_new = jnp.maximum(m, mx)
        acc = (acc + _dot(p.astype(BF16), vals_aug(k0, width))) * jnp.exp2((m - m_new) * c2)
        return m_new, acc, jnp.maximum(risk, mx - m)

    n_past = i * Q_BLOCK
    n_steps = n_past // step
    carry = (m_d, acc_d, jnp.zeros((rows, 1), F32))
    carry = lax.fori_loop(0, n_steps, lambda j, c: fast_step(j * step, step, c), carry)
    rem = n_past - n_steps * step
    k_r = n_steps * step
    carry = lax.cond(
        rem > step // 2, lambda: fast_step(k_r, step, carry),
        lambda: lax.cond(rem > step // 4, lambda: fast_step(k_r, step // 2, carry),
                         lambda: lax.cond(rem > 0, lambda: fast_step(k_r, step // 4, carry),
                                          lambda: carry)))
    _, acc_s, risk = carry

    def redo_exact():
        n_tiles = (n_past + SEL_TILE - 1) // SEL_TILE
        return lax.fori_loop(0, n_tiles, lambda j, c: exact_step(j * SEL_TILE, c), (m_d, acc_d))[1]

    acc_s = lax.cond(jnp.max(risk) * c2 > SEL_EXP2_GUARD, redo_exact, lambda: acc_s)
    o_s = acc_s[:, :NSA_HD] / acc_s[:, NSA_HD:NSA_HD + 1]

    wlen = WINDOW + Q_BLOCK
    ks0 = pl.multiple_of(jnp.maximum(i - WINDOW // Q_BLOCK, 0) * Q_BLOCK, Q_BLOCK)
    s_w = _dot_nt(qs, kw_ref[pl.ds(ks0, wlen), :])
    kpos = ks0 + lax.broadcasted_iota(jnp.int32, (1, wlen), 1)
    p_w = _masked_softmax(s_w, (kpos <= t) & (kpos > t - WINDOW), scale)
    o_w = _dot(p_w.astype(BF16), vw_ref[pl.ds(ks0, wlen), :])

    gates = _sigmoid(gate_ref[...])
    for h in range(NSA_HPG):
        r = slice(h * Q_BLOCK, (h + 1) * Q_BLOCK)
        out = (gates[:, 3 * h:3 * h + 1] * o_c[r] + gates[:, 3 * h + 1:3 * h + 2] * o_s[r]
               + gates[:, 3 * h + 2:3 * h + 3] * o_w[r])
        o_ref[:, h * NSA_HD:(h + 1) * NSA_HD] = out.astype(o_ref.dtype)


def _nsa_attention(projb, projf, kcv, single_buffer=True):
    s = projb.shape[0]
    n_slc = s // SLC_LEN
    step = min(SEL_STEP, s)
    assert n_slc >= SLC_TOPK and s >= WINDOW + Q_BLOCK and s % step == 0 and step % (4 * SEL_TILE) == 0
    ncp = kcv.shape[1]
    wsl = max(n_slc, 128)
    hd = NSA_HD
    c_start = jnp.arange(ncp)[None, :] * CMP_STRIDE
    s_start = jnp.arange(wsl)[:, None] * SLC_LEN
    ovt = ((c_start < s_start + SLC_LEN) & (c_start + CMP_LEN > s_start)
           & (jnp.arange(wsl)[:, None] < n_slc)).astype(BF16)
    eye = jnp.tile(jnp.eye(Q_BLOCK, dtype=BF16), (NSA_HPG, 1))
    resident = dict(pipeline_mode=pl.Buffered(1)) if single_buffer else {}
    kvb = _B_KV // hd
    gb = _F_SMALL // 128
    return pl.pallas_call(
        functools.partial(_nsa_kernel, step=step),
        grid=(NSA_GROUPS, s // Q_BLOCK),
        in_specs=[
            pl.BlockSpec((Q_BLOCK, NSA_HPG * hd), lambda g, i: (i, g)),
            pl.BlockSpec((1, ncp, hd), lambda g, i: (g, 0, 0)),
            pl.BlockSpec((1, ncp, hd), lambda g, i: (NSA_GROUPS + g, 0, 0)),
            pl.BlockSpec((s, hd), lambda g, i: (0, kvb + g), **resident),
            pl.BlockSpec((s, hd), lambda g, i: (0, kvb + 2 + g), **resident),
            pl.BlockSpec((s, hd), lambda g, i: (0, kvb + 4 + g), **resident),
            pl.BlockSpec((s, hd), lambda g, i: (0, kvb + 6 + g), **resident),
            pl.BlockSpec((Q_BLOCK, 128), lambda g, i: (i, gb + g)),
            pl.BlockSpec((wsl, ncp), lambda g, i: (0, 0)),
            pl.BlockSpec((NSA_HPG * Q_BLOCK, Q_BLOCK), lambda g, i: (0, 0)),
        ],
        out_specs=pl.BlockSpec((Q_BLOCK, NSA_HPG * hd), lambda g, i: (i, g)),
        out_shape=jax.ShapeDtypeStruct((s, NSA_HEADS * hd), BF16),
        scratch_shapes=[pltpu.VMEM((wsl, Q_BLOCK), F32), pltpu.VMEM((wsl, Q_BLOCK), F32)],
        compiler_params=_cparams(("parallel", "arbitrary")),
        name="nsa_attention",
    )(projb, kcv, kcv, projb, projb, projb, projb, projf, ovt, eye)


def _mlstm_kernel(q_ref, k_ref, v_ref, o0_ref, o1_ref, gc_ref, gr_ref, cw_ref, cb_ref, ng_ref,
                  y_ref, extq, extk, c_st, n_st, m_st, *, cpb):
    step = pl.program_id(0)
    rows = cpb * ML_CHUNK
    half = ML_HEADS * ML_DQK
    L = ML_CHUNK

    @pl.when(step == 0)
    def _():
        extq[0:8, :] = jnp.zeros((8, half), F32)
        extk[0:8, :] = jnp.zeros((8, half), F32)
        c_st[...] = jnp.zeros_like(c_st)
        n_st[...] = jnp.zeros_like(n_st)
        m_st[...] = jnp.zeros_like(m_st)

    extq[8:8 + rows, :] = q_ref[...]
    extk[8:8 + rows, :] = k_ref[...]
    cw = cw_ref[...]
    cb = cb_ref[...]

    def conv(ext, lo):
        acc = cb[:, lo:lo + half]
        for j in range(ML_CONV):
            off = 8 - (ML_CONV - 1) + j
            acc = acc + cw[j:j + 1, lo:lo + half] * ext[off:off + rows, :]
        return _silu(acc)

    qa = conv(extq, 0)
    ka = conv(extk, half) * (ML_DQK ** -0.5)
    extq[0:8, :] = extq[rows:rows + 8, :]
    extk[0:8, :] = extk[rows:rows + 8, :]

    gcol = gc_ref[...]
    grow = gr_ref[...]
    lf_col_all = _log_sigmoid(gcol[:, ML_HEADS:])
    lf_row_all = _log_sigmoid(grow[ML_HEADS:, :])
    ri = lax.broadcasted_iota(jnp.int32, (L, L), 0)
    ci = lax.broadcasted_iota(jnp.int32, (L, L), 1)
    tri = ri >= ci
    ng = ng_ref[...]

    state = [(c_st[h], n_st[h], m_st[h][:, 0:1]) for h in range(ML_HEADS)]
    ones_blk = jnp.ones((L, 128), BF16)
    for c in range(cpb):
        r0 = c * L
        kt_c = ka[r0:r0 + L, :].T
        for h in range(ML_HEADS):
            qh = qa[r0:r0 + L, h * ML_DQK:(h + 1) * ML_DQK]
            kh = ka[r0:r0 + L, h * ML_DQK:(h + 1) * ML_DQK]
            vh = v_ref[r0:r0 + L, h * ML_DV:(h + 1) * ML_DV]
            ig_col = gcol[r0:r0 + L, h:h + 1]
            ig_row = grow[h:h + 1, r0:r0 + L]
            lf_col = lf_col_all[r0:r0 + L, h:h + 1]
            lf_row = lf_row_all[h:h + 1, r0:r0 + L]
            bcum_col = jnp.sum(jnp.where(tri, lf_row, 0.0), axis=1, keepdims=True)
            bcum_row = jnp.sum(jnp.where(ri <= ci, lf_col, 0.0), axis=0, keepdims=True)
            b_last = jnp.sum(lf_row, axis=1, keepdims=True)
            dmat = jnp.where(tri, bcum_col - bcum_row + ig_row, NEG_INF)
            a_loc = jnp.max(dmat, axis=-1, keepdims=True)
            qb = qh.astype(BF16)
            s_loc = _dot_nt(qb, kh.astype(BF16)) * jnp.exp(dmat - a_loc)
            sv = _dot(s_loc.astype(BF16), vh)
            s_sum = _dot(s_loc.astype(BF16), ones_blk)[:, 0:1]
            a_col = b_last - bcum_col + ig_col
            a_row = b_last - bcum_row + ig_row
            a_max = jnp.max(a_row, axis=-1, keepdims=True)
            k_sum = jnp.sum(kh * jnp.exp(a_col - a_max), axis=0, keepdims=True)
            kwt = kt_c[h * ML_DQK:(h + 1) * ML_DQK, :] * jnp.exp(a_row - a_max)
            kv = _dot(kwt.astype(BF16), vh)

            ct, nrow, m_old = state[h]
            inter = bcum_col + m_old
            m_t = jnp.maximum(inter, a_loc)
            w_inter = jnp.exp(inter - m_t)
            w_loc = jnp.exp(a_loc - m_t)
            num = w_inter * _dot(qb, ct.astype(BF16)) + w_loc * sv
            qn = _dot_nt(qb, jnp.broadcast_to(nrow, (8, ML_DQK)).astype(BF16))[:, 0:1]
            den = w_inter * qn + w_loc * s_sum
            hout = num / jnp.maximum(jnp.abs(den), jnp.exp(-m_t))

            m_new = jnp.maximum(b_last + m_old, a_max)
            decay = jnp.exp(b_last + m_old - m_new)
            g_new = jnp.exp(a_max - m_new)
            state[h] = (decay * ct + g_new * kv, decay * nrow + g_new * k_sum, m_new)

            o_ref = o0_ref if h < ML_HEADS // 2 else o1_ref
            oc = (h % (ML_HEADS // 2)) * ML_DV
            og = _sigmoid(o_ref[r0:r0 + L, oc:oc + ML_DV])
            yn = _ln_rows(hout) * ng[:, h * ML_DV:(h + 1) * ML_DV]
            y_ref[r0:r0 + L, h * ML_DV:(h + 1) * ML_DV] = (yn * og).astype(y_ref.dtype)

    for h in range(ML_HEADS):
        c_st[h], n_st[h] = state[h][0], state[h][1]
        m_st[h] = jnp.broadcast_to(state[h][2], (1, 128))


def _mlstm(projb, projf, gates_col, gates_row, conv_w, conv_b, norm_g, cpb):
    s = projb.shape[0]
    rows = cpb * ML_CHUNK
    half = ML_HEADS * ML_DQK
    dv = ML_HEADS * ML_DV
    qblk = _F_MLQK // half
    oblk = _F_MLO // half
    return pl.pallas_call(
        functools.partial(_mlstm_kernel, cpb=cpb),
        grid=(s // rows,),
        in_specs=[
            pl.BlockSpec((rows, half), lambda j: (j, qblk)),
            pl.BlockSpec((rows, half), lambda j: (j, qblk + 1)),
            pl.BlockSpec((rows, dv), lambda j: (j, _B_MLV // dv)),
            pl.BlockSpec((rows, half), lambda j: (j, oblk)),
            pl.BlockSpec((rows, half), lambda j: (j, oblk + 1)),
            pl.BlockSpec((rows, 2 * ML_HEADS), lambda j: (j, 0)),
            pl.BlockSpec((2 * ML_HEADS, rows), lambda j: (0, j)),
            pl.BlockSpec((ML_CONV, 2 * half), lambda j: (0, 0)),
            pl.BlockSpec((1, 2 * half), lambda j: (0, 0)),
            pl.BlockSpec((1, dv), lambda j: (0, 0)),
        ],
        out_specs=pl.BlockSpec((rows, dv), lambda j: (j, 0)),
        out_shape=jax.ShapeDtypeStruct((s, dv), BF16),
        scratch_shapes=[
            pltpu.VMEM((rows + 8, half), F32),
            pltpu.VMEM((rows + 8, half), F32),
            pltpu.VMEM((ML_HEADS, ML_DQK, ML_DV), F32),
            pltpu.VMEM((ML_HEADS, 1, ML_DQK), F32),
            pltpu.VMEM((ML_HEADS, 1, 128), F32),
        ],
        compiler_params=_cparams(("arbitrary",)),
        name="mlstm",
    )(projf, projf, projb, projf, projf, gates_col, gates_row, conv_w, conv_b, norm_g)


def _merge_kernel(yn_ref, ym_ref, g0_ref, g1_ref, x_ref, wn_ref, wm_ref, wo_ref, gate_ref,
                  lg_ref, lb_ref, o_ref):
    a = _dot(yn_ref[...], wn_ref[...])
    b = _dot(ym_ref[...], wm_ref[...])
    merged = _sigmoid(g0_ref[...]) * a + _sigmoid(g1_ref[...]) * b
    y = _dot(merged.astype(BF16), wo_ref[...])
    z = ALPHA * x_ref[...] + gate_ref[...] * y
    o_ref[...] = _ln_rows(z) * lg_ref[...] + lb_ref[...]


def _merge_outproj(y_nsa, y_ml, projf, x, wn, wm, wo, gate, ln_g, ln_b, tm, single_buffer=True):
    s, d = x.shape
    resident = dict(pipeline_mode=pl.Buffered(1)) if single_buffer else {}
    mb = _F_MERGE // d
    row = lambda i: (0, 0)
    return pl.pallas_call(
        _merge_kernel,
        grid=(s // tm,),
        in_specs=[
            pl.BlockSpec((tm, y_nsa.shape[1]), lambda i: (i, 0)),
            pl.BlockSpec((tm, y_ml.shape[1]), lambda i: (i, 0)),
            pl.BlockSpec((tm, d), lambda i: (i, mb)),
            pl.BlockSpec((tm, d), lambda i: (i, mb + 1)),
            pl.BlockSpec((tm, d), lambda i: (i, 0)),
            pl.BlockSpec(wn.shape, row, **resident),
            pl.BlockSpec(wm.shape, row, **resident),
            pl.BlockSpec(wo.shape, row, **resident),
            pl.BlockSpec((1, d), row),
            pl.BlockSpec((1, d), row),
            pl.BlockSpec((1, d), row),
        ],
        out_specs=pl.BlockSpec((tm, d), lambda i: (i, 0)),
        out_shape=jax.ShapeDtypeStruct((s, d), F32),
        compiler_params=_cparams(("parallel",)),
        name="merge_outproj",
    )(y_nsa, y_ml, projf, projf, x, wn, wm, wo, gate, ln_g, ln_b)


def _ffn_kernel(x_ref, xh_ref, sc_ref, sh_ref, gate_ref, wa_ref, wg_ref, cw_ref, cb_ref, wd_ref,
                lg_ref, lb_ref, o_ref, h_scr, a_scr, acc, *, halo):
    i = pl.program_id(0)
    f = pl.program_id(1)
    tm = x_ref.shape[0]

    @pl.when(f == 0)
    def _():
        mod = lambda v: (_ln_rows(v) * (1.0 + sc_ref[...]) + sh_ref[...]).astype(BF16)
        h_scr[0:halo, :] = mod(xh_ref[...])
        h_scr[halo:halo + tm, :] = mod(x_ref[...])
        acc[...] = jnp.zeros_like(acc)

    hx = h_scr[...]
    a_ext = _dot(hx, wa_ref[...])
    rid = lax.broadcasted_iota(jnp.int32, (halo + tm, 1), 0)
    a_scr[...] = jnp.where((rid >= halo) | (i > 0), a_ext, 0.0)
    cw = cw_ref[...]
    conv = cb_ref[...]
    for j in range(FFN_CONV):
        off = halo - (FFN_CONV - 1) + j
        conv = conv + cw[j:j + 1, :] * a_scr[off:off + tm, :]
    g = _dot(h_scr[halo:halo + tm, :], wg_ref[...])
    act = (_silu(conv) * g).astype(BF16)
    acc[...] += _dot(act, wd_ref[...])

    @pl.when(f == pl.num_programs(1) - 1)
    def _():
        z = ALPHA * x_ref[...] + gate_ref[...] * acc[...]
        o_ref[...] = _ln_rows(z) * lg_ref[...] + lb_ref[...]


def _conv_ffn(x, sc, sh, gate, w_up, conv_w, conv_b, w_down, ln_g, ln_b, tm, tf):
    s, d = x.shape
    dff = w_down.shape[0]
    halo = 16
    nf = dff // tf
    row = lambda i, f: (0, 0)
    return pl.pallas_call(
        functools.partial(_ffn_kernel, halo=halo),
        grid=(s // tm, nf),
        in_specs=[
            pl.BlockSpec((tm, d), lambda i, f: (i, 0)),
            pl.BlockSpec((halo, d), lambda i, f: (jnp.maximum(i * (tm // halo) - 1, 0), 0)),
            pl.BlockSpec((1, d), row),
            pl.BlockSpec((1, d), row),
            pl.BlockSpec((1, d), row),
            pl.BlockSpec((d, tf), lambda i, f: (0, f)),
            pl.BlockSpec((d, tf), lambda i, f: (0, nf + f)),
            pl.BlockSpec((FFN_CONV, tf), lambda i, f: (0, f)),
            pl.BlockSpec((1, tf), lambda i, f: (0, f)),
            pl.BlockSpec((tf, d), lambda i, f: (f, 0)),
            pl.BlockSpec((1, d), row),
            pl.BlockSpec((1, d), row),
        ],
        out_specs=pl.BlockSpec((tm, d), lambda i, f: (i, 0)),
        out_shape=jax.ShapeDtypeStruct((s, d), F32),
        scratch_shapes=[
            pltpu.VMEM((halo + tm, d), BF16),
            pltpu.VMEM((halo + tm, tf), F32),
            pltpu.VMEM((tm, d), F32),
        ],
        compiler_params=_cparams(("parallel", "arbitrary")),
        name="conv_ffn",
    )(x, x, sc, sh, gate, w_up, w_up, conv_w, conv_b, w_down, ln_g, ln_b)


def _token_mixer(x, sc, sh, gate, w_in, cmp_pe, cmp_w1, cmp_w2, ml_conv_w, ml_conv_b, ml_gate_b,
                 ml_norm_g, w_br_nsa, w_br_ml, w_o, ln_g, ln_b):
    s, d = x.shape
    wb, wf = _split_w_in(w_in)
    tm = min(1024, s)
    projb = _inproj(x, sc, sh, wb, BF16, tm, 1024)
    projf = _inproj(x, sc, sh, wf, F32, tm, 1024)
    kcv = _compress(projf, cmp_pe, cmp_w1, cmp_w2)
    y_nsa = _nsa_attention(projb, projf, kcv)
    ifo = _F_SMALL + 2 * 128
    gates_col = projf[:, ifo:ifo + _ML_IF] + ml_gate_b[None, :]
    y_ml = _mlstm(projb, projf, gates_col, gates_col.T, ml_conv_w, ml_conv_b[None, :],
                  ml_norm_g[None, :], cpb=min(4, s // ML_CHUNK))
    return _merge_outproj(y_nsa, y_ml, projf, x, w_br_nsa.astype(BF16), w_br_ml.astype(BF16),
                          w_o.astype(BF16), gate, ln_g, ln_b, tm=min(256, s))


def _forward(x, c, w_ada, b_ada, w_in, cmp_pe, cmp_w1, cmp_w2, ml_conv_w, ml_conv_b, ml_gate_b,
             ml_norm_g, w_br_nsa, w_br_ml, w_o, w_up, ffn_conv_w, ffn_conv_b, w_down, ln_g, ln_b):
    b, s, d = x.shape
    assert b == 1 and d == D_MODEL
    depth = w_ada.shape[0]
    mod = _modulation(c, w_ada, b_ada)
    xs = x[0]
    for l in range(depth):
        sh1, sc1, g1, sh2, sc2, g2 = [mod[l, :, k * d:(k + 1) * d] for k in range(6)]
        xs = _token_mixer(xs, sc1, sh1, g1, w_in[l], cmp_pe[l], cmp_w1[l], cmp_w2[l], ml_conv_w[l],
                          ml_conv_b[l], ml_gate_b[l], ml_norm_g[l], w_br_nsa[l], w_br_ml[l], w_o[l],
                          ln_g[l, 0][None, :], ln_b[l, 0][None, :])
        xs = _conv_ffn(xs, sc2, sh2, g2, w_up[l].astype(BF16), ffn_conv_w[l], ffn_conv_b[l][None, :],
                       w_down[l].astype(BF16), ln_g[l, 1][None, :], ln_b[l, 1][None, :],
                       tm=min(512, s), tf=512)
    return xs[None]


def kernel(x, c, w_ada, b_ada, w_in, cmp_pe, cmp_w1, cmp_w2, ml_conv_w, ml_conv_b, ml_gate_b, ml_norm_g, w_br_nsa, w_br_ml, w_o, w_up, ffn_conv_w, ffn_conv_b, w_down, ln_g, ln_b):
    return _forward(x, c, w_ada, b_ada, w_in, cmp_pe, cmp_w1, cmp_w2, ml_conv_w, ml_conv_b,
                    ml_gate_b, ml_norm_g, w_br_nsa, w_br_ml, w_o, w_up, ffn_conv_w, ffn_conv_b,
                    w_down, ln_g, ln_b)
```

```python
import functools
import math

import jax
import jax.numpy as jnp
from jax import lax
from jax.experimental import pallas as pl
from jax.experimental.pallas import tpu as pltpu

F32 = jnp.float32
BF16 = jnp.bfloat16

D_MODEL = 2048
DEPTH = 2
NSA_HEADS = 8
NSA_GROUPS = 2
NSA_HPG = NSA_HEADS // NSA_GROUPS
NSA_HD = 128
CMP_LEN = 32
CMP_STRIDE = 16
CMP_HID = 256
SLC_LEN = 64
SLC_TOPK = 16
WINDOW = 512
Q_BLOCK = 128
ML_HEADS = 4
ML_DQK = 128
ML_DV = 256
ML_CHUNK = 64
ML_CONV = 4
D_FF = 5632
FFN_CONV = 3
ALPHA = (2 * DEPTH) ** 0.25
LN_EPS = 1e-5
NEG_INF = -1e30

V7X_VMEM_BYTES = 64 * 1024 * 1024
VMEM_LIMIT = 56 * 1024 * 1024

_NSA_Q = NSA_HEADS * NSA_HD
_NSA_KV = 3 * 2 * NSA_GROUPS * NSA_HD
_NSA_G = 3 * NSA_HEADS
_ML_QK = 2 * ML_HEADS * ML_DQK
_ML_V = ML_HEADS * ML_DV
_ML_IF = 2 * ML_HEADS
_CMP_COLS = 2 * NSA_GROUPS * NSA_HD
_F_MERGE = 0
_F_CMP = 2 * D_MODEL
_F_MLQK = _F_CMP + _CMP_COLS
_F_MLO = _F_MLQK + _ML_QK
_F_SMALL = _F_MLO + _ML_V
_F_COLS = _F_SMALL + 4 * 128
_B_Q = 0
_B_KV = _NSA_Q
_B_MLV = _B_KV + 8 * NSA_HD
_B_COLS = _B_MLV + _ML_V

SEL_TILE = 512
SEL_STEP = 4096
SEL_BIG = 2.0 ** 60
NSA_Q_BLOCKS_PER_STEP = 2
SEL_EXP2_GUARD = 64.0


def _cparams(sem):
    return pltpu.CompilerParams(dimension_semantics=sem, vmem_limit_bytes=VMEM_LIMIT)


def _ln_rows(x):
    mu = jnp.mean(x, axis=-1, keepdims=True)
    xc = x - mu
    var = jnp.mean(xc * xc, axis=-1, keepdims=True)
    return xc * lax.rsqrt(var + LN_EPS)


def _sigmoid(x):
    return 1.0 / (1.0 + jnp.exp(-x))


def _silu(x):
    return x * _sigmoid(x)


def _gelu_tanh(x):
    c = math.sqrt(2.0 / math.pi)
    return x * (0.5 * (1.0 + jnp.tanh(c * (x + 0.044715 * (x * x * x)))))


def _log_sigmoid(x):
    return jnp.minimum(x, 0.0) - jnp.log1p(jnp.exp(-jnp.abs(x)))


def _dot(a, b):
    return jnp.dot(a, b, preferred_element_type=F32)


def _dot_nt(a, b):
    return lax.dot_general(a, b, (((1,), (1,)), ((), ())), preferred_element_type=F32)


def _dot_tn(a, b):
    return lax.dot_general(a, b, (((0,), (0,)), ((), ())), preferred_element_type=F32)


def _masked_softmax(raw, mask, scale):
    raw = jnp.where(mask, raw, NEG_INF)
    m = jnp.max(raw, axis=-1, keepdims=True)
    e = jnp.exp2((raw - m) * (scale * math.log2(math.e)))
    den = jnp.sum(e, axis=-1, keepdims=True)
    return e * jnp.where(m > 0.5 * NEG_INF, 1.0 / den, 0.0)


def _mod_kernel(c_ref, w_ref, b_ref, o_ref):
    ca = _silu(c_ref[...])
    o = _dot(ca.astype(BF16), w_ref[0].astype(BF16))
    o_ref[0] = o[0:1] + b_ref[0]


def _modulation(c, w_ada, b_ada):
    depth, d, n = w_ada.shape
    tn = 1024
    c8 = jnp.broadcast_to(c, (8, d))
    return pl.pallas_call(
        _mod_kernel,
        grid=(depth, n // tn),
        in_specs=[
            pl.BlockSpec((8, d), lambda l, j: (0, 0)),
            pl.BlockSpec((1, d, tn), lambda l, j: (l, 0, j)),
            pl.BlockSpec((1, 1, tn), lambda l, j: (l, 0, j)),
        ],
        out_specs=pl.BlockSpec((1, 1, tn), lambda l, j: (l, 0, j)),
        out_shape=jax.ShapeDtypeStruct((depth, 1, n), F32),
        compiler_params=_cparams(("parallel", "parallel")),
        name="adaln_mod",
    )(c8, w_ada, b_ada.reshape(depth, 1, n))


def _inproj_kernel(x_ref, sc_ref, sh_ref, w_ref, o_ref, h_scr):
    @pl.when(pl.program_id(1) == 0)
    def _():
        h = _ln_rows(x_ref[...]) * (1.0 + sc_ref[...]) + sh_ref[...]
        h_scr[...] = h.astype(BF16)

    o_ref[...] = _dot(h_scr[...], w_ref[...]).astype(o_ref.dtype)


def _inproj(x, sc, sh, w, out_dtype, tm, tn):
    s, d = x.shape
    n = w.shape[1]
    return pl.pallas_call(
        _inproj_kernel,
        grid=(s // tm, n // tn),
        in_specs=[
            pl.BlockSpec((tm, d), lambda i, j: (i, 0)),
            pl.BlockSpec((1, d), lambda i, j: (0, 0)),
            pl.BlockSpec((1, d), lambda i, j: (0, 0)),
            pl.BlockSpec((d, tn), lambda i, j: (0, j)),
        ],
        out_specs=pl.BlockSpec((tm, tn), lambda i, j: (i, j)),
        out_shape=jax.ShapeDtypeStruct((s, n), out_dtype),
        scratch_shapes=[pltpu.VMEM((tm, d), BF16)],
        compiler_params=_cparams(("parallel", "arbitrary")),
        name="in_proj",
    )(x, sc, sh, w)


def _split_w_in(w):
    d = w.shape[0]
    o = 0
    q = w[:, o:o + _NSA_Q]; o += _NSA_Q
    kv = w[:, o:o + _NSA_KV]; o += _NSA_KV
    g = w[:, o:o + _NSA_G]; o += _NSA_G
    mlqk = w[:, o:o + _ML_QK]; o += _ML_QK
    mlv = w[:, o:o + _ML_V]; o += _ML_V
    mlif = w[:, o:o + _ML_IF]; o += _ML_IF
    mlo = w[:, o:o + _ML_V]; o += _ML_V
    merge = w[:, o:o + 2 * D_MODEL]
    per_g = 3 * NSA_HPG
    z = lambda n: jnp.zeros((d, n), w.dtype)
    wb = jnp.concatenate([q, kv[:, _CMP_COLS:], mlv], axis=1)
    wf = jnp.concatenate(
        [merge, kv[:, :_CMP_COLS], mlqk, mlo,
         g[:, :per_g], z(128 - per_g), g[:, per_g:], z(128 - per_g), mlif, z(128 - _ML_IF), z(128)],
        axis=1)
    return wb.astype(BF16), wf.astype(BF16)


def _compress_kernel(a_ref, pe_ref, w1_ref, w2_ref, o_ref):
    n = o_ref.shape[1]
    hd = NSA_HD

    def half_sum(lo):
        acc = jnp.zeros((n, CMP_HID), F32)
        for l in range(0, CMP_STRIDE, 2):
            x = jnp.concatenate(
                [a_ref[pl.ds(l + u, n, stride=CMP_STRIDE), :] + pe_ref[0, lo + l + u:lo + l + u + 1, :]
                 for u in range(2)], axis=1)
            w = w1_ref[0, (lo + l) * hd:(lo + l + 2) * hd, :]
            acc = acc + _dot(x.astype(BF16), w.astype(BF16))
        return acc

    pre = half_sum(0) + pltpu.roll(half_sum(CMP_STRIDE), n - 1, 0)
    g = _gelu_tanh(pre)
    o_ref[0] = _dot(g.astype(BF16), w2_ref[0].astype(BF16)).astype(o_ref.dtype)


def _compress(projf, pe, w1, w2):
    s = projf.shape[0]
    n = s // CMP_STRIDE
    four = 2 * NSA_GROUPS
    return pl.pallas_call(
        _compress_kernel,
        grid=(four,),
        in_specs=[
            pl.BlockSpec((s, NSA_HD), lambda j: (0, _F_CMP // NSA_HD + j)),
            pl.BlockSpec((1, CMP_LEN, NSA_HD), lambda j: (j // NSA_GROUPS, 0, 0)),
            pl.BlockSpec((1, CMP_LEN * NSA_HD, CMP_HID), lambda j: (j // NSA_GROUPS, 0, 0)),
            pl.BlockSpec((1, CMP_HID, NSA_HD), lambda j: (j // NSA_GROUPS, 0, 0)),
        ],
        out_specs=pl.BlockSpec((1, n, NSA_HD), lambda j: (j, 0, 0)),
        out_shape=jax.ShapeDtypeStruct((four, n, NSA_HD), BF16),
        compiler_params=_cparams(("parallel",)),
        name="nsa_compress",
    )(projf, pe, w1, w2)


def _nsa_kernel(q_ref, kc_ref, vc_ref, ks_ref, vs_ref, kw_ref, vw_ref, gate_ref, ovt_ref, eye_ref,
                o_ref, nsel_scr, npast_scr, *, step, nq):
    pid = pl.program_id(1)
    rows = NSA_HPG * Q_BLOCK
    scale = NSA_HD ** -0.5
    c2 = scale * math.log2(math.e)
    row_q = lax.broadcasted_iota(jnp.int32, (rows, 1), 0) & (Q_BLOCK - 1)
    lane_q = lax.broadcasted_iota(jnp.int32, (1, Q_BLOCK), 1)
    blk = []
    for u in range(nq):
        i = pid * nq + u
        qt = q_ref[u * Q_BLOCK:(u + 1) * Q_BLOCK, :]
        qs = jnp.concatenate([qt[:, h * NSA_HD:(h + 1) * NSA_HD] for h in range(NSA_HPG)], axis=0)
        blk.append((i, qs, i * Q_BLOCK + row_q))
    i_last = blk[-1][0]

    ncp = kc_ref.shape[1]

    def cmp_branch(width):
        outs = []
        cidx = lax.broadcasted_iota(jnp.int32, (rows, width), 1)
        ovt = ovt_ref[:, :width]
        for _, qs, t in blk:
            last_c = (t - (CMP_LEN - 1)) // CMP_STRIDE
            p_c = _masked_softmax(_dot_nt(qs, kc_ref[0, :width, :]), cidx <= last_c, scale)
            o = _dot(p_c.astype(BF16), vc_ref[0, :width, :])
            psum = (p_c[0:Q_BLOCK] + p_c[Q_BLOCK:2 * Q_BLOCK]
                    + p_c[2 * Q_BLOCK:3 * Q_BLOCK] + p_c[3 * Q_BLOCK:4 * Q_BLOCK])
            p_hi = psum.astype(BF16)
            p_lo = (psum - p_hi.astype(F32)).astype(BF16)
            outs += [o, _dot_nt(ovt, p_hi) + _dot_nt(ovt, p_lo)]
        return tuple(outs)

    n_vis = (i_last * Q_BLOCK + Q_BLOCK - CMP_LEN) // CMP_STRIDE + 1
    branch = lambda: cmp_branch(ncp)
    for width in (ncp // 2, ncp // 4):
        if width % 128 == 0:
            branch = (lambda w, other: lambda: lax.cond(n_vis <= w, lambda: cmp_branch(w), other))(width, branch)
    cmp_out = branch()
    o_cs, imps = cmp_out[0::2], cmp_out[1::2]

    wsl = imps[0].shape[0]
    bid = lax.broadcasted_iota(jnp.int32, (wsl, Q_BLOCK), 0)
    bidf = bid.astype(F32)
    taken = -1e9
    nsel0 = jnp.full((wsl, Q_BLOCK), -SEL_BIG, F32)
    score0s = []
    for (i, _, _), imp in zip(blk, imps):
        t_row = i * Q_BLOCK + lane_q
        cur = t_row // SLC_LEN
        score0s.append(jnp.where(bid == 0, 3e6, jnp.where(bid == cur, 2e6, jnp.where(
            bid == cur - 1, 1e6, jnp.where(bid * SLC_LEN <= t_row, imp, -1.0 - bidf)))))

    def take_max(c, lanes=None):
        sc, ns = c
        hit = sc == jnp.max(sc, axis=0, keepdims=True)
        if lanes is not None:
            hit = hit & lanes
        return jnp.where(hit, taken, sc), jnp.where(hit, 0.0, ns)

    cs = [(jnp.where(s0 >= 1e6, taken, s0), jnp.where(s0 >= 1e6, 0.0, nsel0)) for s0 in score0s]
    n_min = SLC_TOPK - 3
    for _ in range(n_min):
        cs = [take_max(c) for c in cs]

    def early_rounds():
        cur = lane_q // SLC_LEN
        n_free = SLC_TOPK - (1 + jnp.where(cur >= 1, 1, 0) + jnp.where(cur >= 2, 1, 0))
        e = cs[0]
        for r in range(n_min + 1, SLC_TOPK):
            e = take_max(e, n_free >= r)
        return e

    cs[0] = lax.cond(pid == 0, early_rounds, lambda: cs[0])
    nsels = tuple(c[1] for c in cs)

    def topk_ties():
        def one(_, c):
            sc, ns = c
            mx = jnp.max(sc, axis=0, keepdims=True)
            first = jnp.min(jnp.where(sc == mx, bidf, float(wsl)), axis=0, keepdims=True)
            hit = bidf == first
            return jnp.where(hit, taken, sc), jnp.where(hit, 0.0, ns)
        return tuple(lax.fori_loop(0, SLC_TOPK, one, (s0, nsel0))[1] for s0 in score0s)

    n_taken = functools.reduce(jnp.maximum, [jnp.sum(jnp.where(ns == 0.0, 1.0, 0.0), axis=0, keepdims=True)
                                             for ns in nsels])
    nsels = lax.cond(jnp.max(n_taken) > SLC_TOPK, topk_ties, lambda: nsels)
    for u, ((i, _, _), ns) in enumerate(zip(blk, nsels)):
        blk_d = (i * Q_BLOCK) // SLC_LEN
        nsel_scr[u] = ns
        npast_scr[u] = jnp.where(bid < blk_d, ns, -SEL_BIG)

    def keys_aug(k0, width, sel_scr):
        b0 = k0 // SLC_LEN
        bias = jnp.concatenate(
            [jnp.broadcast_to(sel_scr[pl.ds(b0 + b, 1), :], (SLC_LEN, Q_BLOCK))
             for b in range(width // SLC_LEN)], axis=0)
        return jnp.concatenate([ks_ref[pl.ds(k0, width), :], bias.astype(BF16)], axis=1)

    def vals_aug(k0, width):
        ones_col = (lax.broadcasted_iota(jnp.int32, (width, NSA_HD), 1) == 0).astype(BF16)
        return jnp.concatenate([vs_ref[pl.ds(k0, width), :], ones_col], axis=1)

    qas, diag, o_ws = [], [], []
    wlen = WINDOW + Q_BLOCK
    for u, (i, qs, t) in enumerate(blk):
        qa = jnp.concatenate([qs, eye_ref[...]], axis=1)
        k_d = pl.multiple_of(i * Q_BLOCK, Q_BLOCK)
        s_d = _dot_nt(qa, keys_aug(k_d, Q_BLOCK, nsel_scr.at[u]))
        s_d = jnp.where(k_d + lane_q <= t, s_d, NEG_INF)
        m_d = jnp.max(s_d, axis=-1, keepdims=True)
        acc_d = _dot(jnp.exp2((s_d - m_d) * c2).astype(BF16), vals_aug(k_d, Q_BLOCK))
        qas.append(qa)
        diag.append((m_d, acc_d))
        ks0 = pl.multiple_of(jnp.maximum(i - WINDOW // Q_BLOCK, 0) * Q_BLOCK, Q_BLOCK)
        s_w = _dot_nt(qs, kw_ref[pl.ds(ks0, wlen), :])
        kpos = ks0 + lax.broadcasted_iota(jnp.int32, (1, wlen), 1)
        p_w = _masked_softmax(s_w, (kpos <= t) & (kpos > t - WINDOW), scale)
        o_ws.append(_dot(p_w.astype(BF16), vw_ref[pl.ds(ks0, wlen), :]))

    gates = _sigmoid(gate_ref[...])
    for u, (i, qs, t) in enumerate(blk):
        qa = qas[u]
        past_scr = npast_scr.at[u]
        m_d, acc_d = diag[u]

        def exact_step(k0, carry):
            m, acc = carry
            k0 = pl.multiple_of(k0, SEL_TILE)
            s = _dot_nt(qa, keys_aug(k0, SEL_TILE, past_scr))
            m_new = jnp.maximum(m, jnp.max(s, axis=-1, keepdims=True))
            p = jnp.exp2((s - m_new) * c2)
            return m_new, jnp.exp2((m - m_new) * c2) * acc + _dot(p.astype(BF16), vals_aug(k0, SEL_TILE))

        def fast_step(k0, width, carry):
            m, acc, risk = carry
            k0 = pl.multiple_of(k0, width)
            s = _dot_nt(qa, keys_aug(k0, width, past_scr))
            p = jnp.exp2((s - m) * c2)
            mx = jnp.max(s, axis=-1, keepdims=True)
            m_new = jnp.maximum(m, mx)
            acc = (acc + _dot(p.astype(BF16), vals_aug(k0, width))) * jnp.exp2((m - m_new) * c2)
            return m_new, acc, jnp.maximum(risk, mx - m)

        n_past = i * Q_BLOCK
        n_steps = n_past // step
        carry = (m_d, acc_d, jnp.zeros((rows, 1), F32))
        carry = lax.fori_loop(0, n_steps, lambda j, c: fast_step(j * step, step, c), carry)
        rem = n_past - n_steps * step
        k_r = n_steps * step
        carry = lax.cond(
            rem > step // 2, lambda: fast_step(k_r, step, carry),
            lambda: lax.cond(rem > step // 4, lambda: fast_step(k_r, step // 2, carry),
                             lambda: lax.cond(rem > 0, lambda: fast_step(k_r, step // 4, carry),
                                              lambda: carry)))
        _, acc_s, risk = carry

        def redo_exact():
            n_tiles = (n_past + SEL_TILE - 1) // SEL_TILE
            return lax.fori_loop(0, n_tiles, lambda j, c: exact_step(j * SEL_TILE, c), (m_d, acc_d))[1]

        acc_s = lax.cond(jnp.max(risk) * c2 > SEL_EXP2_GUARD, redo_exact, lambda: acc_s)
        o_s = acc_s[:, :NSA_HD] / acc_s[:, NSA_HD:NSA_HD + 1]

        g_u = gates[u * Q_BLOCK:(u + 1) * Q_BLOCK]
        for h in range(NSA_HPG):
            r = slice(h * Q_BLOCK, (h + 1) * Q_BLOCK)
            out = (g_u[:, 3 * h:3 * h + 1] * o_cs[u][r] + g_u[:, 3 * h + 1:3 * h + 2] * o_s[r]
                   + g_u[:, 3 * h + 2:3 * h + 3] * o_ws[u][r])
            o_ref[u * Q_BLOCK:(u + 1) * Q_BLOCK, h * NSA_HD:(h + 1) * NSA_HD] = out.astype(o_ref.dtype)


def _nsa_attention(projb, projf, kcv, single_buffer=True):
    s = projb.shape[0]
    n_slc = s // SLC_LEN
    step = min(SEL_STEP, s)
    nq = NSA_Q_BLOCKS_PER_STEP
    qrows = nq * Q_BLOCK
    assert n_slc >= SLC_TOPK and s >= WINDOW + Q_BLOCK and s % step == 0 and step % (4 * SEL_TILE) == 0
    ncp = kcv.shape[1]
    wsl = max(n_slc, 128)
    hd = NSA_HD
    c_start = jnp.arange(ncp)[None, :] * CMP_STRIDE
    s_start = jnp.arange(wsl)[:, None] * SLC_LEN
    ovt = ((c_start < s_start + SLC_LEN) & (c_start + CMP_LEN > s_start)
           & (jnp.arange(wsl)[:, None] < n_slc)).astype(BF16)
    eye = jnp.tile(jnp.eye(Q_BLOCK, dtype=BF16), (NSA_HPG, 1))
    resident = dict(pipeline_mode=pl.Buffered(1)) if single_buffer else {}
    kvb = _B_KV // hd
    gb = _F_SMALL // 128
    return pl.pallas_call(
        functools.partial(_nsa_kernel, step=step, nq=nq),
        grid=(NSA_GROUPS, s // qrows),
        in_specs=[
            pl.BlockSpec((qrows, NSA_HPG * hd), lambda g, i: (i, g)),
            pl.BlockSpec((1, ncp, hd), lambda g, i: (g, 0, 0)),
            pl.BlockSpec((1, ncp, hd), lambda g, i: (NSA_GROUPS + g, 0, 0)),
            pl.BlockSpec((s, hd), lambda g, i: (0, kvb + g), **resident),
            pl.BlockSpec((s, hd), lambda g, i: (0, kvb + 2 + g), **resident),
            pl.BlockSpec((s, hd), lambda g, i: (0, kvb + 4 + g), **resident),
            pl.BlockSpec((s, hd), lambda g, i: (0, kvb + 6 + g), **resident),
            pl.BlockSpec((qrows, 128), lambda g, i: (i, gb + g)),
            pl.BlockSpec((wsl, ncp), lambda g, i: (0, 0)),
            pl.BlockSpec((NSA_HPG * Q_BLOCK, Q_BLOCK), lambda g, i: (0, 0)),
        ],
        out_specs=pl.BlockSpec((qrows, NSA_HPG * hd), lambda g, i: (i, g)),
        out_shape=jax.ShapeDtypeStruct((s, NSA_HEADS * hd), BF16),
        scratch_shapes=[pltpu.VMEM((nq, wsl, Q_BLOCK), F32), pltpu.VMEM((nq, wsl, Q_BLOCK), F32)],
        compiler_params=_cparams(("parallel", "arbitrary")),
        name="nsa_attention",
    )(projb, kcv, kcv, projb, projb, projb, projb, projf, ovt, eye)


def _mlstm_kernel(q_ref, k_ref, v_ref, o0_ref, o1_ref, gc_ref, gr_ref, cw_ref, cb_ref, ng_ref,
                  y_ref, extq, extk, c_st, n_st, m_st, *, cpb):
    step = pl.program_id(0)
    rows = cpb * ML_CHUNK
    half = ML_HEADS * ML_DQK
    L = ML_CHUNK

    @pl.when(step == 0)
    def _():
        extq[0:8, :] = jnp.zeros((8, half), F32)
        extk[0:8, :] = jnp.zeros((8, half), F32)
        c_st[...] = jnp.zeros_like(c_st)
        n_st[...] = jnp.zeros_like(n_st)
        m_st[...] = jnp.zeros_like(m_st)

    extq[8:8 + rows, :] = q_ref[...]
    extk[8:8 + rows, :] = k_ref[...]
    cw = cw_ref[...]
    cb = cb_ref[...]

    def conv(ext, lo):
        acc = cb[:, lo:lo + half]
        for j in range(ML_CONV):
            off = 8 - (ML_CONV - 1) + j
            acc = acc + cw[j:j + 1, lo:lo + half] * ext[off:off + rows, :]
        return _silu(acc)

    qa = conv(extq, 0)
    ka = conv(extk, half) * (ML_DQK ** -0.5)
    extq[0:8, :] = extq[rows:rows + 8, :]
    extk[0:8, :] = extk[rows:rows + 8, :]

    gcol = gc_ref[...]
    grow = gr_ref[...]
    lf_col_all = _log_sigmoid(gcol[:, ML_HEADS:])
    lf_row_all = _log_sigmoid(grow[ML_HEADS:, :])
    ri = lax.broadcasted_iota(jnp.int32, (L, L), 0)
    ci = lax.broadcasted_iota(jnp.int32, (L, L), 1)
    tri = ri >= ci
    ng = ng_ref[...]

    state = [(c_st[h], n_st[h], m_st[h][:, 0:1]) for h in range(ML_HEADS)]
    ones_blk = jnp.ones((L, 128), BF16)
    for c in range(cpb):
        r0 = c * L
        kt_c = ka[r0:r0 + L, :].T
        for h in range(ML_HEADS):
            qh = qa[r0:r0 + L, h * ML_DQK:(h + 1) * ML_DQK]
            kh = ka[r0:r0 + L, h * ML_DQK:(h + 1) * ML_DQK]
            vh = v_ref[r0:r0 + L, h * ML_DV:(h + 1) * ML_DV]
            ig_col = gcol[r0:r0 + L, h:h + 1]
            ig_row = grow[h:h + 1, r0:r0 + L]
            lf_col = lf_col_all[r0:r0 + L, h:h + 1]
            lf_row = lf_row_all[h:h + 1, r0:r0 + L]
            bcum_col = jnp.sum(jnp.where(tri, lf_row, 0.0), axis=1, keepdims=True)
            bcum_row = jnp.sum(jnp.where(ri <= ci, lf_col, 0.0), axis=0, keepdims=True)
            b_last = jnp.sum(lf_row, axis=1, keepdims=True)
            dmat = jnp.where(tri, bcum_col - bcum_row + ig_row, NEG_INF)
            a_loc = jnp.max(dmat, axis=-1, keepdims=True)
            qb = qh.astype(BF16)
            s_loc = _dot_nt(qb, kh.astype(BF16)) * jnp.exp(dmat - a_loc)
            sv = _dot(s_loc.astype(BF16), vh)
            s_sum = _dot(s_loc.astype(BF16), ones_blk)[:, 0:1]
            a_col = b_last - bcum_col + ig_col
            a_row = b_last - bcum_row + ig_row
            a_max = jnp.max(a_row, axis=-1, keepdims=True)
            k_sum = jnp.sum(kh * jnp.exp(a_col - a_max), axis=0, keepdims=True)
            kwt = kt_c[h * ML_DQK:(h + 1) * ML_DQK, :] * jnp.exp(a_row - a_max)
            kv = _dot(kwt.astype(BF16), vh)

            ct, nrow, m_old = state[h]
            inter = bcum_col + m_old
            m_t = jnp.maximum(inter, a_loc)
            w_inter = jnp.exp(inter - m_t)
            w_loc = jnp.exp(a_loc - m_t)
            num = w_inter * _dot(qb, ct.astype(BF16)) + w_loc * sv
            qn = _dot_nt(qb, jnp.broadcast_to(nrow, (8, ML_DQK)).astype(BF16))[:, 0:1]
            den = w_inter * qn + w_loc * s_sum
            hout = num / jnp.maximum(jnp.abs(den), jnp.exp(-m_t))

            m_new = jnp.maximum(b_last + m_old, a_max)
            decay = jnp.exp(b_last + m_old - m_new)
            g_new = jnp.exp(a_max - m_new)
            state[h] = (decay * ct + g_new * kv, decay * nrow + g_new * k_sum, m_new)

            o_ref = o0_ref if h < ML_HEADS // 2 else o1_ref
            oc = (h % (ML_HEADS // 2)) * ML_DV
            og = _sigmoid(o_ref[r0:r0 + L, oc:oc + ML_DV])
            yn = _ln_rows(hout) * ng[:, h * ML_DV:(h + 1) * ML_DV]
            y_ref[r0:r0 + L, h * ML_DV:(h + 1) * ML_DV] = (yn * og).astype(y_ref.dtype)

    for h in range(ML_HEADS):
        c_st[h], n_st[h] = state[h][0], state[h][1]
        m_st[h] = jnp.broadcast_to(state[h][2], (1, 128))


def _mlstm(projb, projf, gates_col, gates_row, conv_w, conv_b, norm_g, cpb):
    s = projb.shape[0]
    rows = cpb * ML_CHUNK
    half = ML_HEADS * ML_DQK
    dv = ML_HEADS * ML_DV
    qblk = _F_MLQK // half
    oblk = _F_MLO // half
    return pl.pallas_call(
        functools.partial(_mlstm_kernel, cpb=cpb),
        grid=(s // rows,),
        in_specs=[
            pl.BlockSpec((rows, half), lambda j: (j, qblk)),
            pl.BlockSpec((rows, half), lambda j: (j, qblk + 1)),
            pl.BlockSpec((rows, dv), lambda j: (j, _B_MLV // dv)),
            pl.BlockSpec((rows, half), lambda j: (j, oblk)),
            pl.BlockSpec((rows, half), lambda j: (j, oblk + 1)),
            pl.BlockSpec((rows, 2 * ML_HEADS), lambda j: (j, 0)),
            pl.BlockSpec((2 * ML_HEADS, rows), lambda j: (0, j)),
            pl.BlockSpec((ML_CONV, 2 * half), lambda j: (0, 0)),
            pl.BlockSpec((1, 2 * half), lambda j: (0, 0)),
            pl.BlockSpec((1, dv), lambda j: (0, 0)),
        ],
        out_specs=pl.BlockSpec((rows, dv), lambda j: (j, 0)),
        out_shape=jax.ShapeDtypeStruct((s, dv), BF16),
        scratch_shapes=[
            pltpu.VMEM((rows + 8, half), F32),
            pltpu.VMEM((rows + 8, half), F32),
            pltpu.VMEM((ML_HEADS, ML_DQK, ML_DV), F32),
            pltpu.VMEM((ML_HEADS, 1, ML_DQK), F32),
            pltpu.VMEM((ML_HEADS, 1, 128), F32),
        ],
        compiler_params=_cparams(("arbitrary",)),
        name="mlstm",
    )(projf, projf, projb, projf, projf, gates_col, gates_row, conv_w, conv_b, norm_g)


def _merge_kernel(yn_ref, ym_ref, g0_ref, g1_ref, x_ref, wn_ref, wm_ref, wo_ref, gate_ref,
                  lg_ref, lb_ref, o_ref):
    a = _dot(yn_ref[...], wn_ref[...])
    b = _dot(ym_ref[...], wm_ref[...])
    merged = _sigmoid(g0_ref[...]) * a + _sigmoid(g1_ref[...]) * b
    y = _dot(merged.astype(BF16), wo_ref[...])
    z = ALPHA * x_ref[...] + gate_ref[...] * y
    o_ref[...] = _ln_rows(z) * lg_ref[...] + lb_ref[...]


def _merge_outproj(y_nsa, y_ml, projf, x, wn, wm, wo, gate, ln_g, ln_b, tm, single_buffer=True):
    s, d = x.shape
    resident = dict(pipeline_mode=pl.Buffered(1)) if single_buffer else {}
    mb = _F_MERGE // d
    row = lambda i: (0, 0)
    return pl.pallas_call(
        _merge_kernel,
        grid=(s // tm,),
        in_specs=[
            pl.BlockSpec((tm, y_nsa.shape[1]), lambda i: (i, 0)),
            pl.BlockSpec((tm, y_ml.shape[1]), lambda i: (i, 0)),
            pl.BlockSpec((tm, d), lambda i: (i, mb)),
            pl.BlockSpec((tm, d), lambda i: (i, mb + 1)),
            pl.BlockSpec((tm, d), lambda i: (i, 0)),
            pl.BlockSpec(wn.shape, row, **resident),
            pl.BlockSpec(wm.shape, row, **resident),
            pl.BlockSpec(wo.shape, row, **resident),
            pl.BlockSpec((1, d), row),
            pl.BlockSpec((1, d), row),
            pl.BlockSpec((1, d), row),
        ],
        out_specs=pl.BlockSpec((tm, d), lambda i: (i, 0)),
        out_shape=jax.ShapeDtypeStruct((s, d), F32),
        compiler_params=_cparams(("parallel",)),
        name="merge_outproj",
    )(y_nsa, y_ml, projf, projf, x, wn, wm, wo, gate, ln_g, ln_b)


def _ffn_kernel(x_ref, xh_ref, sc_ref, sh_ref, gate_ref, wa_ref, wg_ref, cw_ref, cb_ref, wd_ref,
                lg_ref, lb_ref, o_ref, h_scr, a_scr, acc, *, halo):
    i = pl.program_id(0)
    f = pl.program_id(1)
    tm = x_ref.shape[0]

    @pl.when(f == 0)
    def _():
        mod = lambda v: (_ln_rows(v) * (1.0 + sc_ref[...]) + sh_ref[...]).astype(BF16)
        h_scr[0:halo, :] = mod(xh_ref[...])
        h_scr[halo:halo + tm, :] = mod(x_ref[...])
        acc[...] = jnp.zeros_like(acc)

    hx = h_scr[...]
    a_ext = _dot(hx, wa_ref[...])
    rid = lax.broadcasted_iota(jnp.int32, (halo + tm, 1), 0)
    a_scr[...] = jnp.where((rid >= halo) | (i > 0), a_ext, 0.0)
    cw = cw_ref[...]
    conv = cb_ref[...]
    for j in range(FFN_CONV):
        off = halo - (FFN_CONV - 1) + j
        conv = conv + cw[j:j + 1, :] * a_scr[off:off + tm, :]
    g = _dot(h_scr[halo:halo + tm, :], wg_ref[...])
    act = (_silu(conv) * g).astype(BF16)
    acc[...] += _dot(act, wd_ref[...])

    @pl.when(f == pl.num_programs(1) - 1)
    def _():
        z = ALPHA * x_ref[...] + gate_ref[...] * acc[...]
        o_ref[...] = _ln_rows(z) * lg_ref[...] + lb_ref[...]


def _conv_ffn(x, sc, sh, gate, w_up, conv_w, conv_b, w_down, ln_g, ln_b, tm, tf):
    s, d = x.shape
    dff = w_down.shape[0]
    halo = 16
    nf = dff // tf
    row = lambda i, f: (0, 0)
    return pl.pallas_call(
        functools.partial(_ffn_kernel, halo=halo),
        grid=(s // tm, nf),
        in_specs=[
            pl.BlockSpec((tm, d), lambda i, f: (i, 0)),
            pl.BlockSpec((halo, d), lambda i, f: (jnp.maximum(i * (tm // halo) - 1, 0), 0)),
            pl.BlockSpec((1, d), row),
            pl.BlockSpec((1, d), row),
            pl.BlockSpec((1, d), row),
            pl.BlockSpec((d, tf), lambda i, f: (0, f)),
            pl.BlockSpec((d, tf), lambda i, f: (0, nf + f)),
            pl.BlockSpec((FFN_CONV, tf), lambda i, f: (0, f)),
            pl.BlockSpec((1, tf), lambda i, f: (0, f)),
            pl.BlockSpec((tf, d), lambda i, f: (f, 0)),
            pl.BlockSpec((1, d), row),
            pl.BlockSpec((1, d), row),
        ],
        out_specs=pl.BlockSpec((tm, d), lambda i, f: (i, 0)),
        out_shape=jax.ShapeDtypeStruct((s, d), F32),
        scratch_shapes=[
            pltpu.VMEM((halo + tm, d), BF16),
            pltpu.VMEM((halo + tm, tf), F32),
            pltpu.VMEM((tm, d), F32),
        ],
        compiler_params=_cparams(("parallel", "arbitrary")),
        name="conv_ffn",
    )(x, x, sc, sh, gate, w_up, w_up, conv_w, conv_b, w_down, ln_g, ln_b)


def _token_mixer(x, sc, sh, gate, w_in, cmp_pe, cmp_w1, cmp_w2, ml_conv_w, ml_conv_b, ml_gate_b,
                 ml_norm_g, w_br_nsa, w_br_ml, w_o, ln_g, ln_b):
    s, d = x.shape
    wb, wf = _split_w_in(w_in)
    tm = min(1024, s)
    projb = _inproj(x, sc, sh, wb, BF16, tm, 1024)
    projf = _inproj(x, sc, sh, wf, F32, tm, 1024)
    kcv = _compress(projf, cmp_pe, cmp_w1, cmp_w2)
    y_nsa = _nsa_attention(projb, projf, kcv)
    ifo = _F_SMALL + 2 * 128
    gates_col = projf[:, ifo:ifo + _ML_IF] + ml_gate_b[None, :]
    y_ml = _mlstm(projb, projf, gates_col, gates_col.T, ml_conv_w, ml_conv_b[None, :],
                  ml_norm_g[None, :], cpb=min(4, s // ML_CHUNK))
    return _merge_outproj(y_nsa, y_ml, projf, x, w_br_nsa.astype(BF16), w_br_ml.astype(BF16),
                          w_o.astype(BF16), gate, ln_g, ln_b, tm=min(256, s))


def _forward(x, c, w_ada, b_ada, w_in, cmp_pe, cmp_w1, cmp_w2, ml_conv_w, ml_conv_b, ml_gate_b,
             ml_norm_g, w_br_nsa, w_br_ml, w_o, w_up, ffn_conv_w, ffn_conv_b, w_down, ln_g, ln_b):
    b, s, d = x.shape
    assert b == 1 and d == D_MODEL
    depth = w_ada.shape[0]
    mod = _modulation(c, w_ada, b_ada)
    xs = x[0]
    for l in range(depth):
        sh1, sc1, g1, sh2, sc2, g2 = [mod[l, :, k * d:(k + 1) * d] for k in range(6)]
        xs = _token_mixer(xs, sc1, sh1, g1, w_in[l], cmp_pe[l], cmp_w1[l], cmp_w2[l], ml_conv_w[l],
                          ml_conv_b[l], ml_gate_b[l], ml_norm_g[l], w_br_nsa[l], w_br_ml[l], w_o[l],
                          ln_g[l, 0][None, :], ln_b[l, 0][None, :])
        xs = _conv_ffn(xs, sc2, sh2, g2, w_up[l].astype(BF16), ffn_conv_w[l], ffn_conv_b[l][None, :],
                       w_down[l].astype(BF16), ln_g[l, 1][None, :], ln_b[l, 1][None, :],
                       tm=min(512, s), tf=512)
    return xs[None]


def kernel(x, c, w_ada, b_ada, w_in, cmp_pe, cmp_w1, cmp_w2, ml_conv_w, ml_conv_b, ml_gate_b, ml_norm_g, w_br_nsa, w_br_ml, w_o, w_up, ffn_conv_w, ffn_conv_b, w_down, ln_g, ln_b):
    return _forward(x, c, w_ada, b_ada, w_in, cmp_pe, cmp_w1, cmp_w2, ml_conv_w, ml_conv_b,
                    ml_gate_b, ml_norm_g, w_br_nsa, w_br_ml, w_o, w_up, ffn_conv_w, ffn_conv_b,
                    w_down, ln_g, ln_b)
```

```python
import functools
import math

import jax
import jax.numpy as jnp
from jax import lax
from jax.experimental import pallas as pl
from jax.experimental.pallas import tpu as pltpu

F32 = jnp.float32
BF16 = jnp.bfloat16

D_MODEL = 2048
DEPTH = 2
NSA_HEADS = 8
NSA_GROUPS = 2
NSA_HPG = NSA_HEADS // NSA_GROUPS
NSA_HD = 128
CMP_LEN = 32
CMP_STRIDE = 16
CMP_HID = 256
SLC_LEN = 64
SLC_TOPK = 16
WINDOW = 512
Q_BLOCK = 128
ML_HEADS = 4
ML_DQK = 128
ML_DV = 256
ML_CHUNK = 64
ML_CONV = 4
D_FF = 5632
FFN_CONV = 3
ALPHA = (2 * DEPTH) ** 0.25
LN_EPS = 1e-5
NEG_INF = -1e30

V7X_VMEM_BYTES = 64 * 1024 * 1024
VMEM_LIMIT = 56 * 1024 * 1024

_NSA_Q = NSA_HEADS * NSA_HD
_NSA_KV = 3 * 2 * NSA_GROUPS * NSA_HD
_NSA_G = 3 * NSA_HEADS
_ML_QK = 2 * ML_HEADS * ML_DQK
_ML_V = ML_HEADS * ML_DV
_ML_IF = 2 * ML_HEADS
_CMP_COLS = 2 * NSA_GROUPS * NSA_HD
_F_MERGE = 0
_F_CMP = 2 * D_MODEL
_F_MLQK = _F_CMP + _CMP_COLS
_F_MLO = _F_MLQK + _ML_QK
_F_SMALL = _F_MLO + _ML_V
_F_COLS = _F_SMALL + 4 * 128
_B_Q = 0
_B_KV = _NSA_Q
_B_MLV = _B_KV + 8 * NSA_HD
_B_COLS = _B_MLV + _ML_V

SEL_TILE = 512
SEL_STEP = 4096
SEL_BIG = 2.0 ** 60
NSA_Q_BLOCKS_PER_STEP = 2
SEL_EXP2_GUARD = 64.0


def _cparams(sem):
    return pltpu.CompilerParams(dimension_semantics=sem, vmem_limit_bytes=VMEM_LIMIT)


def _ln_rows(x):
    mu = jnp.mean(x, axis=-1, keepdims=True)
    xc = x - mu
    var = jnp.mean(xc * xc, axis=-1, keepdims=True)
    return xc * lax.rsqrt(var + LN_EPS)


def _sigmoid(x):
    return 1.0 / (1.0 + jnp.exp(-x))


def _silu(x):
    return x * _sigmoid(x)


def _gelu_tanh(x):
    c = math.sqrt(2.0 / math.pi)
    return x * (0.5 * (1.0 + jnp.tanh(c * (x + 0.044715 * (x * x * x)))))


def _log_sigmoid(x):
    return jnp.minimum(x, 0.0) - jnp.log1p(jnp.exp(-jnp.abs(x)))


def _dot(a, b):
    return jnp.dot(a, b, preferred_element_type=F32)


def _dot_nt(a, b):
    return lax.dot_general(a, b, (((1,), (1,)), ((), ())), preferred_element_type=F32)


def _dot_tn(a, b):
    return lax.dot_general(a, b, (((0,), (0,)), ((), ())), preferred_element_type=F32)


def _masked_softmax(raw, mask, scale):
    raw = jnp.where(mask, raw, NEG_INF)
    m = jnp.max(raw, axis=-1, keepdims=True)
    e = jnp.exp2((raw - m) * (scale * math.log2(math.e)))
    den = jnp.sum(e, axis=-1, keepdims=True)
    return e * jnp.where(m > 0.5 * NEG_INF, 1.0 / den, 0.0)


def _mod_kernel(c_ref, w_ref, b_ref, o_ref):
    ca = _silu(c_ref[...])
    o = _dot(ca.astype(BF16), w_ref[0].astype(BF16))
    o_ref[0] = o[0:1] + b_ref[0]


def _modulation(c, w_ada, b_ada):
    depth, d, n = w_ada.shape
    tn = 1024
    c8 = jnp.broadcast_to(c, (8, d))
    return pl.pallas_call(
        _mod_kernel,
        grid=(depth, n // tn),
        in_specs=[
            pl.BlockSpec((8, d), lambda l, j: (0, 0)),
            pl.BlockSpec((1, d, tn), lambda l, j: (l, 0, j)),
            pl.BlockSpec((1, 1, tn), lambda l, j: (l, 0, j)),
        ],
        out_specs=pl.BlockSpec((1, 1, tn), lambda l, j: (l, 0, j)),
        out_shape=jax.ShapeDtypeStruct((depth, 1, n), F32),
        compiler_params=_cparams(("parallel", "parallel")),
        name="adaln_mod",
    )(c8, w_ada, b_ada.reshape(depth, 1, n))


def _inproj_kernel(x_ref, sc_ref, sh_ref, w_ref, ob_ref, of_ref, h_scr, *, nb):
    j = pl.program_id(1)

    @pl.when(j == 0)
    def _():
        h = _ln_rows(x_ref[...]) * (1.0 + sc_ref[...]) + sh_ref[...]
        h_scr[...] = h.astype(BF16)

    @pl.when(j < nb)
    def _():
        ob_ref[...] = _dot(h_scr[...], w_ref[...]).astype(ob_ref.dtype)

    @pl.when(j >= nb)
    def _():
        of_ref[...] = _dot(h_scr[...], w_ref[...])


def _inproj(x, sc, sh, w, tm, tn):
    s, d = x.shape
    nb, nf = _B_COLS // tn, _F_COLS // tn
    assert w.shape[1] == _B_COLS + _F_COLS and _B_COLS % tn == 0 and _F_COLS % tn == 0
    return pl.pallas_call(
        functools.partial(_inproj_kernel, nb=nb),
        grid=(s // tm, nb + nf),
        in_specs=[
            pl.BlockSpec((tm, d), lambda i, j: (i, 0)),
            pl.BlockSpec((1, d), lambda i, j: (0, 0)),
            pl.BlockSpec((1, d), lambda i, j: (0, 0)),
            pl.BlockSpec((d, tn), lambda i, j: (0, j)),
        ],
        out_specs=[pl.BlockSpec((tm, tn), lambda i, j: (i, jnp.minimum(j, nb - 1))),
                   pl.BlockSpec((tm, tn), lambda i, j: (i, jnp.maximum(j - nb, 0)))],
        out_shape=[jax.ShapeDtypeStruct((s, _B_COLS), BF16), jax.ShapeDtypeStruct((s, _F_COLS), F32)],
        scratch_shapes=[pltpu.VMEM((tm, d), BF16)],
        compiler_params=_cparams(("parallel", "arbitrary")),
        name="in_proj",
    )(x, sc, sh, w)


def _arrange_w_in(w):
    d = w.shape[0]
    o = 0
    q = w[:, o:o + _NSA_Q]; o += _NSA_Q
    kv = w[:, o:o + _NSA_KV]; o += _NSA_KV
    g = w[:, o:o + _NSA_G]; o += _NSA_G
    mlqk = w[:, o:o + _ML_QK]; o += _ML_QK
    mlv = w[:, o:o + _ML_V]; o += _ML_V
    mlif = w[:, o:o + _ML_IF]; o += _ML_IF
    mlo = w[:, o:o + _ML_V]; o += _ML_V
    merge = w[:, o:o + 2 * D_MODEL]
    per_g = 3 * NSA_HPG
    z = lambda n: jnp.zeros((d, n), w.dtype)
    return jnp.concatenate(
        [q, kv[:, _CMP_COLS:], mlv,
         merge, kv[:, :_CMP_COLS], mlqk, mlo,
         g[:, :per_g], z(128 - per_g), g[:, per_g:], z(128 - per_g), mlif, z(128 - _ML_IF), z(128)],
        axis=1).astype(BF16)


def _compress_kernel(a_ref, pe_ref, w1_ref, w2_ref, o_ref):
    n = o_ref.shape[1]
    hd = NSA_HD

    def half_sum(lo):
        acc = jnp.zeros((n, CMP_HID), F32)
        for l in range(0, CMP_STRIDE, 2):
            x = jnp.concatenate(
                [a_ref[pl.ds(l + u, n, stride=CMP_STRIDE), :] + pe_ref[0, lo + l + u:lo + l + u + 1, :]
                 for u in range(2)], axis=1)
            w = w1_ref[0, (lo + l) * hd:(lo + l + 2) * hd, :]
            acc = acc + _dot(x.astype(BF16), w.astype(BF16))
        return acc

    pre = half_sum(0) + pltpu.roll(half_sum(CMP_STRIDE), n - 1, 0)
    g = _gelu_tanh(pre)
    o_ref[0] = _dot(g.astype(BF16), w2_ref[0].astype(BF16)).astype(o_ref.dtype)


def _compress(projf, pe, w1, w2):
    s = projf.shape[0]
    n = s // CMP_STRIDE
    four = 2 * NSA_GROUPS
    return pl.pallas_call(
        _compress_kernel,
        grid=(four,),
        in_specs=[
            pl.BlockSpec((s, NSA_HD), lambda j: (0, _F_CMP // NSA_HD + j)),
            pl.BlockSpec((1, CMP_LEN, NSA_HD), lambda j: (j // NSA_GROUPS, 0, 0)),
            pl.BlockSpec((1, CMP_LEN * NSA_HD, CMP_HID), lambda j: (j // NSA_GROUPS, 0, 0)),
            pl.BlockSpec((1, CMP_HID, NSA_HD), lambda j: (j // NSA_GROUPS, 0, 0)),
        ],
        out_specs=pl.BlockSpec((1, n, NSA_HD), lambda j: (j, 0, 0)),
        out_shape=jax.ShapeDtypeStruct((four, n, NSA_HD), BF16),
        compiler_params=_cparams(("parallel",)),
        name="nsa_compress",
    )(projf, pe, w1, w2)


def _nsa_kernel(q_ref, kc_ref, vc_ref, ks_ref, vs_ref, kw_ref, vw_ref, gate_ref, ovt_ref, eye_ref,
                o_ref, nsel_scr, npast_scr, *, step, nq):
    pid = pl.program_id(1)
    rows = NSA_HPG * Q_BLOCK
    scale = NSA_HD ** -0.5
    c2 = scale * math.log2(math.e)
    row_q = lax.broadcasted_iota(jnp.int32, (rows, 1), 0) & (Q_BLOCK - 1)
    lane_q = lax.broadcasted_iota(jnp.int32, (1, Q_BLOCK), 1)
    blk = []
    for u in range(nq):
        i = pid * nq + u
        qt = q_ref[u * Q_BLOCK:(u + 1) * Q_BLOCK, :]
        qs = jnp.concatenate([qt[:, h * NSA_HD:(h + 1) * NSA_HD] for h in range(NSA_HPG)], axis=0)
        blk.append((i, qs, i * Q_BLOCK + row_q))
    i_last = blk[-1][0]

    ncp = kc_ref.shape[1]

    def cmp_branch(width):
        outs = []
        cidx = lax.broadcasted_iota(jnp.int32, (rows, width), 1)
        ovt = ovt_ref[:, :width]
        for _, qs, t in blk:
            last_c = (t - (CMP_LEN - 1)) // CMP_STRIDE
            p_c = _masked_softmax(_dot_nt(qs, kc_ref[0, :width, :]), cidx <= last_c, scale)
            o = _dot(p_c.astype(BF16), vc_ref[0, :width, :])
            psum = (p_c[0:Q_BLOCK] + p_c[Q_BLOCK:2 * Q_BLOCK]
                    + p_c[2 * Q_BLOCK:3 * Q_BLOCK] + p_c[3 * Q_BLOCK:4 * Q_BLOCK])
            p_hi = psum.astype(BF16)
            p_lo = (psum - p_hi.astype(F32)).astype(BF16)
            outs += [o, _dot_nt(ovt, p_hi) + _dot_nt(ovt, p_lo)]
        return tuple(outs)

    n_vis = (i_last * Q_BLOCK + Q_BLOCK - CMP_LEN) // CMP_STRIDE + 1
    branch = lambda: cmp_branch(ncp)
    for width in (ncp // 2, ncp // 4):
        if width % 128 == 0:
            branch = (lambda w, other: lambda: lax.cond(n_vis <= w, lambda: cmp_branch(w), other))(width, branch)
    cmp_out = branch()
    o_cs, imps = cmp_out[0::2], cmp_out[1::2]

    wsl = imps[0].shape[0]
    bid = lax.broadcasted_iota(jnp.int32, (wsl, Q_BLOCK), 0)
    bidf = bid.astype(F32)
    taken = -1e9
    nsel0 = jnp.full((wsl, Q_BLOCK), -SEL_BIG, F32)
    score0s = []
    for (i, _, _), imp in zip(blk, imps):
        t_row = i * Q_BLOCK + lane_q
        cur = t_row // SLC_LEN
        score0s.append(jnp.where(bid == 0, 3e6, jnp.where(bid == cur, 2e6, jnp.where(
            bid == cur - 1, 1e6, jnp.where(bid * SLC_LEN <= t_row, imp, -1.0 - bidf)))))

    def take_max(c, lanes=None):
        sc, ns = c
        hit = sc == jnp.max(sc, axis=0, keepdims=True)
        if lanes is not None:
            hit = hit & lanes
        return jnp.where(hit, taken, sc), jnp.where(hit, 0.0, ns)

    cs = [(jnp.where(s0 >= 1e6, taken, s0), jnp.where(s0 >= 1e6, 0.0, nsel0)) for s0 in score0s]
    n_min = SLC_TOPK - 3
    for _ in range(n_min):
        cs = [take_max(c) for c in cs]

    def early_rounds():
        cur = lane_q // SLC_LEN
        n_free = SLC_TOPK - (1 + jnp.where(cur >= 1, 1, 0) + jnp.where(cur >= 2, 1, 0))
        e = cs[0]
        for r in range(n_min + 1, SLC_TOPK):
            e = take_max(e, n_free >= r)
        return e

    cs[0] = lax.cond(pid == 0, early_rounds, lambda: cs[0])
    nsels = tuple(c[1] for c in cs)

    def topk_ties():
        def one(_, c):
            sc, ns = c
            mx = jnp.max(sc, axis=0, keepdims=True)
            first = jnp.min(jnp.where(sc == mx, bidf, float(wsl)), axis=0, keepdims=True)
            hit = bidf == first
            return jnp.where(hit, taken, sc), jnp.where(hit, 0.0, ns)
        return tuple(lax.fori_loop(0, SLC_TOPK, one, (s0, nsel0))[1] for s0 in score0s)

    n_taken = functools.reduce(jnp.maximum, [jnp.sum(jnp.where(ns == 0.0, 1.0, 0.0), axis=0, keepdims=True)
                                             for ns in nsels])
    nsels = lax.cond(jnp.max(n_taken) > SLC_TOPK, topk_ties, lambda: nsels)
    for u, ((i, _, _), ns) in enumerate(zip(blk, nsels)):
        blk_d = (i * Q_BLOCK) // SLC_LEN
        nsel_scr[u] = ns
        npast_scr[u] = jnp.where(bid < blk_d, ns, -SEL_BIG)

    def keys_aug(k0, width, sel_scr):
        b0 = k0 // SLC_LEN
        bias = jnp.concatenate(
            [jnp.broadcast_to(sel_scr[pl.ds(b0 + b, 1), :], (SLC_LEN, Q_BLOCK))
             for b in range(width // SLC_LEN)], axis=0)
        return jnp.concatenate([ks_ref[pl.ds(k0, width), :], bias.astype(BF16)], axis=1)

    def vals_aug(k0, width):
        ones_col = (lax.broadcasted_iota(jnp.int32, (width, NSA_HD), 1) == 0).astype(BF16)
        return jnp.concatenate([vs_ref[pl.ds(k0, width), :], ones_col], axis=1)

    qas, diag, o_ws = [], [], []
    wlen = WINDOW + Q_BLOCK
    for u, (i, qs, t) in enumerate(blk):
        qa = jnp.concatenate([qs, eye_ref[...]], axis=1)
        k_d = pl.multiple_of(i * Q_BLOCK, Q_BLOCK)
        s_d = _dot_nt(qa, keys_aug(k_d, Q_BLOCK, nsel_scr.at[u]))
        s_d = jnp.where(k_d + lane_q <= t, s_d, NEG_INF)
        m_d = jnp.max(s_d, axis=-1, keepdims=True)
        acc_d = _dot(jnp.exp2((s_d - m_d) * c2).astype(BF16), vals_aug(k_d, Q_BLOCK))
        qas.append(qa)
        diag.append((m_d, acc_d))
        ks0 = pl.multiple_of(jnp.maximum(i - WINDOW // Q_BLOCK, 0) * Q_BLOCK, Q_BLOCK)
        s_w = _dot_nt(qs, kw_ref[pl.ds(ks0, wlen), :])
        kpos = ks0 + lax.broadcasted_iota(jnp.int32, (1, wlen), 1)
        p_w = _masked_softmax(s_w, (kpos <= t) & (kpos > t - WINDOW), scale)
        o_ws.append(_dot(p_w.astype(BF16), vw_ref[pl.ds(ks0, wlen), :]))

    gates = _sigmoid(gate_ref[...])
    for u, (i, qs, t) in enumerate(blk):
        qa = qas[u]
        past_scr = npast_scr.at[u]
        m_d, acc_d = diag[u]

        def exact_step(k0, carry):
            m, acc = carry
            k0 = pl.multiple_of(k0, SEL_TILE)
            s = _dot_nt(qa, keys_aug(k0, SEL_TILE, past_scr))
            m_new = jnp.maximum(m, jnp.max(s, axis=-1, keepdims=True))
            p = jnp.exp2((s - m_new) * c2)
            return m_new, jnp.exp2((m - m_new) * c2) * acc + _dot(p.astype(BF16), vals_aug(k0, SEL_TILE))

        def fast_step(k0, width, carry):
            m, acc, risk = carry
            k0 = pl.multiple_of(k0, SEL_TILE)
            s = _dot_nt(qa, keys_aug(k0, width, past_scr))
            p = jnp.exp2((s - m) * c2)
            mx = jnp.max(s, axis=-1, keepdims=True)
            m_new = jnp.maximum(m, mx)
            acc = (acc + _dot(p.astype(BF16), vals_aug(k0, width))) * jnp.exp2((m - m_new) * c2)
            return m_new, acc, jnp.maximum(risk, mx - m)

        n_past = i * Q_BLOCK
        n_steps = n_past // step
        carry = (m_d, acc_d, jnp.zeros((rows, 1), F32))
        carry = lax.fori_loop(0, n_steps, lambda j, c: fast_step(j * step, step, c), carry)
        rem = n_past - n_steps * step
        k_r = n_steps * step
        widths = sorted({max(SEL_TILE, step * f // 8) for f in (1, 2, 4, 6, 8)})
        tail = lambda: fast_step(k_r, widths[-1], carry)
        for w in reversed(widths[:-1]):
            tail = (lambda w, other: lambda: lax.cond(rem <= w, lambda: fast_step(k_r, w, carry), other))(w, tail)
        carry = lax.cond(rem > 0, tail, lambda: carry)
        _, acc_s, risk = carry

        def redo_exact():
            n_tiles = (n_past + SEL_TILE - 1) // SEL_TILE
            return lax.fori_loop(0, n_tiles, lambda j, c: exact_step(j * SEL_TILE, c), (m_d, acc_d))[1]

        acc_s = lax.cond(jnp.max(risk) * c2 > SEL_EXP2_GUARD, redo_exact, lambda: acc_s)
        o_s = acc_s[:, :NSA_HD] / acc_s[:, NSA_HD:NSA_HD + 1]

        g_u = gates[u * Q_BLOCK:(u + 1) * Q_BLOCK]
        for h in range(NSA_HPG):
            r = slice(h * Q_BLOCK, (h + 1) * Q_BLOCK)
            out = (g_u[:, 3 * h:3 * h + 1] * o_cs[u][r] + g_u[:, 3 * h + 1:3 * h + 2] * o_s[r]
                   + g_u[:, 3 * h + 2:3 * h + 3] * o_ws[u][r])
            o_ref[u * Q_BLOCK:(u + 1) * Q_BLOCK, h * NSA_HD:(h + 1) * NSA_HD] = out.astype(o_ref.dtype)


def _nsa_attention(projb, projf, kcv, single_buffer=True):
    s = projb.shape[0]
    n_slc = s // SLC_LEN
    step = min(SEL_STEP, s)
    nq = NSA_Q_BLOCKS_PER_STEP
    qrows = nq * Q_BLOCK
    assert n_slc >= SLC_TOPK and s >= WINDOW + Q_BLOCK and s % step == 0 and step % (4 * SEL_TILE) == 0
    ncp = kcv.shape[1]
    wsl = max(n_slc, 128)
    hd = NSA_HD
    c_start = jnp.arange(ncp)[None, :] * CMP_STRIDE
    s_start = jnp.arange(wsl)[:, None] * SLC_LEN
    ovt = ((c_start < s_start + SLC_LEN) & (c_start + CMP_LEN > s_start)
           & (jnp.arange(wsl)[:, None] < n_slc)).astype(BF16)
    eye = jnp.tile(jnp.eye(Q_BLOCK, dtype=BF16), (NSA_HPG, 1))
    resident = dict(pipeline_mode=pl.Buffered(1)) if single_buffer else {}
    kvb = _B_KV // hd
    gb = _F_SMALL // 128
    return pl.pallas_call(
        functools.partial(_nsa_kernel, step=step, nq=nq),
        grid=(NSA_GROUPS, s // qrows),
        in_specs=[
            pl.BlockSpec((qrows, NSA_HPG * hd), lambda g, i: (i, g)),
            pl.BlockSpec((1, ncp, hd), lambda g, i: (g, 0, 0)),
            pl.BlockSpec((1, ncp, hd), lambda g, i: (NSA_GROUPS + g, 0, 0)),
            pl.BlockSpec((s, hd), lambda g, i: (0, kvb + g), **resident),
            pl.BlockSpec((s, hd), lambda g, i: (0, kvb + 2 + g), **resident),
            pl.BlockSpec((s, hd), lambda g, i: (0, kvb + 4 + g), **resident),
            pl.BlockSpec((s, hd), lambda g, i: (0, kvb + 6 + g), **resident),
            pl.BlockSpec((qrows, 128), lambda g, i: (i, gb + g)),
            pl.BlockSpec((wsl, ncp), lambda g, i: (0, 0)),
            pl.BlockSpec((NSA_HPG * Q_BLOCK, Q_BLOCK), lambda g, i: (0, 0)),
        ],
        out_specs=pl.BlockSpec((qrows, NSA_HPG * hd), lambda g, i: (i, g)),
        out_shape=jax.ShapeDtypeStruct((s, NSA_HEADS * hd), BF16),
        scratch_shapes=[pltpu.VMEM((nq, wsl, Q_BLOCK), F32), pltpu.VMEM((nq, wsl, Q_BLOCK), F32)],
        compiler_params=_cparams(("parallel", "arbitrary")),
        name="nsa_attention",
    )(projb, kcv, kcv, projb, projb, projb, projb, projf, ovt, eye)


def _mlstm_kernel(q_ref, k_ref, v_ref, o0_ref, o1_ref, gc_ref, gr_ref, cw_ref, cb_ref, ng_ref,
                  y_ref, extq, extk, c_st, n_st, m_st, *, cpb):
    step = pl.program_id(0)
    rows = cpb * ML_CHUNK
    half = ML_HEADS * ML_DQK
    L = ML_CHUNK

    @pl.when(step == 0)
    def _():
        extq[0:8, :] = jnp.zeros((8, half), F32)
        extk[0:8, :] = jnp.zeros((8, half), F32)
        c_st[...] = jnp.zeros_like(c_st)
        n_st[...] = jnp.zeros_like(n_st)
        m_st[...] = jnp.zeros_like(m_st)

    extq[8:8 + rows, :] = q_ref[...]
    extk[8:8 + rows, :] = k_ref[...]
    cw = cw_ref[...]
    cb = cb_ref[...]

    def conv(ext, lo):
        acc = cb[:, lo:lo + half]
        for j in range(ML_CONV):
            off = 8 - (ML_CONV - 1) + j
            acc = acc + cw[j:j + 1, lo:lo + half] * ext[off:off + rows, :]
        return _silu(acc)

    qa = conv(extq, 0)
    ka = conv(extk, half) * (ML_DQK ** -0.5)
    extq[0:8, :] = extq[rows:rows + 8, :]
    extk[0:8, :] = extk[rows:rows + 8, :]

    gcol = gc_ref[...]
    grow = gr_ref[...]
    lf_col_all = _log_sigmoid(gcol[:, ML_HEADS:])
    lf_row_all = _log_sigmoid(grow[ML_HEADS:, :])
    ri = lax.broadcasted_iota(jnp.int32, (L, L), 0)
    ci = lax.broadcasted_iota(jnp.int32, (L, L), 1)
    tri = ri >= ci
    ng = ng_ref[...]

    state = [(c_st[h], n_st[h], m_st[h][:, 0:1]) for h in range(ML_HEADS)]
    ones_blk = jnp.ones((L, 128), BF16)
    for c in range(cpb):
        r0 = c * L
        kt_c = ka[r0:r0 + L, :].T
        for h in range(ML_HEADS):
            qh = qa[r0:r0 + L, h * ML_DQK:(h + 1) * ML_DQK]
            kh = ka[r0:r0 + L, h * ML_DQK:(h + 1) * ML_DQK]
            vh = v_ref[r0:r0 + L, h * ML_DV:(h + 1) * ML_DV]
            ig_col = gcol[r0:r0 + L, h:h + 1]
            ig_row = grow[h:h + 1, r0:r0 + L]
            lf_col = lf_col_all[r0:r0 + L, h:h + 1]
            lf_row = lf_row_all[h:h + 1, r0:r0 + L]
            bcum_col = jnp.sum(jnp.where(tri, lf_row, 0.0), axis=1, keepdims=True)
            bcum_row = jnp.sum(jnp.where(ri <= ci, lf_col, 0.0), axis=0, keepdims=True)
            b_last = jnp.sum(lf_row, axis=1, keepdims=True)
            dmat = jnp.where(tri, bcum_col - bcum_row + ig_row, NEG_INF)
            a_loc = jnp.max(dmat, axis=-1, keepdims=True)
            qb = qh.astype(BF16)
            s_loc = _dot_nt(qb, kh.astype(BF16)) * jnp.exp(dmat - a_loc)
            sv = _dot(s_loc.astype(BF16), vh)
            s_sum = _dot(s_loc.astype(BF16), ones_blk)[:, 0:1]
            a_col = b_last - bcum_col + ig_col
            a_row = b_last - bcum_row + ig_row
            a_max = jnp.max(a_row, axis=-1, keepdims=True)
            k_sum = jnp.sum(kh * jnp.exp(a_col - a_max), axis=0, keepdims=True)
            kwt = kt_c[h * ML_DQK:(h + 1) * ML_DQK, :] * jnp.exp(a_row - a_max)
            kv = _dot(kwt.astype(BF16), vh)

            ct, nrow, m_old = state[h]
            inter = bcum_col + m_old
            m_t = jnp.maximum(inter, a_loc)
            w_inter = jnp.exp(inter - m_t)
            w_loc = jnp.exp(a_loc - m_t)
            num = w_inter * _dot(qb, ct.astype(BF16)) + w_loc * sv
            qn = _dot_nt(qb, jnp.broadcast_to(nrow, (8, ML_DQK)).astype(BF16))[:, 0:1]
            den = w_inter * qn + w_loc * s_sum
            hout = num / jnp.maximum(jnp.abs(den), jnp.exp(-m_t))

            m_new = jnp.maximum(b_last + m_old, a_max)
            decay = jnp.exp(b_last + m_old - m_new)
            g_new = jnp.exp(a_max - m_new)
            state[h] = (decay * ct + g_new * kv, decay * nrow + g_new * k_sum, m_new)

            o_ref = o0_ref if h < ML_HEADS // 2 else o1_ref
            oc = (h % (ML_HEADS // 2)) * ML_DV
            og = _sigmoid(o_ref[r0:r0 + L, oc:oc + ML_DV])
            yn = _ln_rows(hout) * ng[:, h * ML_DV:(h + 1) * ML_DV]
            y_ref[r0:r0 + L, h * ML_DV:(h + 1) * ML_DV] = (yn * og).astype(y_ref.dtype)

    for h in range(ML_HEADS):
        c_st[h], n_st[h] = state[h][0], state[h][1]
        m_st[h] = jnp.broadcast_to(state[h][2], (1, 128))


def _mlstm(projb, projf, gates_col, gates_row, conv_w, conv_b, norm_g, cpb):
    s = projb.shape[0]
    rows = cpb * ML_CHUNK
    half = ML_HEADS * ML_DQK
    dv = ML_HEADS * ML_DV
    qblk = _F_MLQK // half
    oblk = _F_MLO // half
    return pl.pallas_call(
        functools.partial(_mlstm_kernel, cpb=cpb),
        grid=(s // rows,),
        in_specs=[
            pl.BlockSpec((rows, half), lambda j: (j, qblk)),
            pl.BlockSpec((rows, half), lambda j: (j, qblk + 1)),
            pl.BlockSpec((rows, dv), lambda j: (j, _B_MLV // dv)),
            pl.BlockSpec((rows, half), lambda j: (j, oblk)),
            pl.BlockSpec((rows, half), lambda j: (j, oblk + 1)),
            pl.BlockSpec((rows, 2 * ML_HEADS), lambda j: (j, 0)),
            pl.BlockSpec((2 * ML_HEADS, rows), lambda j: (0, j)),
            pl.BlockSpec((ML_CONV, 2 * half), lambda j: (0, 0)),
            pl.BlockSpec((1, 2 * half), lambda j: (0, 0)),
            pl.BlockSpec((1, dv), lambda j: (0, 0)),
        ],
        out_specs=pl.BlockSpec((rows, dv), lambda j: (j, 0)),
        out_shape=jax.ShapeDtypeStruct((s, dv), BF16),
        scratch_shapes=[
            pltpu.VMEM((rows + 8, half), F32),
            pltpu.VMEM((rows + 8, half), F32),
            pltpu.VMEM((ML_HEADS, ML_DQK, ML_DV), F32),
            pltpu.VMEM((ML_HEADS, 1, ML_DQK), F32),
            pltpu.VMEM((ML_HEADS, 1, 128), F32),
        ],
        compiler_params=_cparams(("arbitrary",)),
        name="mlstm",
    )(projf, projf, projb, projf, projf, gates_col, gates_row, conv_w, conv_b, norm_g)


def _merge_kernel(yn_ref, ym_ref, g0_ref, g1_ref, x_ref, wn_ref, wm_ref, wo_ref, gate_ref,
                  lg_ref, lb_ref, o_ref):
    a = _dot(yn_ref[...], wn_ref[...])
    b = _dot(ym_ref[...], wm_ref[...])
    merged = _sigmoid(g0_ref[...]) * a + _sigmoid(g1_ref[...]) * b
    y = _dot(merged.astype(BF16), wo_ref[...])
    z = ALPHA * x_ref[...] + gate_ref[...] * y
    o_ref[...] = _ln_rows(z) * lg_ref[...] + lb_ref[...]


def _merge_outproj(y_nsa, y_ml, projf, x, wn, wm, wo, gate, ln_g, ln_b, tm, single_buffer=True):
    s, d = x.shape
    resident = dict(pipeline_mode=pl.Buffered(1)) if single_buffer else {}
    mb = _F_MERGE // d
    row = lambda i: (0, 0)
    return pl.pallas_call(
        _merge_kernel,
        grid=(s // tm,),
        in_specs=[
            pl.BlockSpec((tm, y_nsa.shape[1]), lambda i: (i, 0)),
            pl.BlockSpec((tm, y_ml.shape[1]), lambda i: (i, 0)),
            pl.BlockSpec((tm, d), lambda i: (i, mb)),
            pl.BlockSpec((tm, d), lambda i: (i, mb + 1)),
            pl.BlockSpec((tm, d), lambda i: (i, 0)),
            pl.BlockSpec(wn.shape, row, **resident),
            pl.BlockSpec(wm.shape, row, **resident),
            pl.BlockSpec(wo.shape, row, **resident),
            pl.BlockSpec((1, d), row),
            pl.BlockSpec((1, d), row),
            pl.BlockSpec((1, d), row),
        ],
        out_specs=pl.BlockSpec((tm, d), lambda i: (i, 0)),
        out_shape=jax.ShapeDtypeStruct((s, d), F32),
        compiler_params=_cparams(("parallel",)),
        name="merge_outproj",
    )(y_nsa, y_ml, projf, projf, x, wn, wm, wo, gate, ln_g, ln_b)


def _ffn_kernel(x_ref, xh_ref, sc_ref, sh_ref, gate_ref, wa_ref, wg_ref, cw_ref, cb_ref, wd_ref,
                lg_ref, lb_ref, o_ref, h_scr, a_scr, acc, *, halo):
    i = pl.program_id(0)
    f = pl.program_id(1)
    tm = x_ref.shape[0]

    @pl.when(f == 0)
    def _():
        mod = lambda v: (_ln_rows(v) * (1.0 + sc_ref[...]) + sh_ref[...]).astype(BF16)
        h_scr[0:halo, :] = mod(xh_ref[...])
        h_scr[halo:halo + tm, :] = mod(x_ref[...])
        acc[...] = jnp.zeros_like(acc)

    hx = h_scr[...]
    a_ext = _dot(hx, wa_ref[...])
    rid = lax.broadcasted_iota(jnp.int32, (halo + tm, 1), 0)
    a_scr[...] = jnp.where((rid >= halo) | (i > 0), a_ext, 0.0)
    cw = cw_ref[...]
    conv = cb_ref[...]
    for j in range(FFN_CONV):
        off = halo - (FFN_CONV - 1) + j
        conv = conv + cw[j:j + 1, :] * a_scr[off:off + tm, :]
    g = _dot(h_scr[halo:halo + tm, :], wg_ref[...])
    act = (_silu(conv) * g).astype(BF16)
    acc[...] += _dot(act, wd_ref[...])

    @pl.when(f == pl.num_programs(1) - 1)
    def _():
        z = ALPHA * x_ref[...] + gate_ref[...] * acc[...]
        o_ref[...] = _ln_rows(z) * lg_ref[...] + lb_ref[...]


def _conv_ffn(x, sc, sh, gate, w_up, conv_w, conv_b, w_down, ln_g, ln_b, tm, tf):
    s, d = x.shape
    dff = w_down.shape[0]
    halo = 16
    nf = dff // tf
    row = lambda i, f: (0, 0)
    return pl.pallas_call(
        functools.partial(_ffn_kernel, halo=halo),
        grid=(s // tm, nf),
        in_specs=[
            pl.BlockSpec((tm, d), lambda i, f: (i, 0)),
            pl.BlockSpec((halo, d), lambda i, f: (jnp.maximum(i * (tm // halo) - 1, 0), 0)),
            pl.BlockSpec((1, d), row),
            pl.BlockSpec((1, d), row),
            pl.BlockSpec((1, d), row),
            pl.BlockSpec((d, tf), lambda i, f: (0, f)),
            pl.BlockSpec((d, tf), lambda i, f: (0, nf + f)),
            pl.BlockSpec((FFN_CONV, tf), lambda i, f: (0, f)),
            pl.BlockSpec((1, tf), lambda i, f: (0, f)),
            pl.BlockSpec((tf, d), lambda i, f: (f, 0)),
            pl.BlockSpec((1, d), row),
            pl.BlockSpec((1, d), row),
        ],
        out_specs=pl.BlockSpec((tm, d), lambda i, f: (i, 0)),
        out_shape=jax.ShapeDtypeStruct((s, d), F32),
        scratch_shapes=[
            pltpu.VMEM((halo + tm, d), BF16),
            pltpu.VMEM((halo + tm, tf), F32),
            pltpu.VMEM((tm, d), F32),
        ],
        compiler_params=_cparams(("parallel", "arbitrary")),
        name="conv_ffn",
    )(x, x, sc, sh, gate, w_up, w_up, conv_w, conv_b, w_down, ln_g, ln_b)


def _token_mixer(x, sc, sh, gate, w_in, cmp_pe, cmp_w1, cmp_w2, ml_conv_w, ml_conv_b, ml_gate_b,
                 ml_norm_g, w_br_nsa, w_br_ml, w_o, ln_g, ln_b):
    s, d = x.shape
    projb, projf = _inproj(x, sc, sh, _arrange_w_in(w_in), min(1024, s), 1024)
    kcv = _compress(projf, cmp_pe, cmp_w1, cmp_w2)
    y_nsa = _nsa_attention(projb, projf, kcv)
    ifo = _F_SMALL + 2 * 128
    gates_col = projf[:, ifo:ifo + _ML_IF] + ml_gate_b[None, :]
    y_ml = _mlstm(projb, projf, gates_col, gates_col.T, ml_conv_w, ml_conv_b[None, :],
                  ml_norm_g[None, :], cpb=min(4, s // ML_CHUNK))
    return _merge_outproj(y_nsa, y_ml, projf, x, w_br_nsa.astype(BF16), w_br_ml.astype(BF16),
                          w_o.astype(BF16), gate, ln_g, ln_b, tm=min(256, s))


def _forward(x, c, w_ada, b_ada, w_in, cmp_pe, cmp_w1, cmp_w2, ml_conv_w, ml_conv_b, ml_gate_b,
             ml_norm_g, w_br_nsa, w_br_ml, w_o, w_up, ffn_conv_w, ffn_conv_b, w_down, ln_g, ln_b):
    b, s, d = x.shape
    assert b == 1 and d == D_MODEL
    depth = w_ada.shape[0]
    mod = _modulation(c, w_ada, b_ada)
    xs = x[0]
    for l in range(depth):
        sh1, sc1, g1, sh2, sc2, g2 = [mod[l, :, k * d:(k + 1) * d] for k in range(6)]
        xs = _token_mixer(xs, sc1, sh1, g1, w_in[l], cmp_pe[l], cmp_w1[l], cmp_w2[l], ml_conv_w[l],
                          ml_conv_b[l], ml_gate_b[l], ml_norm_g[l], w_br_nsa[l], w_br_ml[l], w_o[l],
                          ln_g[l, 0][None, :], ln_b[l, 0][None, :])
        xs = _conv_ffn(xs, sc2, sh2, g2, w_up[l].astype(BF16), ffn_conv_w[l], ffn_conv_b[l][None, :],
                       w_down[l].astype(BF16), ln_g[l, 1][None, :], ln_b[l, 1][None, :],
                       tm=min(512, s), tf=512)
    return xs[None]


def kernel(x, c, w_ada, b_ada, w_in, cmp_pe, cmp_w1, cmp_w2, ml_conv_w, ml_conv_b, ml_gate_b, ml_norm_g, w_br_nsa, w_br_ml, w_o, w_up, ffn_conv_w, ffn_conv_b, w_down, ln_g, ln_b):
    return _forward(x, c, w_ada, b_ada, w_in, cmp_pe, cmp_w1, cmp_w2, ml_conv_w, ml_conv_b,
                    ml_gate_b, ml_norm_g, w_br_nsa, w_br_ml, w_o, w_up, ffn_conv_w, ffn_conv_b,
                    w_down, ln_g, ln_b)
```

```python
import functools
import math

import jax
import jax.numpy as jnp
from jax import lax
from jax.experimental import pallas as pl
from jax.experimental.pallas import tpu as pltpu

F32 = jnp.float32
BF16 = jnp.bfloat16

D_MODEL = 2048
DEPTH = 2
NSA_HEADS = 8
NSA_GROUPS = 2
NSA_HPG = NSA_HEADS // NSA_GROUPS
NSA_HD = 128
CMP_LEN = 32
CMP_STRIDE = 16
CMP_HID = 256
SLC_LEN = 64
SLC_TOPK = 16
WINDOW = 512
Q_BLOCK = 128
ML_HEADS = 4
ML_DQK = 128
ML_DV = 256
ML_CHUNK = 64
ML_CONV = 4
D_FF = 5632
FFN_CONV = 3
ALPHA = (2 * DEPTH) ** 0.25
LN_EPS = 1e-5
NEG_INF = -1e30

V7X_VMEM_BYTES = 64 * 1024 * 1024
VMEM_LIMIT = V7X_VMEM_BYTES - 8 * 1024 * 1024

INPROJ_TM, INPROJ_TN = 1024, 1024
MERGE_TM = 256
FFN_TM, FFN_TF = 512, 512
MLSTM_CHUNKS_PER_STEP = 4

_NSA_Q = NSA_HEADS * NSA_HD
_NSA_KV = 3 * 2 * NSA_GROUPS * NSA_HD
_NSA_G = 3 * NSA_HEADS
_ML_QK = 2 * ML_HEADS * ML_DQK
_ML_V = ML_HEADS * ML_DV
_ML_IF = 2 * ML_HEADS
_CMP_COLS = 2 * NSA_GROUPS * NSA_HD
_F_MERGE = 0
_F_CMP = 2 * D_MODEL
_F_MLQK = _F_CMP + _CMP_COLS
_F_MLO = _F_MLQK + _ML_QK
_F_SMALL = _F_MLO + _ML_V
_F_COLS = _F_SMALL + 4 * 128
_B_Q = 0
_B_KV = _NSA_Q
_B_MLV = _B_KV + 8 * NSA_HD
_B_COLS = _B_MLV + _ML_V

SEL_TILE = 512
SEL_STEP = 4096
SEL_BIG = 2.0 ** 60
NSA_Q_BLOCKS_PER_STEP = 2
SEL_EXP2_GUARD = 64.0


def _cparams(sem):
    return pltpu.CompilerParams(dimension_semantics=sem, vmem_limit_bytes=VMEM_LIMIT)


def _ln_rows(x):
    mu = jnp.mean(x, axis=-1, keepdims=True)
    xc = x - mu
    var = jnp.mean(xc * xc, axis=-1, keepdims=True)
    return xc * lax.rsqrt(var + LN_EPS)


def _sigmoid(x):
    return 1.0 / (1.0 + jnp.exp(-x))


def _silu(x):
    return x * _sigmoid(x)


def _gelu_tanh(x):
    c = math.sqrt(2.0 / math.pi)
    return x * (0.5 * (1.0 + jnp.tanh(c * (x + 0.044715 * (x * x * x)))))


def _log_sigmoid(x):
    return jnp.minimum(x, 0.0) - jnp.log1p(jnp.exp(-jnp.abs(x)))


def _dot(a, b):
    return jnp.dot(a, b, preferred_element_type=F32)


def _dot_nt(a, b):
    return lax.dot_general(a, b, (((1,), (1,)), ((), ())), preferred_element_type=F32)


def _masked_softmax(raw, mask, scale):
    raw = jnp.where(mask, raw, NEG_INF)
    m = jnp.max(raw, axis=-1, keepdims=True)
    e = jnp.exp2((raw - m) * (scale * math.log2(math.e)))
    den = jnp.sum(e, axis=-1, keepdims=True)
    return e * jnp.where(m > 0.5 * NEG_INF, 1.0 / den, 0.0)


def _mod_kernel(c_ref, w_ref, b_ref, o_ref):
    ca = _silu(c_ref[...])
    o = _dot(ca.astype(BF16), w_ref[0].astype(BF16))
    o_ref[0] = o[0:1] + b_ref[0]


def _modulation(c, w_ada, b_ada):
    depth, d, n = w_ada.shape
    tn = INPROJ_TN
    c8 = jnp.broadcast_to(c, (8, d))
    return pl.pallas_call(
        _mod_kernel,
        grid=(depth, n // tn),
        in_specs=[
            pl.BlockSpec((8, d), lambda l, j: (0, 0)),
            pl.BlockSpec((1, d, tn), lambda l, j: (l, 0, j)),
            pl.BlockSpec((1, 1, tn), lambda l, j: (l, 0, j)),
        ],
        out_specs=pl.BlockSpec((1, 1, tn), lambda l, j: (l, 0, j)),
        out_shape=jax.ShapeDtypeStruct((depth, 1, n), F32),
        compiler_params=_cparams(("parallel", "parallel")),
        name="adaln_mod",
    )(c8, w_ada, b_ada.reshape(depth, 1, n))


def _inproj_kernel(x_ref, sc_ref, sh_ref, wb_ref, wf_ref, ob_ref, of_ref, h_scr, *, nb):
    j = pl.program_id(1)

    @pl.when(j == 0)
    def _():
        h = _ln_rows(x_ref[...]) * (1.0 + sc_ref[...]) + sh_ref[...]
        h_scr[...] = h.astype(BF16)

    @pl.when(j < nb)
    def _():
        ob_ref[...] = _dot(h_scr[...], wb_ref[...]).astype(ob_ref.dtype)

    @pl.when(j >= nb)
    def _():
        of_ref[...] = _dot(h_scr[...], wf_ref[...])


def _inproj(x, sc, sh, wb, wf, tm, tn):
    s, d = x.shape
    nb, nf = wb.shape[1] // tn, wf.shape[1] // tn
    bcol = lambda i, j: (i, jnp.minimum(j, nb - 1))
    fcol = lambda i, j: (i, jnp.maximum(j - nb, 0))
    return pl.pallas_call(
        functools.partial(_inproj_kernel, nb=nb),
        grid=(s // tm, nb + nf),
        in_specs=[
            pl.BlockSpec((tm, d), lambda i, j: (i, 0)),
            pl.BlockSpec((1, d), lambda i, j: (0, 0)),
            pl.BlockSpec((1, d), lambda i, j: (0, 0)),
            pl.BlockSpec((d, tn), lambda i, j: (0, jnp.minimum(j, nb - 1))),
            pl.BlockSpec((d, tn), lambda i, j: (0, jnp.maximum(j - nb, 0))),
        ],
        out_specs=[pl.BlockSpec((tm, tn), bcol), pl.BlockSpec((tm, tn), fcol)],
        out_shape=[jax.ShapeDtypeStruct((s, wb.shape[1]), BF16), jax.ShapeDtypeStruct((s, wf.shape[1]), F32)],
        scratch_shapes=[pltpu.VMEM((tm, d), BF16)],
        compiler_params=_cparams(("parallel", "arbitrary")),
        name="in_proj",
    )(x, sc, sh, wb, wf)


def _split_w_in(w):
    d = w.shape[0]
    o = 0
    q = w[:, o:o + _NSA_Q]; o += _NSA_Q
    kv = w[:, o:o + _NSA_KV]; o += _NSA_KV
    g = w[:, o:o + _NSA_G]; o += _NSA_G
    mlqk = w[:, o:o + _ML_QK]; o += _ML_QK
    mlv = w[:, o:o + _ML_V]; o += _ML_V
    mlif = w[:, o:o + _ML_IF]; o += _ML_IF
    mlo = w[:, o:o + _ML_V]; o += _ML_V
    merge = w[:, o:o + 2 * D_MODEL]
    per_g = 3 * NSA_HPG
    z = lambda n: jnp.zeros((d, n), w.dtype)
    wb = jnp.concatenate([q, kv[:, _CMP_COLS:], mlv], axis=1)
    wf = jnp.concatenate(
        [merge, kv[:, :_CMP_COLS], mlqk, mlo,
         g[:, :per_g], z(128 - per_g), g[:, per_g:], z(128 - per_g), mlif, z(128 - _ML_IF), z(128)],
        axis=1)
    return wb.astype(BF16), wf.astype(BF16)


def _compress_kernel(a_ref, pe_ref, w1_ref, w2_ref, o_ref):
    n = o_ref.shape[1]
    hd = NSA_HD

    def half_sum(lo):
        acc = jnp.zeros((n, CMP_HID), F32)
        for l in range(0, CMP_STRIDE, 2):
            x = jnp.concatenate(
                [a_ref[pl.ds(l + u, n, stride=CMP_STRIDE), :] + pe_ref[0, lo + l + u:lo + l + u + 1, :]
                 for u in range(2)], axis=1)
            w = w1_ref[0, (lo + l) * hd:(lo + l + 2) * hd, :]
            acc = acc + _dot(x.astype(BF16), w.astype(BF16))
        return acc

    pre = half_sum(0) + pltpu.roll(half_sum(CMP_STRIDE), n - 1, 0)
    g = _gelu_tanh(pre)
    o_ref[0] = _dot(g.astype(BF16), w2_ref[0].astype(BF16)).astype(o_ref.dtype)


def _compress(projf, pe, w1, w2):
    s = projf.shape[0]
    n = s // CMP_STRIDE
    four = 2 * NSA_GROUPS
    return pl.pallas_call(
        _compress_kernel,
        grid=(four,),
        in_specs=[
            pl.BlockSpec((s, NSA_HD), lambda j: (0, _F_CMP // NSA_HD + j)),
            pl.BlockSpec((1, CMP_LEN, NSA_HD), lambda j: (j // NSA_GROUPS, 0, 0)),
            pl.BlockSpec((1, CMP_LEN * NSA_HD, CMP_HID), lambda j: (j // NSA_GROUPS, 0, 0)),
            pl.BlockSpec((1, CMP_HID, NSA_HD), lambda j: (j // NSA_GROUPS, 0, 0)),
        ],
        out_specs=pl.BlockSpec((1, n, NSA_HD), lambda j: (j, 0, 0)),
        out_shape=jax.ShapeDtypeStruct((four, n, NSA_HD), BF16),
        compiler_params=_cparams(("parallel",)),
        name="nsa_compress",
    )(projf, pe, w1, w2)


def _nsa_kernel(q_ref, kc_ref, vc_ref, ks_ref, vs_ref, kw_ref, vw_ref, gate_ref, ovt_ref, eye_ref,
                o_ref, nsel_scr, npast_scr, *, step, nq):
    pid = pl.program_id(1)
    rows = NSA_HPG * Q_BLOCK
    scale = NSA_HD ** -0.5
    c2 = scale * math.log2(math.e)
    row_q = lax.broadcasted_iota(jnp.int32, (rows, 1), 0) & (Q_BLOCK - 1)
    lane_q = lax.broadcasted_iota(jnp.int32, (1, Q_BLOCK), 1)
    blk = []
    for u in range(nq):
        i = pid * nq + u
        qt = q_ref[u * Q_BLOCK:(u + 1) * Q_BLOCK, :]
        qs = jnp.concatenate([qt[:, h * NSA_HD:(h + 1) * NSA_HD] for h in range(NSA_HPG)], axis=0)
        blk.append((i, qs, i * Q_BLOCK + row_q))
    i_last = blk[-1][0]

    ncp = kc_ref.shape[1]

    def cmp_branch(width):
        outs = []
        cidx = lax.broadcasted_iota(jnp.int32, (rows, width), 1)
        ovt = ovt_ref[:, :width]
        for _, qs, t in blk:
            last_c = (t - (CMP_LEN - 1)) // CMP_STRIDE
            p_c = _masked_softmax(_dot_nt(qs, kc_ref[0, :width, :]), cidx <= last_c, scale)
            o = _dot(p_c.astype(BF16), vc_ref[0, :width, :])
            psum = (p_c[0:Q_BLOCK] + p_c[Q_BLOCK:2 * Q_BLOCK]
                    + p_c[2 * Q_BLOCK:3 * Q_BLOCK] + p_c[3 * Q_BLOCK:4 * Q_BLOCK])
            outs += [o, _dot_nt(ovt, psum.astype(BF16))]
        return tuple(outs)

    n_vis = (i_last * Q_BLOCK + Q_BLOCK - CMP_LEN) // CMP_STRIDE + 1
    branch = lambda: cmp_branch(ncp)
    for width in (ncp // 2, ncp // 4):
        if width % 128 == 0:
            branch = (lambda w, other: lambda: lax.cond(n_vis <= w, lambda: cmp_branch(w), other))(width, branch)
    cmp_out = branch()
    o_cs, imps = cmp_out[0::2], cmp_out[1::2]

    wsl = imps[0].shape[0]
    bid = lax.broadcasted_iota(jnp.int32, (wsl, Q_BLOCK), 0)
    bidf = bid.astype(F32)
    taken = -1e9
    nsel0 = jnp.full((wsl, Q_BLOCK), -SEL_BIG, F32)
    score0s = []
    for (i, _, _), imp in zip(blk, imps):
        t_row = i * Q_BLOCK + lane_q
        cur = t_row // SLC_LEN
        score0s.append(jnp.where(bid == 0, 3e6, jnp.where(bid == cur, 2e6, jnp.where(
            bid == cur - 1, 1e6, jnp.where(bid * SLC_LEN <= t_row, imp, -1.0 - bidf)))))

    def take_max(c, lanes=None):
        sc, ns = c
        hit = sc == jnp.max(sc, axis=0, keepdims=True)
        if lanes is not None:
            hit = hit & lanes
        return jnp.where(hit, taken, sc), jnp.where(hit, 0.0, ns)

    cs = [(jnp.where(s0 >= 1e6, taken, s0), jnp.where(s0 >= 1e6, 0.0, nsel0)) for s0 in score0s]
    n_min = SLC_TOPK - 3
    for _ in range(n_min):
        cs = [take_max(c) for c in cs]

    def early_rounds():
        cur = lane_q // SLC_LEN
        n_free = SLC_TOPK - (1 + jnp.where(cur >= 1, 1, 0) + jnp.where(cur >= 2, 1, 0))
        e = cs[0]
        for r in range(n_min + 1, SLC_TOPK):
            e = take_max(e, n_free >= r)
        return e

    cs[0] = lax.cond(pid == 0, early_rounds, lambda: cs[0])
    nsels = tuple(c[1] for c in cs)

    def topk_ties():
        def one(_, c):
            sc, ns = c
            mx = jnp.max(sc, axis=0, keepdims=True)
            first = jnp.min(jnp.where(sc == mx, bidf, float(wsl)), axis=0, keepdims=True)
            hit = bidf == first
            return jnp.where(hit, taken, sc), jnp.where(hit, 0.0, ns)
        return tuple(lax.fori_loop(0, SLC_TOPK, one, (s0, nsel0))[1] for s0 in score0s)

    n_taken = functools.reduce(jnp.maximum, [jnp.sum(jnp.where(ns == 0.0, 1.0, 0.0), axis=0, keepdims=True)
                                             for ns in nsels])
    nsels = lax.cond(jnp.max(n_taken) > SLC_TOPK, topk_ties, lambda: nsels)
    for u, ((i, _, _), ns) in enumerate(zip(blk, nsels)):
        blk_d = (i * Q_BLOCK) // SLC_LEN
        nsel_scr[u] = ns
        npast_scr[u] = jnp.where(bid < blk_d, ns, -SEL_BIG)

    def keys_aug(k0, width, sel_scr):
        b0 = k0 // SLC_LEN
        bias = jnp.concatenate(
            [jnp.broadcast_to(sel_scr[pl.ds(b0 + b, 1), :], (SLC_LEN, Q_BLOCK))
             for b in range(width // SLC_LEN)], axis=0)
        return jnp.concatenate([ks_ref[pl.ds(k0, width), :], bias.astype(BF16)], axis=1)

    def vals_aug(k0, width):
        ones_col = (lax.broadcasted_iota(jnp.int32, (width, NSA_HD), 1) == 0).astype(BF16)
        return jnp.concatenate([vs_ref[pl.ds(k0, width), :], ones_col], axis=1)

    qas, diag, o_ws = [], [], []
    wlen = WINDOW + Q_BLOCK
    for u, (i, qs, t) in enumerate(blk):
        qa = jnp.concatenate([qs, eye_ref[...]], axis=1)
        k_d = pl.multiple_of(i * Q_BLOCK, Q_BLOCK)
        s_d = _dot_nt(qa, keys_aug(k_d, Q_BLOCK, nsel_scr.at[u]))
        s_d = jnp.where(k_d + lane_q <= t, s_d, NEG_INF)
        m_d = jnp.max(s_d, axis=-1, keepdims=True)
        acc_d = _dot(jnp.exp2((s_d - m_d) * c2).astype(BF16), vals_aug(k_d, Q_BLOCK))
        qas.append(qa)
        diag.append((m_d, acc_d))
        ks0 = pl.multiple_of(jnp.maximum(i - WINDOW // Q_BLOCK, 0) * Q_BLOCK, Q_BLOCK)
        s_w = _dot_nt(qs, kw_ref[pl.ds(ks0, wlen), :])
        kpos = ks0 + lax.broadcasted_iota(jnp.int32, (1, wlen), 1)
        p_w = _masked_softmax(s_w, (kpos <= t) & (kpos > t - WINDOW), scale)
        o_ws.append(_dot(p_w.astype(BF16), vw_ref[pl.ds(ks0, wlen), :]))

    gates = _sigmoid(gate_ref[...])
    for u, (i, qs, t) in enumerate(blk):
        qa = qas[u]
        past_scr = npast_scr.at[u]
        m_d, acc_d = diag[u]

        def exact_step(k0, carry):
            m, acc = carry
            k0 = pl.multiple_of(k0, SEL_TILE)
            s = _dot_nt(qa, keys_aug(k0, SEL_TILE, past_scr))
            m_new = jnp.maximum(m, jnp.max(s, axis=-1, keepdims=True))
            p = jnp.exp2((s - m_new) * c2)
            return m_new, jnp.exp2((m - m_new) * c2) * acc + _dot(p.astype(BF16), vals_aug(k0, SEL_TILE))

        def fast_step(k0, width, carry):
            m, acc, risk = carry
            k0 = pl.multiple_of(k0, SEL_TILE)
            s = _dot_nt(qa, keys_aug(k0, width, past_scr))
            p = jnp.exp2((s - m) * c2)
            mx = jnp.max(s, axis=-1, keepdims=True)
            m_new = jnp.maximum(m, mx)
            acc = (acc + _dot(p.astype(BF16), vals_aug(k0, width))) * jnp.exp2((m - m_new) * c2)
            return m_new, acc, jnp.maximum(risk, mx - m)

        n_past = i * Q_BLOCK
        n_steps = n_past // step
        carry = (m_d, acc_d, jnp.zeros((rows, 1), F32))
        carry = lax.fori_loop(0, n_steps, lambda j, c: fast_step(j * step, step, c), carry)
        rem = n_past - n_steps * step
        k_r = n_steps * step
        carry = lax.cond(
            rem > step // 2, lambda: fast_step(k_r, step, carry),
            lambda: lax.cond(rem > step // 4, lambda: fast_step(k_r, step // 2, carry),
                             lambda: lax.cond(rem > 0, lambda: fast_step(k_r, step // 4, carry),
                                              lambda: carry)))
        _, acc_s, risk = carry

        def redo_exact():
            n_tiles = (n_past + SEL_TILE - 1) // SEL_TILE
            return lax.fori_loop(0, n_tiles, lambda j, c: exact_step(j * SEL_TILE, c), (m_d, acc_d))[1]

        acc_s = lax.cond(jnp.max(risk) * c2 > SEL_EXP2_GUARD, redo_exact, lambda: acc_s)
        o_s = acc_s[:, :NSA_HD] / acc_s[:, NSA_HD:NSA_HD + 1]

        g_u = gates[u * Q_BLOCK:(u + 1) * Q_BLOCK]
        for h in range(NSA_HPG):
            r = slice(h * Q_BLOCK, (h + 1) * Q_BLOCK)
            out = (g_u[:, 3 * h:3 * h + 1] * o_cs[u][r] + g_u[:, 3 * h + 1:3 * h + 2] * o_s[r]
                   + g_u[:, 3 * h + 2:3 * h + 3] * o_ws[u][r])
            o_ref[u * Q_BLOCK:(u + 1) * Q_BLOCK, h * NSA_HD:(h + 1) * NSA_HD] = out.astype(o_ref.dtype)


def _nsa_attention(projb, projf, kcv, single_buffer=True):
    s = projb.shape[0]
    n_slc = s // SLC_LEN
    step = min(SEL_STEP, s)
    nq = NSA_Q_BLOCKS_PER_STEP
    qrows = nq * Q_BLOCK
    assert n_slc >= SLC_TOPK and s >= WINDOW + Q_BLOCK and s % step == 0 and step % (4 * SEL_TILE) == 0
    ncp = kcv.shape[1]
    wsl = max(n_slc, 128)
    hd = NSA_HD
    c_start = jnp.arange(ncp)[None, :] * CMP_STRIDE
    s_start = jnp.arange(wsl)[:, None] * SLC_LEN
    ovt = ((c_start < s_start + SLC_LEN) & (c_start + CMP_LEN > s_start)
           & (jnp.arange(wsl)[:, None] < n_slc)).astype(BF16)
    eye = jnp.tile(jnp.eye(Q_BLOCK, dtype=BF16), (NSA_HPG, 1))
    resident = dict(pipeline_mode=pl.Buffered(1)) if single_buffer else {}
    kvb = _B_KV // hd
    gb = _F_SMALL // 128
    return pl.pallas_call(
        functools.partial(_nsa_kernel, step=step, nq=nq),
        grid=(NSA_GROUPS, s // qrows),
        in_specs=[
            pl.BlockSpec((qrows, NSA_HPG * hd), lambda g, i: (i, g)),
            pl.BlockSpec((1, ncp, hd), lambda g, i: (g, 0, 0)),
            pl.BlockSpec((1, ncp, hd), lambda g, i: (NSA_GROUPS + g, 0, 0)),
            pl.BlockSpec((s, hd), lambda g, i: (0, kvb + g), **resident),
            pl.BlockSpec((s, hd), lambda g, i: (0, kvb + 2 + g), **resident),
            pl.BlockSpec((s, hd), lambda g, i: (0, kvb + 4 + g), **resident),
            pl.BlockSpec((s, hd), lambda g, i: (0, kvb + 6 + g), **resident),
            pl.BlockSpec((qrows, 128), lambda g, i: (i, gb + g)),
            pl.BlockSpec((wsl, ncp), lambda g, i: (0, 0)),
            pl.BlockSpec((NSA_HPG * Q_BLOCK, Q_BLOCK), lambda g, i: (0, 0)),
        ],
        out_specs=pl.BlockSpec((qrows, NSA_HPG * hd), lambda g, i: (i, g)),
        out_shape=jax.ShapeDtypeStruct((s, NSA_HEADS * hd), BF16),
        scratch_shapes=[pltpu.VMEM((nq, wsl, Q_BLOCK), F32), pltpu.VMEM((nq, wsl, Q_BLOCK), F32)],
        compiler_params=_cparams(("parallel", "arbitrary")),
        name="nsa_attention",
    )(projb, kcv, kcv, projb, projb, projb, projb, projf, ovt, eye)


def _mlstm_kernel(q_ref, k_ref, v_ref, o0_ref, o1_ref, gc_ref, gr_ref, cw_ref, cb_ref, ng_ref,
                  y_ref, extq, extk, c_st, n_st, m_st, *, cpb):
    step = pl.program_id(0)
    rows = cpb * ML_CHUNK
    half = ML_HEADS * ML_DQK
    L = ML_CHUNK

    @pl.when(step == 0)
    def _():
        extq[0:8, :] = jnp.zeros((8, half), F32)
        extk[0:8, :] = jnp.zeros((8, half), F32)
        c_st[...] = jnp.zeros_like(c_st)
        n_st[...] = jnp.zeros_like(n_st)
        m_st[...] = jnp.zeros_like(m_st)

    extq[8:8 + rows, :] = q_ref[...]
    extk[8:8 + rows, :] = k_ref[...]
    cw = cw_ref[...]
    cb = cb_ref[...]

    def conv(ext, lo):
        acc = cb[:, lo:lo + half]
        for j in range(ML_CONV):
            off = 8 - (ML_CONV - 1) + j
            acc = acc + cw[j:j + 1, lo:lo + half] * ext[off:off + rows, :]
        return _silu(acc)

    qa = conv(extq, 0)
    ka = conv(extk, half) * (ML_DQK ** -0.5)
    extq[0:8, :] = extq[rows:rows + 8, :]
    extk[0:8, :] = extk[rows:rows + 8, :]

    gcol = gc_ref[...]
    grow = gr_ref[...]
    lf_col_all = _log_sigmoid(gcol[:, ML_HEADS:])
    lf_row_all = _log_sigmoid(grow[ML_HEADS:, :])
    ri = lax.broadcasted_iota(jnp.int32, (L, L), 0)
    ci = lax.broadcasted_iota(jnp.int32, (L, L), 1)
    tri = ri >= ci
    ng = ng_ref[...]

    state = [(c_st[h], n_st[h], m_st[h][:, 0:1]) for h in range(ML_HEADS)]
    ones_blk = jnp.ones((L, 128), BF16)
    for c in range(cpb):
        r0 = c * L
        kt_c = ka[r0:r0 + L, :].T
        for h in range(ML_HEADS):
            qh = qa[r0:r0 + L, h * ML_DQK:(h + 1) * ML_DQK]
            kh = ka[r0:r0 + L, h * ML_DQK:(h + 1) * ML_DQK]
            vh = v_ref[r0:r0 + L, h * ML_DV:(h + 1) * ML_DV]
            ig_col = gcol[r0:r0 + L, h:h + 1]
            ig_row = grow[h:h + 1, r0:r0 + L]
            lf_col = lf_col_all[r0:r0 + L, h:h + 1]
            lf_row = lf_row_all[h:h + 1, r0:r0 + L]
            bcum_col = jnp.sum(jnp.where(tri, lf_row, 0.0), axis=1, keepdims=True)
            bcum_row = jnp.sum(jnp.where(ri <= ci, lf_col, 0.0), axis=0, keepdims=True)
            b_last = jnp.sum(lf_row, axis=1, keepdims=True)
            dmat = jnp.where(tri, bcum_col - bcum_row + ig_row, NEG_INF)
            a_loc = jnp.max(dmat, axis=-1, keepdims=True)
            qb = qh.astype(BF16)
            s_loc = _dot_nt(qb, kh.astype(BF16)) * jnp.exp(dmat - a_loc)
            sv = _dot(s_loc.astype(BF16), vh)
            s_sum = _dot(s_loc.astype(BF16), ones_blk)[:, 0:1]
            a_col = b_last - bcum_col + ig_col
            a_row = b_last - bcum_row + ig_row
            a_max = jnp.max(a_row, axis=-1, keepdims=True)
            k_sum = jnp.sum(kh * jnp.exp(a_col - a_max), axis=0, keepdims=True)
            kwt = kt_c[h * ML_DQK:(h + 1) * ML_DQK, :] * jnp.exp(a_row - a_max)
            kv = _dot(kwt.astype(BF16), vh)

            ct, nrow, m_old = state[h]
            inter = bcum_col + m_old
            m_t = jnp.maximum(inter, a_loc)
            w_inter = jnp.exp(inter - m_t)
            w_loc = jnp.exp(a_loc - m_t)
            num = w_inter * _dot(qb, ct.astype(BF16)) + w_loc * sv
            qn = _dot_nt(qb, jnp.broadcast_to(nrow, (8, ML_DQK)).astype(BF16))[:, 0:1]
            den = w_inter * qn + w_loc * s_sum
            hout = num / jnp.maximum(jnp.abs(den), jnp.exp(-m_t))

            m_new = jnp.maximum(b_last + m_old, a_max)
            decay = jnp.exp(b_last + m_old - m_new)
            g_new = jnp.exp(a_max - m_new)
            state[h] = (decay * ct + g_new * kv, decay * nrow + g_new * k_sum, m_new)

            o_ref = o0_ref if h < ML_HEADS // 2 else o1_ref
            oc = (h % (ML_HEADS // 2)) * ML_DV
            og = _sigmoid(o_ref[r0:r0 + L, oc:oc + ML_DV])
            yn = _ln_rows(hout) * ng[:, h * ML_DV:(h + 1) * ML_DV]
            y_ref[r0:r0 + L, h * ML_DV:(h + 1) * ML_DV] = (yn * og).astype(y_ref.dtype)

    for h in range(ML_HEADS):
        c_st[h], n_st[h] = state[h][0], state[h][1]
        m_st[h] = jnp.broadcast_to(state[h][2], (1, 128))


def _mlstm(projb, projf, gates_col, gates_row, conv_w, conv_b, norm_g, cpb):
    s = projb.shape[0]
    rows = cpb * ML_CHUNK
    half = ML_HEADS * ML_DQK
    dv = ML_HEADS * ML_DV
    qblk = _F_MLQK // half
    oblk = _F_MLO // half
    return pl.pallas_call(
        functools.partial(_mlstm_kernel, cpb=cpb),
        grid=(s // rows,),
        in_specs=[
            pl.BlockSpec((rows, half), lambda j: (j, qblk)),
            pl.BlockSpec((rows, half), lambda j: (j, qblk + 1)),
            pl.BlockSpec((rows, dv), lambda j: (j, _B_MLV // dv)),
            pl.BlockSpec((rows, half), lambda j: (j, oblk)),
            pl.BlockSpec((rows, half), lambda j: (j, oblk + 1)),
            pl.BlockSpec((rows, 2 * ML_HEADS), lambda j: (j, 0)),
            pl.BlockSpec((2 * ML_HEADS, rows), lambda j: (0, j)),
            pl.BlockSpec((ML_CONV, 2 * half), lambda j: (0, 0)),
            pl.BlockSpec((1, 2 * half), lambda j: (0, 0)),
            pl.BlockSpec((1, dv), lambda j: (0, 0)),
        ],
        out_specs=pl.BlockSpec((rows, dv), lambda j: (j, 0)),
        out_shape=jax.ShapeDtypeStruct((s, dv), BF16),
        scratch_shapes=[
            pltpu.VMEM((rows + 8, half), F32),
            pltpu.VMEM((rows + 8, half), F32),
            pltpu.VMEM((ML_HEADS, ML_DQK, ML_DV), F32),
            pltpu.VMEM((ML_HEADS, 1, ML_DQK), F32),
            pltpu.VMEM((ML_HEADS, 1, 128), F32),
        ],
        compiler_params=_cparams(("arbitrary",)),
        name="mlstm",
    )(projf, projf, projb, projf, projf, gates_col, gates_row, conv_w, conv_b, norm_g)


def _merge_kernel(yn_ref, ym_ref, g0_ref, g1_ref, x_ref, wn_ref, wm_ref, wo_ref, gate_ref,
                  lg_ref, lb_ref, o_ref):
    a = _dot(yn_ref[...], wn_ref[...])
    b = _dot(ym_ref[...], wm_ref[...])
    merged = _sigmoid(g0_ref[...]) * a + _sigmoid(g1_ref[...]) * b
    y = _dot(merged.astype(BF16), wo_ref[...])
    z = ALPHA * x_ref[...] + gate_ref[...] * y
    o_ref[...] = _ln_rows(z) * lg_ref[...] + lb_ref[...]


def _merge_outproj(y_nsa, y_ml, projf, x, wn, wm, wo, gate, ln_g, ln_b, tm, single_buffer=True):
    s, d = x.shape
    resident = dict(pipeline_mode=pl.Buffered(1)) if single_buffer else {}
    mb = _F_MERGE // d
    row = lambda i: (0, 0)
    return pl.pallas_call(
        _merge_kernel,
        grid=(s // tm,),
        in_specs=[
            pl.BlockSpec((tm, y_nsa.shape[1]), lambda i: (i, 0)),
            pl.BlockSpec((tm, y_ml.shape[1]), lambda i: (i, 0)),
            pl.BlockSpec((tm, d), lambda i: (i, mb)),
            pl.BlockSpec((tm, d), lambda i: (i, mb + 1)),
            pl.BlockSpec((tm, d), lambda i: (i, 0)),
            pl.BlockSpec(wn.shape, row, **resident),
            pl.BlockSpec(wm.shape, row, **resident),
            pl.BlockSpec(wo.shape, row, **resident),
            pl.BlockSpec((1, d), row),
            pl.BlockSpec((1, d), row),
            pl.BlockSpec((1, d), row),
        ],
        out_specs=pl.BlockSpec((tm, d), lambda i: (i, 0)),
        out_shape=jax.ShapeDtypeStruct((s, d), F32),
        compiler_params=_cparams(("parallel",)),
        name="merge_outproj",
    )(y_nsa, y_ml, projf, projf, x, wn, wm, wo, gate, ln_g, ln_b)


def _ffn_kernel(x_ref, xh_ref, sc_ref, sh_ref, gate_ref, wa_ref, wg_ref, cw_ref, cb_ref, wd_ref,
                lg_ref, lb_ref, o_ref, h_scr, a_scr, acc, *, halo):
    i = pl.program_id(0)
    f = pl.program_id(1)
    tm = x_ref.shape[0]

    @pl.when(f == 0)
    def _():
        mod = lambda v: (_ln_rows(v) * (1.0 + sc_ref[...]) + sh_ref[...]).astype(BF16)
        h_scr[0:halo, :] = mod(xh_ref[...])
        h_scr[halo:halo + tm, :] = mod(x_ref[...])
        acc[...] = jnp.zeros_like(acc)

    hx = h_scr[...]
    a_ext = _dot(hx, wa_ref[...])
    rid = lax.broadcasted_iota(jnp.int32, (halo + tm, 1), 0)
    a_scr[...] = jnp.where((rid >= halo) | (i > 0), a_ext, 0.0)
    cw = cw_ref[...]
    conv = cb_ref[...]
    for j in range(FFN_CONV):
        off = halo - (FFN_CONV - 1) + j
        conv = conv + cw[j:j + 1, :] * a_scr[off:off + tm, :]
    g = _dot(h_scr[halo:halo + tm, :], wg_ref[...])
    act = (_silu(conv) * g).astype(BF16)
    acc[...] += _dot(act, wd_ref[...])

    @pl.when(f == pl.num_programs(1) - 1)
    def _():
        z = ALPHA * x_ref[...] + gate_ref[...] * acc[...]
        o_ref[...] = _ln_rows(z) * lg_ref[...] + lb_ref[...]


def _conv_ffn(x, sc, sh, gate, w_up, conv_w, conv_b, w_down, ln_g, ln_b, tm, tf):
    s, d = x.shape
    dff = w_down.shape[0]
    halo = 16
    nf = dff // tf
    row = lambda i, f: (0, 0)
    return pl.pallas_call(
        functools.partial(_ffn_kernel, halo=halo),
        grid=(s // tm, nf),
        in_specs=[
            pl.BlockSpec((tm, d), lambda i, f: (i, 0)),
            pl.BlockSpec((halo, d), lambda i, f: (jnp.maximum(i * (tm // halo) - 1, 0), 0)),
            pl.BlockSpec((1, d), row),
            pl.BlockSpec((1, d), row),
            pl.BlockSpec((1, d), row),
            pl.BlockSpec((d, tf), lambda i, f: (0, f)),
            pl.BlockSpec((d, tf), lambda i, f: (0, nf + f)),
            pl.BlockSpec((FFN_CONV, tf), lambda i, f: (0, f)),
            pl.BlockSpec((1, tf), lambda i, f: (0, f)),
            pl.BlockSpec((tf, d), lambda i, f: (f, 0)),
            pl.BlockSpec((1, d), row),
            pl.BlockSpec((1, d), row),
        ],
        out_specs=pl.BlockSpec((tm, d), lambda i, f: (i, 0)),
        out_shape=jax.ShapeDtypeStruct((s, d), F32),
        scratch_shapes=[
            pltpu.VMEM((halo + tm, d), BF16),
            pltpu.VMEM((halo + tm, tf), F32),
            pltpu.VMEM((tm, d), F32),
        ],
        compiler_params=_cparams(("parallel", "arbitrary")),
        name="conv_ffn",
    )(x, x, sc, sh, gate, w_up, w_up, conv_w, conv_b, w_down, ln_g, ln_b)


def _token_mixer(x, sc, sh, gate, w_in, cmp_pe, cmp_w1, cmp_w2, ml_conv_w, ml_conv_b, ml_gate_b,
                 ml_norm_g, w_br_nsa, w_br_ml, w_o, ln_g, ln_b):
    s, d = x.shape
    wb, wf = _split_w_in(w_in)
    projb, projf = _inproj(x, sc, sh, wb, wf, min(INPROJ_TM, s), INPROJ_TN)
    kcv = _compress(projf, cmp_pe, cmp_w1, cmp_w2)
    y_nsa = _nsa_attention(projb, projf, kcv)
    ifo = _F_SMALL + 2 * 128
    gates_col = projf[:, ifo:ifo + _ML_IF] + ml_gate_b[None, :]
    y_ml = _mlstm(projb, projf, gates_col, gates_col.T, ml_conv_w, ml_conv_b[None, :],
                  ml_norm_g[None, :], cpb=min(MLSTM_CHUNKS_PER_STEP, s // ML_CHUNK))
    return _merge_outproj(y_nsa, y_ml, projf, x, w_br_nsa.astype(BF16), w_br_ml.astype(BF16),
                          w_o.astype(BF16), gate, ln_g, ln_b, tm=min(MERGE_TM, s))


def _forward(x, c, w_ada, b_ada, w_in, cmp_pe, cmp_w1, cmp_w2, ml_conv_w, ml_conv_b, ml_gate_b,
             ml_norm_g, w_br_nsa, w_br_ml, w_o, w_up, ffn_conv_w, ffn_conv_b, w_down, ln_g, ln_b):
    b, s, d = x.shape
    assert b == 1 and d == D_MODEL
    depth = w_ada.shape[0]
    mod = _modulation(c, w_ada, b_ada)
    xs = x[0]
    for l in range(depth):
        sh1, sc1, g1, sh2, sc2, g2 = [mod[l, :, k * d:(k + 1) * d] for k in range(6)]
        xs = _token_mixer(xs, sc1, sh1, g1, w_in[l], cmp_pe[l], cmp_w1[l], cmp_w2[l], ml_conv_w[l],
                          ml_conv_b[l], ml_gate_b[l], ml_norm_g[l], w_br_nsa[l], w_br_ml[l], w_o[l],
                          ln_g[l, 0][None, :], ln_b[l, 0][None, :])
        xs = _conv_ffn(xs, sc2, sh2, g2, w_up[l].astype(BF16), ffn_conv_w[l], ffn_conv_b[l][None, :],
                       w_down[l].astype(BF16), ln_g[l, 1][None, :], ln_b[l, 1][None, :],
                       tm=min(FFN_TM, s), tf=FFN_TF)
    return xs[None]


def kernel(x, c, w_ada, b_ada, w_in, cmp_pe, cmp_w1, cmp_w2, ml_conv_w, ml_conv_b, ml_gate_b, ml_norm_g, w_br_nsa, w_br_ml, w_o, w_up, ffn_conv_w, ffn_conv_b, w_down, ln_g, ln_b):
    return _forward(x, c, w_ada, b_ada, w_in, cmp_pe, cmp_w1, cmp_w2, ml_conv_w, ml_conv_b,
                    ml_gate_b, ml_norm_g, w_br_nsa, w_br_ml, w_o, w_up, ffn_conv_w, ffn_conv_b,
                    w_down, ln_g, ln_b)
```

```python
import functools
import math

import jax
import jax.numpy as jnp
from jax import lax
from jax.experimental import pallas as pl
from jax.experimental.pallas import tpu as pltpu

F32 = jnp.float32
BF16 = jnp.bfloat16

D_MODEL = 2048
DEPTH = 2
NSA_HEADS = 8
NSA_GROUPS = 2
NSA_HPG = NSA_HEADS // NSA_GROUPS
NSA_HD = 128
CMP_LEN = 32
CMP_STRIDE = 16
CMP_HID = 256
SLC_LEN = 64
SLC_TOPK = 16
WINDOW = 512
Q_BLOCK = 128
ML_HEADS = 4
ML_DQK = 128
ML_DV = 256
ML_CHUNK = 64
ML_CONV = 4
D_FF = 5632
FFN_CONV = 3
ALPHA = (2 * DEPTH) ** 0.25
LN_EPS = 1e-5
NEG_INF = -1e30

V7X_VMEM_BYTES = 64 * 1024 * 1024
VMEM_LIMIT = V7X_VMEM_BYTES - 8 * 1024 * 1024

INPROJ_TM, INPROJ_TN = 1024, 1024
MERGE_TM = 256
FFN_TM, FFN_TF = 512, 512
MLSTM_CHUNKS_PER_STEP = 4

_NSA_Q = NSA_HEADS * NSA_HD
_NSA_KV = 3 * 2 * NSA_GROUPS * NSA_HD
_NSA_G = 3 * NSA_HEADS
_ML_QK = 2 * ML_HEADS * ML_DQK
_ML_V = ML_HEADS * ML_DV
_ML_IF = 2 * ML_HEADS
_CMP_COLS = 2 * NSA_GROUPS * NSA_HD
_F_MERGE = 0
_F_CMP = 2 * D_MODEL
_F_MLQK = _F_CMP + _CMP_COLS
_F_MLO = _F_MLQK + _ML_QK
_F_SMALL = _F_MLO + _ML_V
_F_COLS = _F_SMALL + 4 * 128
_B_Q = 0
_B_KV = _NSA_Q
_B_MLV = _B_KV + 8 * NSA_HD
_B_COLS = _B_MLV + _ML_V

SEL_TILE = 512
SEL_STEP = 4096
SEL_BIG = 2.0 ** 60
NSA_Q_BLOCKS_PER_STEP = 1
SEL_EXP2_GUARD = 64.0


def _cparams(sem):
    return pltpu.CompilerParams(dimension_semantics=sem, vmem_limit_bytes=VMEM_LIMIT)


def _ln_rows(x):
    mu = jnp.mean(x, axis=-1, keepdims=True)
    xc = x - mu
    var = jnp.mean(xc * xc, axis=-1, keepdims=True)
    return xc * lax.rsqrt(var + LN_EPS)


def _sigmoid(x):
    return 1.0 / (1.0 + jnp.exp(-x))


def _silu(x):
    return x * _sigmoid(x)


def _gelu_tanh(x):
    c = math.sqrt(2.0 / math.pi)
    return x * (0.5 * (1.0 + jnp.tanh(c * (x + 0.044715 * (x * x * x)))))


def _log_sigmoid(x):
    return jnp.minimum(x, 0.0) - jnp.log1p(jnp.exp(-jnp.abs(x)))


def _dot(a, b):
    return jnp.dot(a, b, preferred_element_type=F32)


def _dot_nt(a, b):
    return lax.dot_general(a, b, (((1,), (1,)), ((), ())), preferred_element_type=F32)


def _masked_softmax(raw, mask, scale):
    raw = jnp.where(mask, raw, NEG_INF)
    m = jnp.max(raw, axis=-1, keepdims=True)
    e = jnp.exp2((raw - m) * (scale * math.log2(math.e)))
    den = jnp.sum(e, axis=-1, keepdims=True)
    return e * jnp.where(m > 0.5 * NEG_INF, 1.0 / den, 0.0)


def _mod_kernel(c_ref, w_ref, b_ref, o_ref):
    ca = _silu(c_ref[...])
    o = _dot(ca.astype(BF16), w_ref[0].astype(BF16))
    o_ref[0] = o[0:1] + b_ref[0]


def _modulation(c, w_ada, b_ada):
    depth, d, n = w_ada.shape
    tn = INPROJ_TN
    c8 = jnp.broadcast_to(c, (8, d))
    return pl.pallas_call(
        _mod_kernel,
        grid=(depth, n // tn),
        in_specs=[
            pl.BlockSpec((8, d), lambda l, j: (0, 0)),
            pl.BlockSpec((1, d, tn), lambda l, j: (l, 0, j)),
            pl.BlockSpec((1, 1, tn), lambda l, j: (l, 0, j)),
        ],
        out_specs=pl.BlockSpec((1, 1, tn), lambda l, j: (l, 0, j)),
        out_shape=jax.ShapeDtypeStruct((depth, 1, n), F32),
        compiler_params=_cparams(("parallel", "parallel")),
        name="adaln_mod",
    )(c8, w_ada, b_ada.reshape(depth, 1, n))


def _inproj_kernel(x_ref, sc_ref, sh_ref, wb_ref, wf_ref, ob_ref, of_ref, h_scr, *, nb):
    j = pl.program_id(1)

    @pl.when(j == 0)
    def _():
        h = _ln_rows(x_ref[...]) * (1.0 + sc_ref[...]) + sh_ref[...]
        h_scr[...] = h.astype(BF16)

    @pl.when(j < nb)
    def _():
        ob_ref[...] = _dot(h_scr[...], wb_ref[...]).astype(ob_ref.dtype)

    @pl.when(j >= nb)
    def _():
        of_ref[...] = _dot(h_scr[...], wf_ref[...])


def _inproj(x, sc, sh, wb, wf, tm, tn):
    s, d = x.shape
    nb, nf = wb.shape[1] // tn, wf.shape[1] // tn
    bcol = lambda i, j: (i, jnp.minimum(j, nb - 1))
    fcol = lambda i, j: (i, jnp.maximum(j - nb, 0))
    return pl.pallas_call(
        functools.partial(_inproj_kernel, nb=nb),
        grid=(s // tm, nb + nf),
        in_specs=[
            pl.BlockSpec((tm, d), lambda i, j: (i, 0)),
            pl.BlockSpec((1, d), lambda i, j: (0, 0)),
            pl.BlockSpec((1, d), lambda i, j: (0, 0)),
            pl.BlockSpec((d, tn), lambda i, j: (0, jnp.minimum(j, nb - 1))),
            pl.BlockSpec((d, tn), lambda i, j: (0, jnp.maximum(j - nb, 0))),
        ],
        out_specs=[pl.BlockSpec((tm, tn), bcol), pl.BlockSpec((tm, tn), fcol)],
        out_shape=[jax.ShapeDtypeStruct((s, wb.shape[1]), BF16), jax.ShapeDtypeStruct((s, wf.shape[1]), F32)],
        scratch_shapes=[pltpu.VMEM((tm, d), BF16)],
        compiler_params=_cparams(("parallel", "arbitrary")),
        name="in_proj",
    )(x, sc, sh, wb, wf)


def _split_w_in(w):
    d = w.shape[0]
    o = 0
    q = w[:, o:o + _NSA_Q]; o += _NSA_Q
    kv = w[:, o:o + _NSA_KV]; o += _NSA_KV
    g = w[:, o:o + _NSA_G]; o += _NSA_G
    mlqk = w[:, o:o + _ML_QK]; o += _ML_QK
    mlv = w[:, o:o + _ML_V]; o += _ML_V
    mlif = w[:, o:o + _ML_IF]; o += _ML_IF
    mlo = w[:, o:o + _ML_V]; o += _ML_V
    merge = w[:, o:o + 2 * D_MODEL]
    per_g = 3 * NSA_HPG
    z = lambda n: jnp.zeros((d, n), w.dtype)
    wb = jnp.concatenate([q, kv[:, _CMP_COLS:], mlv], axis=1)
    wf = jnp.concatenate(
        [merge, kv[:, :_CMP_COLS], mlqk, mlo,
         g[:, :per_g], z(128 - per_g), g[:, per_g:], z(128 - per_g), mlif, z(128 - _ML_IF), z(128)],
        axis=1)
    return wb.astype(BF16), wf.astype(BF16)


def _compress_kernel(a_ref, pe_ref, w1_ref, w2_ref, o_ref):
    n = o_ref.shape[1]
    hd = NSA_HD

    def half_sum(lo):
        acc = jnp.zeros((n, CMP_HID), F32)
        for l in range(0, CMP_STRIDE, 2):
            x = jnp.concatenate(
                [a_ref[pl.ds(l + u, n, stride=CMP_STRIDE), :] + pe_ref[0, lo + l + u:lo + l + u + 1, :]
                 for u in range(2)], axis=1)
            w = w1_ref[0, (lo + l) * hd:(lo + l + 2) * hd, :]
            acc = acc + _dot(x.astype(BF16), w.astype(BF16))
        return acc

    pre = half_sum(0) + pltpu.roll(half_sum(CMP_STRIDE), n - 1, 0)
    g = _gelu_tanh(pre)
    o_ref[0] = _dot(g.astype(BF16), w2_ref[0].astype(BF16)).astype(o_ref.dtype)


def _compress(projf, pe, w1, w2):
    s = projf.shape[0]
    n = s // CMP_STRIDE
    four = 2 * NSA_GROUPS
    return pl.pallas_call(
        _compress_kernel,
        grid=(four,),
        in_specs=[
            pl.BlockSpec((s, NSA_HD), lambda j: (0, _F_CMP // NSA_HD + j)),
            pl.BlockSpec((1, CMP_LEN, NSA_HD), lambda j: (j // NSA_GROUPS, 0, 0)),
            pl.BlockSpec((1, CMP_LEN * NSA_HD, CMP_HID), lambda j: (j // NSA_GROUPS, 0, 0)),
            pl.BlockSpec((1, CMP_HID, NSA_HD), lambda j: (j // NSA_GROUPS, 0, 0)),
        ],
        out_specs=pl.BlockSpec((1, n, NSA_HD), lambda j: (j, 0, 0)),
        out_shape=jax.ShapeDtypeStruct((four, n, NSA_HD), BF16),
        compiler_params=_cparams(("parallel",)),
        name="nsa_compress",
    )(projf, pe, w1, w2)


def _nsa_kernel(q_ref, kc_ref, vc_ref, ks_ref, vs_ref, kw_ref, vw_ref, gate_ref, ovt_ref, eye_ref,
                o_ref, nsel_scr, npast_scr, *, step, nq):
    pid = pl.program_id(1)
    rows = NSA_HPG * Q_BLOCK
    scale = NSA_HD ** -0.5
    c2 = scale * math.log2(math.e)
    row_q = lax.broadcasted_iota(jnp.int32, (rows, 1), 0) & (Q_BLOCK - 1)
    lane_q = lax.broadcasted_iota(jnp.int32, (1, Q_BLOCK), 1)
    blk = []
    for u in range(nq):
        i = pid * nq + u
        qt = q_ref[u * Q_BLOCK:(u + 1) * Q_BLOCK, :]
        qs = jnp.concatenate([qt[:, h * NSA_HD:(h + 1) * NSA_HD] for h in range(NSA_HPG)], axis=0)
        blk.append((i, qs, i * Q_BLOCK + row_q))
    i_last = blk[-1][0]

    ncp = kc_ref.shape[1]

    def cmp_branch(width):
        outs = []
        cidx = lax.broadcasted_iota(jnp.int32, (rows, width), 1)
        ovt = ovt_ref[:, :width]
        for _, qs, t in blk:
            last_c = (t - (CMP_LEN - 1)) // CMP_STRIDE
            p_c = _masked_softmax(_dot_nt(qs, kc_ref[0, :width, :]), cidx <= last_c, scale)
            o = _dot(p_c.astype(BF16), vc_ref[0, :width, :])
            psum = (p_c[0:Q_BLOCK] + p_c[Q_BLOCK:2 * Q_BLOCK]
                    + p_c[2 * Q_BLOCK:3 * Q_BLOCK] + p_c[3 * Q_BLOCK:4 * Q_BLOCK])
            outs += [o, _dot_nt(ovt, psum.astype(BF16))]
        return tuple(outs)

    n_vis = (i_last * Q_BLOCK + Q_BLOCK - CMP_LEN) // CMP_STRIDE + 1
    branch = lambda: cmp_branch(ncp)
    for width in (ncp // 2, ncp // 4):
        if width % 128 == 0:
            branch = (lambda w, other: lambda: lax.cond(n_vis <= w, lambda: cmp_branch(w), other))(width, branch)
    cmp_out = branch()
    o_cs, imps = cmp_out[0::2], cmp_out[1::2]

    wsl = imps[0].shape[0]
    bid = lax.broadcasted_iota(jnp.int32, (wsl, Q_BLOCK), 0)
    bidf = bid.astype(F32)
    taken = -1e9
    nsel0 = jnp.full((wsl, Q_BLOCK), -SEL_BIG, F32)
    score0s = []
    for (i, _, _), imp in zip(blk, imps):
        t_row = i * Q_BLOCK + lane_q
        cur = t_row // SLC_LEN
        score0s.append(jnp.where(bid == 0, 3e6, jnp.where(bid == cur, 2e6, jnp.where(
            bid == cur - 1, 1e6, jnp.where(bid * SLC_LEN <= t_row, imp, -1.0 - bidf)))))

    def take_max(c, lanes=None):
        sc, ns = c
        hit = sc == jnp.max(sc, axis=0, keepdims=True)
        if lanes is not None:
            hit = hit & lanes
        return jnp.where(hit, taken, sc), jnp.where(hit, 0.0, ns)

    cs = [(jnp.where(s0 >= 1e6, taken, s0), jnp.where(s0 >= 1e6, 0.0, nsel0)) for s0 in score0s]
    n_min = SLC_TOPK - 3
    for _ in range(n_min):
        cs = [take_max(c) for c in cs]

    def early_rounds():
        cur = lane_q // SLC_LEN
        n_free = SLC_TOPK - (1 + jnp.where(cur >= 1, 1, 0) + jnp.where(cur >= 2, 1, 0))
        e = cs[0]
        for r in range(n_min + 1, SLC_TOPK):
            e = take_max(e, n_free >= r)
        return e

    cs[0] = lax.cond(pid == 0, early_rounds, lambda: cs[0])
    nsels = tuple(c[1] for c in cs)

    def topk_ties():
        def one(_, c):
            sc, ns = c
            mx = jnp.max(sc, axis=0, keepdims=True)
            first = jnp.min(jnp.where(sc == mx, bidf, float(wsl)), axis=0, keepdims=True)
            hit = bidf == first
            return jnp.where(hit, taken, sc), jnp.where(hit, 0.0, ns)
        return tuple(lax.fori_loop(0, SLC_TOPK, one, (s0, nsel0))[1] for s0 in score0s)

    n_taken = functools.reduce(jnp.maximum, [jnp.sum(jnp.where(ns == 0.0, 1.0, 0.0), axis=0, keepdims=True)
                                             for ns in nsels])
    nsels = lax.cond(jnp.max(n_taken) > SLC_TOPK, topk_ties, lambda: nsels)
    for u, ((i, _, _), ns) in enumerate(zip(blk, nsels)):
        blk_d = (i * Q_BLOCK) // SLC_LEN
        nsel_scr[u] = ns
        npast_scr[u] = jnp.where(bid < blk_d, ns, -SEL_BIG)

    def keys_aug(k0, width, sel_scr):
        b0 = k0 // SLC_LEN
        bias = jnp.concatenate(
            [jnp.broadcast_to(sel_scr[pl.ds(b0 + b, 1), :], (SLC_LEN, Q_BLOCK))
             for b in range(width // SLC_LEN)], axis=0)
        return jnp.concatenate([ks_ref[pl.ds(k0, width), :], bias.astype(BF16)], axis=1)

    def vals_aug(k0, width):
        ones_col = (lax.broadcasted_iota(jnp.int32, (width, NSA_HD), 1) == 0).astype(BF16)
        return jnp.concatenate([vs_ref[pl.ds(k0, width), :], ones_col], axis=1)

    qas, diag, o_ws = [], [], []
    wlen = WINDOW + Q_BLOCK
    for u, (i, qs, t) in enumerate(blk):
        qa = jnp.concatenate([qs, eye_ref[...]], axis=1)
        k_d = pl.multiple_of(i * Q_BLOCK, Q_BLOCK)
        s_d = _dot_nt(qa, keys_aug(k_d, Q_BLOCK, nsel_scr.at[u]))
        s_d = jnp.where(k_d + lane_q <= t, s_d, NEG_INF)
        m_d = jnp.max(s_d, axis=-1, keepdims=True)
        acc_d = _dot(jnp.exp2((s_d - m_d) * c2).astype(BF16), vals_aug(k_d, Q_BLOCK))
        qas.append(qa)
        diag.append((m_d, acc_d))
        ks0 = pl.multiple_of(jnp.maximum(i - WINDOW // Q_BLOCK, 0) * Q_BLOCK, Q_BLOCK)
        s_w = _dot_nt(qs, kw_ref[pl.ds(ks0, wlen), :])
        kpos = ks0 + lax.broadcasted_iota(jnp.int32, (1, wlen), 1)
        p_w = _masked_softmax(s_w, (kpos <= t) & (kpos > t - WINDOW), scale)
        o_ws.append(_dot(p_w.astype(BF16), vw_ref[pl.ds(ks0, wlen), :]))

    gates = _sigmoid(gate_ref[...])
    for u, (i, qs, t) in enumerate(blk):
        qa = qas[u]
        past_scr = npast_scr.at[u]
        m_d, acc_d = diag[u]

        def exact_step(k0, carry):
            m, acc = carry
            k0 = pl.multiple_of(k0, SEL_TILE)
            s = _dot_nt(qa, keys_aug(k0, SEL_TILE, past_scr))
            m_new = jnp.maximum(m, jnp.max(s, axis=-1, keepdims=True))
            p = jnp.exp2((s - m_new) * c2)
            return m_new, jnp.exp2((m - m_new) * c2) * acc + _dot(p.astype(BF16), vals_aug(k0, SEL_TILE))

        def fast_step(k0, width, carry):
            m, acc, risk = carry
            k0 = pl.multiple_of(k0, SEL_TILE)
            s = _dot_nt(qa, keys_aug(k0, width, past_scr))
            p = jnp.exp2((s - m) * c2)
            mx = jnp.max(s, axis=-1, keepdims=True)
            m_new = jnp.maximum(m, mx)
            acc = (acc + _dot(p.astype(BF16), vals_aug(k0, width))) * jnp.exp2((m - m_new) * c2)
            return m_new, acc, jnp.maximum(risk, mx - m)

        n_past = i * Q_BLOCK
        n_steps = n_past // step
        carry = (m_d, acc_d, jnp.zeros((rows, 1), F32))
        carry = lax.fori_loop(0, n_steps, lambda j, c: fast_step(j * step, step, c), carry)
        rem = n_past - n_steps * step
        k_r = n_steps * step
        carry = lax.cond(
            rem > step // 2, lambda: fast_step(k_r, step, carry),
            lambda: lax.cond(rem > step // 4, lambda: fast_step(k_r, step // 2, carry),
                             lambda: lax.cond(rem > 0, lambda: fast_step(k_r, step // 4, carry),
                                              lambda: carry)))
        _, acc_s, risk = carry

        def redo_exact():
            n_tiles = (n_past + SEL_TILE - 1) // SEL_TILE
            return lax.fori_loop(0, n_tiles, lambda j, c: exact_step(j * SEL_TILE, c), (m_d, acc_d))[1]

        acc_s = lax.cond(jnp.max(risk) * c2 > SEL_EXP2_GUARD, redo_exact, lambda: acc_s)
        o_s = acc_s[:, :NSA_HD] / acc_s[:, NSA_HD:NSA_HD + 1]

        g_u = gates[u * Q_BLOCK:(u + 1) * Q_BLOCK]
        for h in range(NSA_HPG):
            r = slice(h * Q_BLOCK, (h + 1) * Q_BLOCK)
            out = (g_u[:, 3 * h:3 * h + 1] * o_cs[u][r] + g_u[:, 3 * h + 1:3 * h + 2] * o_s[r]
                   + g_u[:, 3 * h + 2:3 * h + 3] * o_ws[u][r])
            o_ref[u * Q_BLOCK:(u + 1) * Q_BLOCK, h * NSA_HD:(h + 1) * NSA_HD] = out.astype(o_ref.dtype)


def _nsa_attention(projb, projf, kcv, single_buffer=True):
    s = projb.shape[0]
    n_slc = s // SLC_LEN
    step = min(SEL_STEP, s)
    nq = NSA_Q_BLOCKS_PER_STEP
    qrows = nq * Q_BLOCK
    assert n_slc >= SLC_TOPK and s >= WINDOW + Q_BLOCK and s % step == 0 and step % (4 * SEL_TILE) == 0
    ncp = kcv.shape[1]
    wsl = max(n_slc, 128)
    hd = NSA_HD
    c_start = jnp.arange(ncp)[None, :] * CMP_STRIDE
    s_start = jnp.arange(wsl)[:, None] * SLC_LEN
    ovt = ((c_start < s_start + SLC_LEN) & (c_start + CMP_LEN > s_start)
           & (jnp.arange(wsl)[:, None] < n_slc)).astype(BF16)
    eye = jnp.tile(jnp.eye(Q_BLOCK, dtype=BF16), (NSA_HPG, 1))
    resident = dict(pipeline_mode=pl.Buffered(1)) if single_buffer else {}
    kvb = _B_KV // hd
    gb = _F_SMALL // 128
    return pl.pallas_call(
        functools.partial(_nsa_kernel, step=step, nq=nq),
        grid=(NSA_GROUPS, s // qrows),
        in_specs=[
            pl.BlockSpec((qrows, NSA_HPG * hd), lambda g, i: (i, g)),
            pl.BlockSpec((1, ncp, hd), lambda g, i: (g, 0, 0)),
            pl.BlockSpec((1, ncp, hd), lambda g, i: (NSA_GROUPS + g, 0, 0)),
            pl.BlockSpec((s, hd), lambda g, i: (0, kvb + g), **resident),
            pl.BlockSpec((s, hd), lambda g, i: (0, kvb + 2 + g), **resident),
            pl.BlockSpec((s, hd), lambda g, i: (0, kvb + 4 + g), **resident),
            pl.BlockSpec((s, hd), lambda g, i: (0, kvb + 6 + g), **resident),
            pl.BlockSpec((qrows, 128), lambda g, i: (i, gb + g)),
            pl.BlockSpec((wsl, ncp), lambda g, i: (0, 0)),
            pl.BlockSpec((NSA_HPG * Q_BLOCK, Q_BLOCK), lambda g, i: (0, 0)),
        ],
        out_specs=pl.BlockSpec((qrows, NSA_HPG * hd), lambda g, i: (i, g)),
        out_shape=jax.ShapeDtypeStruct((s, NSA_HEADS * hd), BF16),
        scratch_shapes=[pltpu.VMEM((nq, wsl, Q_BLOCK), F32), pltpu.VMEM((nq, wsl, Q_BLOCK), F32)],
        compiler_params=_cparams(("parallel", "arbitrary")),
        name="nsa_attention",
    )(projb, kcv, kcv, projb, projb, projb, projb, projf, ovt, eye)


def _mlstm_kernel(q_ref, k_ref, v_ref, o0_ref, o1_ref, gc_ref, gr_ref, cw_ref, cb_ref, ng_ref,
                  y_ref, extq, extk, c_st, n_st, m_st, *, cpb):
    step = pl.program_id(0)
    rows = cpb * ML_CHUNK
    half = ML_HEADS * ML_DQK
    L = ML_CHUNK

    @pl.when(step == 0)
    def _():
        extq[0:8, :] = jnp.zeros((8, half), F32)
        extk[0:8, :] = jnp.zeros((8, half), F32)
        c_st[...] = jnp.zeros_like(c_st)
        n_st[...] = jnp.zeros_like(n_st)
        m_st[...] = jnp.zeros_like(m_st)

    extq[8:8 + rows, :] = q_ref[...]
    extk[8:8 + rows, :] = k_ref[...]
    cw = cw_ref[...]
    cb = cb_ref[...]

    def conv(ext, lo):
        acc = cb[:, lo:lo + half]
        for j in range(ML_CONV):
            off = 8 - (ML_CONV - 1) + j
            acc = acc + cw[j:j + 1, lo:lo + half] * ext[off:off + rows, :]
        return _silu(acc)

    qa = conv(extq, 0)
    ka = conv(extk, half) * (ML_DQK ** -0.5)
    extq[0:8, :] = extq[rows:rows + 8, :]
    extk[0:8, :] = extk[rows:rows + 8, :]

    gcol = gc_ref[...]
    grow = gr_ref[...]
    lf_col_all = _log_sigmoid(gcol[:, ML_HEADS:])
    lf_row_all = _log_sigmoid(grow[ML_HEADS:, :])
    ri = lax.broadcasted_iota(jnp.int32, (L, L), 0)
    ci = lax.broadcasted_iota(jnp.int32, (L, L), 1)
    tri = ri >= ci
    ng = ng_ref[...]

    state = [(c_st[h], n_st[h], m_st[h][:, 0:1]) for h in range(ML_HEADS)]
    ones_blk = jnp.ones((L, 128), BF16)
    for c in range(cpb):
        r0 = c * L
        kt_c = ka[r0:r0 + L, :].T
        for h in range(ML_HEADS):
            qh = qa[r0:r0 + L, h * ML_DQK:(h + 1) * ML_DQK]
            kh = ka[r0:r0 + L, h * ML_DQK:(h + 1) * ML_DQK]
            vh = v_ref[r0:r0 + L, h * ML_DV:(h + 1) * ML_DV]
            ig_col = gcol[r0:r0 + L, h:h + 1]
            ig_row = grow[h:h + 1, r0:r0 + L]
            lf_col = lf_col_all[r0:r0 + L, h:h + 1]
            lf_row = lf_row_all[h:h + 1, r0:r0 + L]
            bcum_col = jnp.sum(jnp.where(tri, lf_row, 0.0), axis=1, keepdims=True)
            bcum_row = jnp.sum(jnp.where(ri <= ci, lf_col, 0.0), axis=0, keepdims=True)
            b_last = jnp.sum(lf_row, axis=1, keepdims=True)
            dmat = jnp.where(tri, bcum_col - bcum_row + ig_row, NEG_INF)
            a_loc = jnp.max(dmat, axis=-1, keepdims=True)
            qb = qh.astype(BF16)
            s_loc = _dot_nt(qb, kh.astype(BF16)) * jnp.exp(dmat - a_loc)
            sv = _dot(s_loc.astype(BF16), vh)
            s_sum = _dot(s_loc.astype(BF16), ones_blk)[:, 0:1]
            a_col = b_last - bcum_col + ig_col
            a_row = b_last - bcum_row + ig_row
            a_max = jnp.max(a_row, axis=-1, keepdims=True)
            k_sum = jnp.sum(kh * jnp.exp(a_col - a_max), axis=0, keepdims=True)
            kwt = kt_c[h * ML_DQK:(h + 1) * ML_DQK, :] * jnp.exp(a_row - a_max)
            kv = _dot(kwt.astype(BF16), vh)

            ct, nrow, m_old = state[h]
            inter = bcum_col + m_old
            m_t = jnp.maximum(inter, a_loc)
            w_inter = jnp.exp(inter - m_t)
            w_loc = jnp.exp(a_loc - m_t)
            num = w_inter * _dot(qb, ct.astype(BF16)) + w_loc * sv
            qn = _dot_nt(qb, jnp.broadcast_to(nrow, (8, ML_DQK)).astype(BF16))[:, 0:1]
            den = w_inter * qn + w_loc * s_sum
            hout = num / jnp.maximum(jnp.abs(den), jnp.exp(-m_t))

            m_new = jnp.maximum(b_last + m_old, a_max)
            decay = jnp.exp(b_last + m_old - m_new)
            g_new = jnp.exp(a_max - m_new)
            state[h] = (decay * ct + g_new * kv, decay * nrow + g_new * k_sum, m_new)

            o_ref = o0_ref if h < ML_HEADS // 2 else o1_ref
            oc = (h % (ML_HEADS // 2)) * ML_DV
            og = _sigmoid(o_ref[r0:r0 + L, oc:oc + ML_DV])
            yn = _ln_rows(hout) * ng[:, h * ML_DV:(h + 1) * ML_DV]
            y_ref[r0:r0 + L, h * ML_DV:(h + 1) * ML_DV] = (yn * og).astype(y_ref.dtype)

    for h in range(ML_HEADS):
        c_st[h], n_st[h] = state[h][0], state[h][1]
        m_st[h] = jnp.broadcast_to(state[h][2], (1, 128))


def _mlstm(projb, projf, gates_col, gates_row, conv_w, conv_b, norm_g, cpb):
    s = projb.shape[0]
    rows = cpb * ML_CHUNK
    half = ML_HEADS * ML_DQK
    dv = ML_HEADS * ML_DV
    qblk = _F_MLQK // half
    oblk = _F_MLO // half
    return pl.pallas_call(
        functools.partial(_mlstm_kernel, cpb=cpb),
        grid=(s // rows,),
        in_specs=[
            pl.BlockSpec((rows, half), lambda j: (j, qblk)),
            pl.BlockSpec((rows, half), lambda j: (j, qblk + 1)),
            pl.BlockSpec((rows, dv), lambda j: (j, _B_MLV // dv)),
            pl.BlockSpec((rows, half), lambda j: (j, oblk)),
            pl.BlockSpec((rows, half), lambda j: (j, oblk + 1)),
            pl.BlockSpec((rows, 2 * ML_HEADS), lambda j: (j, 0)),
            pl.BlockSpec((2 * ML_HEADS, rows), lambda j: (0, j)),
            pl.BlockSpec((ML_CONV, 2 * half), lambda j: (0, 0)),
            pl.BlockSpec((1, 2 * half), lambda j: (0, 0)),
            pl.BlockSpec((1, dv), lambda j: (0, 0)),
        ],
        out_specs=pl.BlockSpec((rows, dv), lambda j: (j, 0)),
        out_shape=jax.ShapeDtypeStruct((s, dv), BF16),
        scratch_shapes=[
            pltpu.VMEM((rows + 8, half), F32),
            pltpu.VMEM((rows + 8, half), F32),
            pltpu.VMEM((ML_HEADS, ML_DQK, ML_DV), F32),
            pltpu.VMEM((ML_HEADS, 1, ML_DQK), F32),
            pltpu.VMEM((ML_HEADS, 1, 128), F32),
        ],
        compiler_params=_cparams(("arbitrary",)),
        name="mlstm",
    )(projf, projf, projb, projf, projf, gates_col, gates_row, conv_w, conv_b, norm_g)


def _merge_kernel(yn_ref, ym_ref, g0_ref, g1_ref, x_ref, wn_ref, wm_ref, wo_ref, gate_ref,
                  lg_ref, lb_ref, o_ref):
    a = _dot(yn_ref[...], wn_ref[...])
    b = _dot(ym_ref[...], wm_ref[...])
    merged = _sigmoid(g0_ref[...]) * a + _sigmoid(g1_ref[...]) * b
    y = _dot(merged.astype(BF16), wo_ref[...])
    z = ALPHA * x_ref[...] + gate_ref[...] * y
    o_ref[...] = _ln_rows(z) * lg_ref[...] + lb_ref[...]


def _merge_outproj(y_nsa, y_ml, projf, x, wn, wm, wo, gate, ln_g, ln_b, tm, single_buffer=True):
    s, d = x.shape
    resident = dict(pipeline_mode=pl.Buffered(1)) if single_buffer else {}
    mb = _F_MERGE // d
    row = lambda i: (0, 0)
    return pl.pallas_call(
        _merge_kernel,
        grid=(s // tm,),
        in_specs=[
            pl.BlockSpec((tm, y_nsa.shape[1]), lambda i: (i, 0)),
            pl.BlockSpec((tm, y_ml.shape[1]), lambda i: (i, 0)),
            pl.BlockSpec((tm, d), lambda i: (i, mb)),
            pl.BlockSpec((tm, d), lambda i: (i, mb + 1)),
            pl.BlockSpec((tm, d), lambda i: (i, 0)),
            pl.BlockSpec(wn.shape, row, **resident),
            pl.BlockSpec(wm.shape, row, **resident),
            pl.BlockSpec(wo.shape, row, **resident),
            pl.BlockSpec((1, d), row),
            pl.BlockSpec((1, d), row),
            pl.BlockSpec((1, d), row),
        ],
        out_specs=pl.BlockSpec((tm, d), lambda i: (i, 0)),
        out_shape=jax.ShapeDtypeStruct((s, d), F32),
        compiler_params=_cparams(("parallel",)),
        name="merge_outproj",
    )(y_nsa, y_ml, projf, projf, x, wn, wm, wo, gate, ln_g, ln_b)


def _ffn_kernel(x_ref, xh_ref, sc_ref, sh_ref, gate_ref, wa_ref, wg_ref, cw_ref, cb_ref, wd_ref,
                lg_ref, lb_ref, o_ref, h_scr, a_scr, acc, *, halo):
    i = pl.program_id(0)
    f = pl.program_id(1)
    tm = x_ref.shape[0]

    @pl.when(f == 0)
    def _():
        mod = lambda v: (_ln_rows(v) * (1.0 + sc_ref[...]) + sh_ref[...]).astype(BF16)
        h_scr[0:halo, :] = mod(xh_ref[...])
        h_scr[halo:halo + tm, :] = mod(x_ref[...])
        acc[...] = jnp.zeros_like(acc)

    hx = h_scr[...]
    a_ext = _dot(hx, wa_ref[...])
    rid = lax.broadcasted_iota(jnp.int32, (halo + tm, 1), 0)
    a_scr[...] = jnp.where((rid >= halo) | (i > 0), a_ext, 0.0)
    cw = cw_ref[...]
    conv = cb_ref[...]
    for j in range(FFN_CONV):
        off = halo - (FFN_CONV - 1) + j
        conv = conv + cw[j:j + 1, :] * a_scr[off:off + tm, :]
    g = _dot(h_scr[halo:halo + tm, :], wg_ref[...])
    act = (_silu(conv) * g).astype(BF16)
    acc[...] += _dot(act, wd_ref[...])

    @pl.when(f == pl.num_programs(1) - 1)
    def _():
        z = ALPHA * x_ref[...] + gate_ref[...] * acc[...]
        o_ref[...] = _ln_rows(z) * lg_ref[...] + lb_ref[...]


def _conv_ffn(x, sc, sh, gate, w_up, conv_w, conv_b, w_down, ln_g, ln_b, tm, tf):
    s, d = x.shape
    dff = w_down.shape[0]
    halo = 16
    nf = dff // tf
    row = lambda i, f: (0, 0)
    return pl.pallas_call(
        functools.partial(_ffn_kernel, halo=halo),
        grid=(s // tm, nf),
        in_specs=[
            pl.BlockSpec((tm, d), lambda i, f: (i, 0)),
            pl.BlockSpec((halo, d), lambda i, f: (jnp.maximum(i * (tm // halo) - 1, 0), 0)),
            pl.BlockSpec((1, d), row),
            pl.BlockSpec((1, d), row),
            pl.BlockSpec((1, d), row),
            pl.BlockSpec((d, tf), lambda i, f: (0, f)),
            pl.BlockSpec((d, tf), lambda i, f: (0, nf + f)),
            pl.BlockSpec((FFN_CONV, tf), lambda i, f: (0, f)),
            pl.BlockSpec((1, tf), lambda i, f: (0, f)),
            pl.BlockSpec((tf, d), lambda i, f: (f, 0)),
            pl.BlockSpec((1, d), row),
            pl.BlockSpec((1, d), row),
        ],
        out_specs=pl.BlockSpec((tm, d), lambda i, f: (i, 0)),
        out_shape=jax.ShapeDtypeStruct((s, d), F32),
        scratch_shapes=[
            pltpu.VMEM((halo + tm, d), BF16),
            pltpu.VMEM((halo + tm, tf), F32),
            pltpu.VMEM((tm, d), F32),
        ],
        compiler_params=_cparams(("parallel", "arbitrary")),
        name="conv_ffn",
    )(x, x, sc, sh, gate, w_up, w_up, conv_w, conv_b, w_down, ln_g, ln_b)


def _token_mixer(x, sc, sh, gate, w_in, cmp_pe, cmp_w1, cmp_w2, ml_conv_w, ml_conv_b, ml_gate_b,
                 ml_norm_g, w_br_nsa, w_br_ml, w_o, ln_g, ln_b):
    s, d = x.shape
    wb, wf = _split_w_in(w_in)
    projb, projf = _inproj(x, sc, sh, wb, wf, min(INPROJ_TM, s), INPROJ_TN)
    kcv = _compress(projf, cmp_pe, cmp_w1, cmp_w2)
    y_nsa = _nsa_attention(projb, projf, kcv)
    ifo = _F_SMALL + 2 * 128
    gates_col = projf[:, ifo:ifo + _ML_IF] + ml_gate_b[None, :]
    y_ml = _mlstm(projb, projf, gates_col, gates_col.T, ml_conv_w, ml_conv_b[None, :],
                  ml_norm_g[None, :], cpb=min(MLSTM_CHUNKS_PER_STEP, s // ML_CHUNK))
    return _merge_outproj(y_nsa, y_ml, projf, x, w_br_nsa.astype(BF16), w_br_ml.astype(BF16),
                          w_o.astype(BF16), gate, ln_g, ln_b, tm=min(MERGE_TM, s))


def _forward(x, c, w_ada, b_ada, w_in, cmp_pe, cmp_w1, cmp_w2, ml_conv_w, ml_conv_b, ml_gate_b,
             ml_norm_g, w_br_nsa, w_br_ml, w_o, w_up, ffn_conv_w, ffn_conv_b, w_down, ln_g, ln_b):
    b, s, d = x.shape
    assert b == 1 and d == D_MODEL
    depth = w_ada.shape[0]
    mod = _modulation(c, w_ada, b_ada)
    xs = x[0]
    for l in range(depth):
        sh1, sc1, g1, sh2, sc2, g2 = [mod[l, :, k * d:(k + 1) * d] for k in range(6)]
        xs = _token_mixer(xs, sc1, sh1, g1, w_in[l], cmp_pe[l], cmp_w1[l], cmp_w2[l], ml_conv_w[l],
                          ml_conv_b[l], ml_gate_b[l], ml_norm_g[l], w_br_nsa[l], w_br_ml[l], w_o[l],
                          ln_g[l, 0][None, :], ln_b[l, 0][None, :])
        xs = _conv_ffn(xs, sc2, sh2, g2, w_up[l].astype(BF16), ffn_conv_w[l], ffn_conv_b[l][None, :],
                       w_down[l].astype(BF16), ln_g[l, 1][None, :], ln_b[l, 1][None, :],
                       tm=min(FFN_TM, s), tf=FFN_TF)
    return xs[None]


def kernel(x, c, w_ada, b_ada, w_in, cmp_pe, cmp_w1, cmp_w2, ml_conv_w, ml_conv_b, ml_gate_b, ml_norm_g, w_br_nsa, w_br_ml, w_o, w_up, ffn_conv_w, ffn_conv_b, w_down, ln_g, ln_b):
    return _forward(x, c, w_ada, b_ada, w_in, cmp_pe, cmp_w1, cmp_w2, ml_conv_w, ml_conv_b,
                    ml_gate_b, ml_norm_g, w_br_nsa, w_br_ml, w_o, w_up, ffn_conv_w, ffn_conv_b,
                    w_down, ln_g, ln_b)
```

```python
import functools
import math

import jax
import jax.numpy as jnp
from jax import lax
from jax.experimental import pallas as pl
from jax.experimental.pallas import tpu as pltpu

F32 = jnp.float32
BF16 = jnp.bfloat16

D_MODEL = 2048
DEPTH = 2
NSA_HEADS = 8
NSA_GROUPS = 2
NSA_HPG = NSA_HEADS // NSA_GROUPS
NSA_HD = 128
CMP_LEN = 32
CMP_STRIDE = 16
CMP_HID = 256
SLC_LEN = 64
SLC_TOPK = 16
WINDOW = 512
Q_BLOCK = 128
ML_HEADS = 4
ML_DQK = 128
ML_DV = 256
ML_CHUNK = 64
ML_CONV = 4
D_FF = 5632
FFN_CONV = 3
ALPHA = (2 * DEPTH) ** 0.25
LN_EPS = 1e-5
NEG_INF = -1e30

V7X_VMEM_BYTES = 64 * 1024 * 1024
VMEM_LIMIT = V7X_VMEM_BYTES - 8 * 1024 * 1024

INPROJ_TM, INPROJ_TN = 1024, 1024
MERGE_TM = 256
FFN_TM, FFN_TF = 512, 512
MLSTM_CHUNKS_PER_STEP = 4

_NSA_Q = NSA_HEADS * NSA_HD
_NSA_KV = 3 * 2 * NSA_GROUPS * NSA_HD
_NSA_G = 3 * NSA_HEADS
_ML_QK = 2 * ML_HEADS * ML_DQK
_ML_V = ML_HEADS * ML_DV
_ML_IF = 2 * ML_HEADS
_CMP_COLS = 2 * NSA_GROUPS * NSA_HD
_F_MERGE = 0
_F_CMP = 2 * D_MODEL
_F_MLQK = _F_CMP + _CMP_COLS
_F_MLO = _F_MLQK + _ML_QK
_F_SMALL = _F_MLO + _ML_V
_F_COLS = _F_SMALL + 4 * 128
_B_Q = 0
_B_KV = _NSA_Q
_B_MLV = _B_KV + 8 * NSA_HD
_B_COLS = _B_MLV + _ML_V

SEL_TILE = 512
SEL_STEP = 4096
SEL_BIG = 2.0 ** 60
NSA_Q_BLOCKS_PER_STEP = 2
SEL_EXP2_GUARD = 64.0


def _cparams(sem):
    return pltpu.CompilerParams(dimension_semantics=sem, vmem_limit_bytes=VMEM_LIMIT)


def _ln_rows(x):
    mu = jnp.mean(x, axis=-1, keepdims=True)
    xc = x - mu
    var = jnp.mean(xc * xc, axis=-1, keepdims=True)
    return xc * lax.rsqrt(var + LN_EPS)


def _sigmoid(x):
    return 1.0 / (1.0 + jnp.exp(-x))


def _silu(x):
    return x * _sigmoid(x)


def _gelu_tanh(x):
    c = math.sqrt(2.0 / math.pi)
    return x * (0.5 * (1.0 + jnp.tanh(c * (x + 0.044715 * (x * x * x)))))


def _log_sigmoid(x):
    return jnp.minimum(x, 0.0) - jnp.log1p(jnp.exp(-jnp.abs(x)))


def _dot(a, b):
    return jnp.dot(a, b, preferred_element_type=F32)


def _dot_nt(a, b):
    return lax.dot_general(a, b, (((1,), (1,)), ((), ())), preferred_element_type=F32)


def _masked_softmax(raw, mask, scale):
    raw = jnp.where(mask, raw, NEG_INF)
    m = jnp.max(raw, axis=-1, keepdims=True)
    e = jnp.exp2((raw - m) * (scale * math.log2(math.e)))
    den = jnp.sum(e, axis=-1, keepdims=True)
    return e * jnp.where(m > 0.5 * NEG_INF, 1.0 / den, 0.0)


def _mod_kernel(c_ref, w_ref, b_ref, o_ref):
    ca = _silu(c_ref[...])
    o = _dot(ca.astype(BF16), w_ref[0].astype(BF16))
    o_ref[0] = o[0:1] + b_ref[0]


def _modulation(c, w_ada, b_ada):
    depth, d, n = w_ada.shape
    tn = INPROJ_TN
    c8 = jnp.broadcast_to(c, (8, d))
    return pl.pallas_call(
        _mod_kernel,
        grid=(depth, n // tn),
        in_specs=[
            pl.BlockSpec((8, d), lambda l, j: (0, 0)),
            pl.BlockSpec((1, d, tn), lambda l, j: (l, 0, j)),
            pl.BlockSpec((1, 1, tn), lambda l, j: (l, 0, j)),
        ],
        out_specs=pl.BlockSpec((1, 1, tn), lambda l, j: (l, 0, j)),
        out_shape=jax.ShapeDtypeStruct((depth, 1, n), F32),
        compiler_params=_cparams(("parallel", "parallel")),
        name="adaln_mod",
    )(c8, w_ada, b_ada.reshape(depth, 1, n))


def _inproj_kernel(x_ref, sc_ref, sh_ref, wb_ref, wf_ref, ob_ref, of_ref, h_scr, *, nb):
    j = pl.program_id(1)

    @pl.when(j == 0)
    def _():
        h = _ln_rows(x_ref[...]) * (1.0 + sc_ref[...]) + sh_ref[...]
        h_scr[...] = h.astype(BF16)

    @pl.when(j < nb)
    def _():
        ob_ref[...] = _dot(h_scr[...], wb_ref[...]).astype(ob_ref.dtype)

    @pl.when(j >= nb)
    def _():
        of_ref[...] = _dot(h_scr[...], wf_ref[...])


def _inproj(x, sc, sh, wb, wf, tm, tn):
    s, d = x.shape
    nb, nf = wb.shape[1] // tn, wf.shape[1] // tn
    bcol = lambda i, j: (i, jnp.minimum(j, nb - 1))
    fcol = lambda i, j: (i, jnp.maximum(j - nb, 0))
    return pl.pallas_call(
        functools.partial(_inproj_kernel, nb=nb),
        grid=(s // tm, nb + nf),
        in_specs=[
            pl.BlockSpec((tm, d), lambda i, j: (i, 0)),
            pl.BlockSpec((1, d), lambda i, j: (0, 0)),
            pl.BlockSpec((1, d), lambda i, j: (0, 0)),
            pl.BlockSpec((d, tn), lambda i, j: (0, jnp.minimum(j, nb - 1))),
            pl.BlockSpec((d, tn), lambda i, j: (0, jnp.maximum(j - nb, 0))),
        ],
        out_specs=[pl.BlockSpec((tm, tn), bcol), pl.BlockSpec((tm, tn), fcol)],
        out_shape=[jax.ShapeDtypeStruct((s, wb.shape[1]), BF16), jax.ShapeDtypeStruct((s, wf.shape[1]), F32)],
        scratch_shapes=[pltpu.VMEM((tm, d), BF16)],
        compiler_params=_cparams(("parallel", "arbitrary")),
        name="in_proj",
    )(x, sc, sh, wb, wf)


def _split_w_in(w):
    d = w.shape[0]
    o = 0
    q = w[:, o:o + _NSA_Q]; o += _NSA_Q
    kv = w[:, o:o + _NSA_KV]; o += _NSA_KV
    g = w[:, o:o + _NSA_G]; o += _NSA_G
    mlqk = w[:, o:o + _ML_QK]; o += _ML_QK
    mlv = w[:, o:o + _ML_V]; o += _ML_V
    mlif = w[:, o:o + _ML_IF]; o += _ML_IF
    mlo = w[:, o:o + _ML_V]; o += _ML_V
    merge = w[:, o:o + 2 * D_MODEL]
    per_g = 3 * NSA_HPG
    z = lambda n: jnp.zeros((d, n), w.dtype)
    wb = jnp.concatenate([q, kv[:, _CMP_COLS:], mlv], axis=1)
    wf = jnp.concatenate(
        [merge, kv[:, :_CMP_COLS], mlqk, mlo,
         g[:, :per_g], z(128 - per_g), g[:, per_g:], z(128 - per_g), mlif, z(128 - _ML_IF), z(128)],
        axis=1)
    return wb.astype(BF16), wf.astype(BF16)


def _compress_kernel(a_ref, pe_ref, w1_ref, w2_ref, o_ref):
    n = o_ref.shape[1]
    hd = NSA_HD

    def half_sum(lo):
        acc = jnp.zeros((n, CMP_HID), F32)
        for l in range(0, CMP_STRIDE, 2):
            x = jnp.concatenate(
                [a_ref[pl.ds(l + u, n, stride=CMP_STRIDE), :] + pe_ref[0, lo + l + u:lo + l + u + 1, :]
                 for u in range(2)], axis=1)
            w = w1_ref[0, (lo + l) * hd:(lo + l + 2) * hd, :]
            acc = acc + _dot(x.astype(BF16), w.astype(BF16))
        return acc

    pre = half_sum(0) + pltpu.roll(half_sum(CMP_STRIDE), n - 1, 0)
    g = _gelu_tanh(pre)
    o_ref[0] = _dot(g.astype(BF16), w2_ref[0].astype(BF16)).astype(o_ref.dtype)


def _compress(projf, pe, w1, w2):
    s = projf.shape[0]
    n = s // CMP_STRIDE
    four = 2 * NSA_GROUPS
    return pl.pallas_call(
        _compress_kernel,
        grid=(four,),
        in_specs=[
            pl.BlockSpec((s, NSA_HD), lambda j: (0, _F_CMP // NSA_HD + j)),
            pl.BlockSpec((1, CMP_LEN, NSA_HD), lambda j: (j // NSA_GROUPS, 0, 0)),
            pl.BlockSpec((1, CMP_LEN * NSA_HD, CMP_HID), lambda j: (j // NSA_GROUPS, 0, 0)),
            pl.BlockSpec((1, CMP_HID, NSA_HD), lambda j: (j // NSA_GROUPS, 0, 0)),
        ],
        out_specs=pl.BlockSpec((1, n, NSA_HD), lambda j: (j, 0, 0)),
        out_shape=jax.ShapeDtypeStruct((four, n, NSA_HD), BF16),
        compiler_params=_cparams(("parallel",)),
        name="nsa_compress",
    )(projf, pe, w1, w2)


def _nsa_kernel(q_ref, kc_ref, vc_ref, ks_ref, vs_ref, kw_ref, vw_ref, gate_ref, ovt_ref, eye_ref,
                o_ref, nsel_scr, npast_scr, *, step, nq):
    pid = pl.program_id(1)
    rows = NSA_HPG * Q_BLOCK
    scale = NSA_HD ** -0.5
    c2 = scale * math.log2(math.e)
    row_q = lax.broadcasted_iota(jnp.int32, (rows, 1), 0) & (Q_BLOCK - 1)
    lane_q = lax.broadcasted_iota(jnp.int32, (1, Q_BLOCK), 1)
    blk = []
    for u in range(nq):
        i = pid * nq + u
        qt = q_ref[u * Q_BLOCK:(u + 1) * Q_BLOCK, :]
        qs = jnp.concatenate([qt[:, h * NSA_HD:(h + 1) * NSA_HD] for h in range(NSA_HPG)], axis=0)
        blk.append((i, qs, i * Q_BLOCK + row_q))
    i_last = blk[-1][0]

    ncp = kc_ref.shape[1]

    def cmp_branch(width):
        outs = []
        cidx = lax.broadcasted_iota(jnp.int32, (rows, width), 1)
        ovt = ovt_ref[:, :width]
        for _, qs, t in blk:
            last_c = (t - (CMP_LEN - 1)) // CMP_STRIDE
            p_c = _masked_softmax(_dot_nt(qs, kc_ref[0, :width, :]), cidx <= last_c, scale)
            o = _dot(p_c.astype(BF16), vc_ref[0, :width, :])
            psum = (p_c[0:Q_BLOCK] + p_c[Q_BLOCK:2 * Q_BLOCK]
                    + p_c[2 * Q_BLOCK:3 * Q_BLOCK] + p_c[3 * Q_BLOCK:4 * Q_BLOCK])
            outs += [o, _dot_nt(ovt, psum.astype(BF16))]
        return tuple(outs)

    n_vis = (i_last * Q_BLOCK + Q_BLOCK - CMP_LEN) // CMP_STRIDE + 1
    branch = lambda: cmp_branch(ncp)
    for width in (ncp // 2, ncp // 4):
        if width % 128 == 0:
            branch = (lambda w, other: lambda: lax.cond(n_vis <= w, lambda: cmp_branch(w), other))(width, branch)
    cmp_out = branch()
    o_cs, imps = cmp_out[0::2], cmp_out[1::2]

    wsl = imps[0].shape[0]
    bid = lax.broadcasted_iota(jnp.int32, (wsl, Q_BLOCK), 0)
    bidf = bid.astype(F32)
    taken = -1e9
    nsel0 = jnp.full((wsl, Q_BLOCK), -SEL_BIG, F32)
    score0s = []
    for (i, _, _), imp in zip(blk, imps):
        t_row = i * Q_BLOCK + lane_q
        cur = t_row // SLC_LEN
        score0s.append(jnp.where(bid == 0, 3e6, jnp.where(bid == cur, 2e6, jnp.where(
            bid == cur - 1, 1e6, jnp.where(bid * SLC_LEN <= t_row, imp, -1.0 - bidf)))))

    def take_max(c, lanes=None):
        sc, ns = c
        hit = sc == jnp.max(sc, axis=0, keepdims=True)
        if lanes is not None:
            hit = hit & lanes
        return jnp.where(hit, taken, sc), jnp.where(hit, 0.0, ns)

    cs = [(jnp.where(s0 >= 1e6, taken, s0), jnp.where(s0 >= 1e6, 0.0, nsel0)) for s0 in score0s]
    n_min = SLC_TOPK - 3
    for _ in range(n_min):
        cs = [take_max(c) for c in cs]

    def early_rounds():
        cur = lane_q // SLC_LEN
        n_free = SLC_TOPK - (1 + jnp.where(cur >= 1, 1, 0) + jnp.where(cur >= 2, 1, 0))
        e = cs[0]
        for r in range(n_min + 1, SLC_TOPK):
            e = take_max(e, n_free >= r)
        return e

    cs[0] = lax.cond(pid == 0, early_rounds, lambda: cs[0])
    nsels = tuple(c[1] for c in cs)

    def topk_ties():
        def one(_, c):
            sc, ns = c
            mx = jnp.max(sc, axis=0, keepdims=True)
            first = jnp.min(jnp.where(sc == mx, bidf, float(wsl)), axis=0, keepdims=True)
            hit = bidf == first
            return jnp.where(hit, taken, sc), jnp.where(hit, 0.0, ns)
        return tuple(lax.fori_loop(0, SLC_TOPK, one, (s0, nsel0))[1] for s0 in score0s)

    n_taken = functools.reduce(jnp.maximum, [jnp.sum(jnp.where(ns == 0.0, 1.0, 0.0), axis=0, keepdims=True)
                                             for ns in nsels])
    nsels = lax.cond(jnp.max(n_taken) > SLC_TOPK, topk_ties, lambda: nsels)
    for u, ((i, _, _), ns) in enumerate(zip(blk, nsels)):
        blk_d = (i * Q_BLOCK) // SLC_LEN
        nsel_scr[u] = ns
        npast_scr[u] = jnp.where(bid < blk_d, ns, -SEL_BIG)

    def keys_aug(k0, width, sel_scr):
        b0 = k0 // SLC_LEN
        bias = jnp.concatenate(
            [jnp.broadcast_to(sel_scr[pl.ds(b0 + b, 1), :], (SLC_LEN, Q_BLOCK))
             for b in range(width // SLC_LEN)], axis=0)
        return jnp.concatenate([ks_ref[pl.ds(k0, width), :], bias.astype(BF16)], axis=1)

    def vals_aug(k0, width):
        ones_col = (lax.broadcasted_iota(jnp.int32, (width, NSA_HD), 1) == 0).astype(BF16)
        return jnp.concatenate([vs_ref[pl.ds(k0, width), :], ones_col], axis=1)

    qas, diag, o_ws = [], [], []
    wlen = WINDOW + Q_BLOCK
    for u, (i, qs, t) in enumerate(blk):
        qa = jnp.concatenate([qs, eye_ref[...]], axis=1)
        k_d = pl.multiple_of(i * Q_BLOCK, Q_BLOCK)
        s_d = _dot_nt(qa, keys_aug(k_d, Q_BLOCK, nsel_scr.at[u]))
        s_d = jnp.where(k_d + lane_q <= t, s_d, NEG_INF)
        m_d = jnp.max(s_d, axis=-1, keepdims=True)
        acc_d = _dot(jnp.exp2((s_d - m_d) * c2).astype(BF16), vals_aug(k_d, Q_BLOCK))
        qas.append(qa)
        diag.append((m_d, acc_d))
        ks0 = pl.multiple_of(jnp.maximum(i - WINDOW // Q_BLOCK, 0) * Q_BLOCK, Q_BLOCK)
        s_w = _dot_nt(qs, kw_ref[pl.ds(ks0, wlen), :])
        kpos = ks0 + lax.broadcasted_iota(jnp.int32, (1, wlen), 1)
        p_w = _masked_softmax(s_w, (kpos <= t) & (kpos > t - WINDOW), scale)
        o_ws.append(_dot(p_w.astype(BF16), vw_ref[pl.ds(ks0, wlen), :]))

    gates = _sigmoid(gate_ref[...])
    for u, (i, qs, t) in enumerate(blk):
        qa = qas[u]
        past_scr = npast_scr.at[u]
        m_d, acc_d = diag[u]

        def exact_step(k0, carry):
            m, acc = carry
            k0 = pl.multiple_of(k0, SEL_TILE)
            s = _dot_nt(qa, keys_aug(k0, SEL_TILE, past_scr))
            m_new = jnp.maximum(m, jnp.max(s, axis=-1, keepdims=True))
            p = jnp.exp2((s - m_new) * c2)
            return m_new, jnp.exp2((m - m_new) * c2) * acc + _dot(p.astype(BF16), vals_aug(k0, SEL_TILE))

        def fast_step(k0, width, carry):
            m, acc, risk = carry
            k0 = pl.multiple_of(k0, SEL_TILE)
            s = _dot_nt(qa, keys_aug(k0, width, past_scr))
            p = jnp.exp2((s - m) * c2)
            mx = jnp.max(s, axis=-1, keepdims=True)
            m_new = jnp.maximum(m, mx)
            acc = (acc + _dot(p.astype(BF16), vals_aug(k0, width))) * jnp.exp2((m - m_new) * c2)
            return m_new, acc, jnp.maximum(risk, mx - m)

        n_past = i * Q_BLOCK
        n_steps = n_past // step
        carry = (m_d, acc_d, jnp.zeros((rows, 1), F32))
        carry = lax.fori_loop(0, n_steps, lambda j, c: fast_step(j * step, step, c), carry)
        rem = n_past - n_steps * step
        k_r = n_steps * step
        carry = lax.cond(
            rem > step // 2, lambda: fast_step(k_r, step, carry),
            lambda: lax.cond(rem > step // 4, lambda: fast_step(k_r, step // 2, carry),
                             lambda: lax.cond(rem > 0, lambda: fast_step(k_r, step // 4, carry),
                                              lambda: carry)))
        _, acc_s, risk = carry

        def redo_exact():
            n_tiles = (n_past + SEL_TILE - 1) // SEL_TILE
            return lax.fori_loop(0, n_tiles, lambda j, c: exact_step(j * SEL_TILE, c), (m_d, acc_d))[1]

        acc_s = lax.cond(jnp.max(risk) * c2 > SEL_EXP2_GUARD, redo_exact, lambda: acc_s)
        o_s = acc_s[:, :NSA_HD] / acc_s[:, NSA_HD:NSA_HD + 1]

        g_u = gates[u * Q_BLOCK:(u + 1) * Q_BLOCK]
        for h in range(NSA_HPG):
            r = slice(h * Q_BLOCK, (h + 1) * Q_BLOCK)
            out = (g_u[:, 3 * h:3 * h + 1] * o_cs[u][r] + g_u[:, 3 * h + 1:3 * h + 2] * o_s[r]
                   + g_u[:, 3 * h + 2:3 * h + 3] * o_ws[u][r])
            o_ref[u * Q_BLOCK:(u + 1) * Q_BLOCK, h * NSA_HD:(h + 1) * NSA_HD] = out.astype(o_ref.dtype)


def _nsa_attention(projb, projf, kcv, single_buffer=False):
    s = projb.shape[0]
    n_slc = s // SLC_LEN
    step = min(SEL_STEP, s)
    nq = NSA_Q_BLOCKS_PER_STEP
    qrows = nq * Q_BLOCK
    assert n_slc >= SLC_TOPK and s >= WINDOW + Q_BLOCK and s % step == 0 and step % (4 * SEL_TILE) == 0
    ncp = kcv.shape[1]
    wsl = max(n_slc, 128)
    hd = NSA_HD
    c_start = jnp.arange(ncp)[None, :] * CMP_STRIDE
    s_start = jnp.arange(wsl)[:, None] * SLC_LEN
    ovt = ((c_start < s_start + SLC_LEN) & (c_start + CMP_LEN > s_start)
           & (jnp.arange(wsl)[:, None] < n_slc)).astype(BF16)
    eye = jnp.tile(jnp.eye(Q_BLOCK, dtype=BF16), (NSA_HPG, 1))
    resident = dict(pipeline_mode=pl.Buffered(1)) if single_buffer else {}
    kvb = _B_KV // hd
    gb = _F_SMALL // 128
    return pl.pallas_call(
        functools.partial(_nsa_kernel, step=step, nq=nq),
        grid=(NSA_GROUPS, s // qrows),
        in_specs=[
            pl.BlockSpec((qrows, NSA_HPG * hd), lambda g, i: (i, g)),
            pl.BlockSpec((1, ncp, hd), lambda g, i: (g, 0, 0)),
            pl.BlockSpec((1, ncp, hd), lambda g, i: (NSA_GROUPS + g, 0, 0)),
            pl.BlockSpec((s, hd), lambda g, i: (0, kvb + g), **resident),
            pl.BlockSpec((s, hd), lambda g, i: (0, kvb + 2 + g), **resident),
            pl.BlockSpec((s, hd), lambda g, i: (0, kvb + 4 + g), **resident),
            pl.BlockSpec((s, hd), lambda g, i: (0, kvb + 6 + g), **resident),
            pl.BlockSpec((qrows, 128), lambda g, i: (i, gb + g)),
            pl.BlockSpec((wsl, ncp), lambda g, i: (0, 0)),
            pl.BlockSpec((NSA_HPG * Q_BLOCK, Q_BLOCK), lambda g, i: (0, 0)),
        ],
        out_specs=pl.BlockSpec((qrows, NSA_HPG * hd), lambda g, i: (i, g)),
        out_shape=jax.ShapeDtypeStruct((s, NSA_HEADS * hd), BF16),
        scratch_shapes=[pltpu.VMEM((nq, wsl, Q_BLOCK), F32), pltpu.VMEM((nq, wsl, Q_BLOCK), F32)],
        compiler_params=_cparams(("parallel", "arbitrary")),
        name="nsa_attention",
    )(projb, kcv, kcv, projb, projb, projb, projb, projf, ovt, eye)


def _mlstm_kernel(q_ref, k_ref, v_ref, o0_ref, o1_ref, gc_ref, gr_ref, cw_ref, cb_ref, ng_ref,
                  y_ref, extq, extk, c_st, n_st, m_st, *, cpb):
    step = pl.program_id(0)
    rows = cpb * ML_CHUNK
    half = ML_HEADS * ML_DQK
    L = ML_CHUNK

    @pl.when(step == 0)
    def _():
        extq[0:8, :] = jnp.zeros((8, half), F32)
        extk[0:8, :] = jnp.zeros((8, half), F32)
        c_st[...] = jnp.zeros_like(c_st)
        n_st[...] = jnp.zeros_like(n_st)
        m_st[...] = jnp.zeros_like(m_st)

    extq[8:8 + rows, :] = q_ref[...]
    extk[8:8 + rows, :] = k_ref[...]
    cw = cw_ref[...]
    cb = cb_ref[...]

    def conv(ext, lo):
        acc = cb[:, lo:lo + half]
        for j in range(ML_CONV):
            off = 8 - (ML_CONV - 1) + j
            acc = acc + cw[j:j + 1, lo:lo + half] * ext[off:off + rows, :]
        return _silu(acc)

    qa = conv(extq, 0)
    ka = conv(extk, half) * (ML_DQK ** -0.5)
    extq[0:8, :] = extq[rows:rows + 8, :]
    extk[0:8, :] = extk[rows:rows + 8, :]

    gcol = gc_ref[...]
    grow = gr_ref[...]
    lf_col_all = _log_sigmoid(gcol[:, ML_HEADS:])
    lf_row_all = _log_sigmoid(grow[ML_HEADS:, :])
    ri = lax.broadcasted_iota(jnp.int32, (L, L), 0)
    ci = lax.broadcasted_iota(jnp.int32, (L, L), 1)
    tri = ri >= ci
    ng = ng_ref[...]

    state = [(c_st[h], n_st[h], m_st[h][:, 0:1]) for h in range(ML_HEADS)]
    ones_blk = jnp.ones((L, 128), BF16)
    for c in range(cpb):
        r0 = c * L
        kt_c = ka[r0:r0 + L, :].T
        for h in range(ML_HEADS):
            qh = qa[r0:r0 + L, h * ML_DQK:(h + 1) * ML_DQK]
            kh = ka[r0:r0 + L, h * ML_DQK:(h + 1) * ML_DQK]
            vh = v_ref[r0:r0 + L, h * ML_DV:(h + 1) * ML_DV]
            ig_col = gcol[r0:r0 + L, h:h + 1]
            ig_row = grow[h:h + 1, r0:r0 + L]
            lf_col = lf_col_all[r0:r0 + L, h:h + 1]
            lf_row = lf_row_all[h:h + 1, r0:r0 + L]
            bcum_col = jnp.sum(jnp.where(tri, lf_row, 0.0), axis=1, keepdims=True)
            bcum_row = jnp.sum(jnp.where(ri <= ci, lf_col, 0.0), axis=0, keepdims=True)
            b_last = jnp.sum(lf_row, axis=1, keepdims=True)
            dmat = jnp.where(tri, bcum_col - bcum_row + ig_row, NEG_INF)
            a_loc = jnp.max(dmat, axis=-1, keepdims=True)
            qb = qh.astype(BF16)
            s_loc = _dot_nt(qb, kh.astype(BF16)) * jnp.exp(dmat - a_loc)
            sv = _dot(s_loc.astype(BF16), vh)
            s_sum = _dot(s_loc.astype(BF16), ones_blk)[:, 0:1]
            a_col = b_last - bcum_col + ig_col
            a_row = b_last - bcum_row + ig_row
            a_max = jnp.max(a_row, axis=-1, keepdims=True)
            k_sum = jnp.sum(kh * jnp.exp(a_col - a_max), axis=0, keepdims=True)
            kwt = kt_c[h * ML_DQK:(h + 1) * ML_DQK, :] * jnp.exp(a_row - a_max)
            kv = _dot(kwt.astype(BF16), vh)

            ct, nrow, m_old = state[h]
            inter = bcum_col + m_old
            m_t = jnp.maximum(inter, a_loc)
            w_inter = jnp.exp(inter - m_t)
            w_loc = jnp.exp(a_loc - m_t)
            num = w_inter * _dot(qb, ct.astype(BF16)) + w_loc * sv
            qn = _dot_nt(qb, jnp.broadcast_to(nrow, (8, ML_DQK)).astype(BF16))[:, 0:1]
            den = w_inter * qn + w_loc * s_sum
            hout = num / jnp.maximum(jnp.abs(den), jnp.exp(-m_t))

            m_new = jnp.maximum(b_last + m_old, a_max)
            decay = jnp.exp(b_last + m_old - m_new)
            g_new = jnp.exp(a_max - m_new)
            state[h] = (decay * ct + g_new * kv, decay * nrow + g_new * k_sum, m_new)

            o_ref = o0_ref if h < ML_HEADS // 2 else o1_ref
            oc = (h % (ML_HEADS // 2)) * ML_DV
            og = _sigmoid(o_ref[r0:r0 + L, oc:oc + ML_DV])
            yn = _ln_rows(hout) * ng[:, h * ML_DV:(h + 1) * ML_DV]
            y_ref[r0:r0 + L, h * ML_DV:(h + 1) * ML_DV] = (yn * og).astype(y_ref.dtype)

    for h in range(ML_HEADS):
        c_st[h], n_st[h] = state[h][0], state[h][1]
        m_st[h] = jnp.broadcast_to(state[h][2], (1, 128))


def _mlstm(projb, projf, gates_col, gates_row, conv_w, conv_b, norm_g, cpb):
    s = projb.shape[0]
    rows = cpb * ML_CHUNK
    half = ML_HEADS * ML_DQK
    dv = ML_HEADS * ML_DV
    qblk = _F_MLQK // half
    oblk = _F_MLO // half
    return pl.pallas_call(
        functools.partial(_mlstm_kernel, cpb=cpb),
        grid=(s // rows,),
        in_specs=[
            pl.BlockSpec((rows, half), lambda j: (j, qblk)),
            pl.BlockSpec((rows, half), lambda j: (j, qblk + 1)),
            pl.BlockSpec((rows, dv), lambda j: (j, _B_MLV // dv)),
            pl.BlockSpec((rows, half), lambda j: (j, oblk)),
            pl.BlockSpec((rows, half), lambda j: (j, oblk + 1)),
            pl.BlockSpec((rows, 2 * ML_HEADS), lambda j: (j, 0)),
            pl.BlockSpec((2 * ML_HEADS, rows), lambda j: (0, j)),
            pl.BlockSpec((ML_CONV, 2 * half), lambda j: (0, 0)),
            pl.BlockSpec((1, 2 * half), lambda j: (0, 0)),
            pl.BlockSpec((1, dv), lambda j: (0, 0)),
        ],
        out_specs=pl.BlockSpec((rows, dv), lambda j: (j, 0)),
        out_shape=jax.ShapeDtypeStruct((s, dv), BF16),
        scratch_shapes=[
            pltpu.VMEM((rows + 8, half), F32),
            pltpu.VMEM((rows + 8, half), F32),
            pltpu.VMEM((ML_HEADS, ML_DQK, ML_DV), F32),
            pltpu.VMEM((ML_HEADS, 1, ML_DQK), F32),
            pltpu.VMEM((ML_HEADS, 1, 128), F32),
        ],
        compiler_params=_cparams(("arbitrary",)),
        name="mlstm",
    )(projf, projf, projb, projf, projf, gates_col, gates_row, conv_w, conv_b, norm_g)


def _merge_kernel(yn_ref, ym_ref, g0_ref, g1_ref, x_ref, wn_ref, wm_ref, wo_ref, gate_ref,
                  lg_ref, lb_ref, o_ref):
    a = _dot(yn_ref[...], wn_ref[...])
    b = _dot(ym_ref[...], wm_ref[...])
    merged = _sigmoid(g0_ref[...]) * a + _sigmoid(g1_ref[...]) * b
    y = _dot(merged.astype(BF16), wo_ref[...])
    z = ALPHA * x_ref[...] + gate_ref[...] * y
    o_ref[...] = _ln_rows(z) * lg_ref[...] + lb_ref[...]


def _merge_outproj(y_nsa, y_ml, projf, x, wn, wm, wo, gate, ln_g, ln_b, tm, single_buffer=True):
    s, d = x.shape
    resident = dict(pipeline_mode=pl.Buffered(1)) if single_buffer else {}
    mb = _F_MERGE // d
    row = lambda i: (0, 0)
    return pl.pallas_call(
        _merge_kernel,
        grid=(s // tm,),
        in_specs=[
            pl.BlockSpec((tm, y_nsa.shape[1]), lambda i: (i, 0)),
            pl.BlockSpec((tm, y_ml.shape[1]), lambda i: (i, 0)),
            pl.BlockSpec((tm, d), lambda i: (i, mb)),
            pl.BlockSpec((tm, d), lambda i: (i, mb + 1)),
            pl.BlockSpec((tm, d), lambda i: (i, 0)),
            pl.BlockSpec(wn.shape, row, **resident),
            pl.BlockSpec(wm.shape, row, **resident),
            pl.BlockSpec(wo.shape, row, **resident),
            pl.BlockSpec((1, d), row),
            pl.BlockSpec((1, d), row),
            pl.BlockSpec((1, d), row),
        ],
        out_specs=pl.BlockSpec((tm, d), lambda i: (i, 0)),
        out_shape=jax.ShapeDtypeStruct((s, d), F32),
        compiler_params=_cparams(("parallel",)),
        name="merge_outproj",
    )(y_nsa, y_ml, projf, projf, x, wn, wm, wo, gate, ln_g, ln_b)


def _ffn_kernel(x_ref, xh_ref, sc_ref, sh_ref, gate_ref, wa_ref, wg_ref, cw_ref, cb_ref, wd_ref,
                lg_ref, lb_ref, o_ref, h_scr, a_scr, acc, *, halo):
    i = pl.program_id(0)
    f = pl.program_id(1)
    tm = x_ref.shape[0]

    @pl.when(f == 0)
    def _():
        mod = lambda v: (_ln_rows(v) * (1.0 + sc_ref[...]) + sh_ref[...]).astype(BF16)
        h_scr[0:halo, :] = mod(xh_ref[...])
        h_scr[halo:halo + tm, :] = mod(x_ref[...])
        acc[...] = jnp.zeros_like(acc)

    hx = h_scr[...]
    a_ext = _dot(hx, wa_ref[...])
    rid = lax.broadcasted_iota(jnp.int32, (halo + tm, 1), 0)
    a_scr[...] = jnp.where((rid >= halo) | (i > 0), a_ext, 0.0)
    cw = cw_ref[...]
    conv = cb_ref[...]
    for j in range(FFN_CONV):
        off = halo - (FFN_CONV - 1) + j
        conv = conv + cw[j:j + 1, :] * a_scr[off:off + tm, :]
    g = _dot(h_scr[halo:halo + tm, :], wg_ref[...])
    act = (_silu(conv) * g).astype(BF16)
    acc[...] += _dot(act, wd_ref[...])

    @pl.when(f == pl.num_programs(1) - 1)
    def _():
        z = ALPHA * x_ref[...] + gate_ref[...] * acc[...]
        o_ref[...] = _ln_rows(z) * lg_ref[...] + lb_ref[...]


def _conv_ffn(x, sc, sh, gate, w_up, conv_w, conv_b, w_down, ln_g, ln_b, tm, tf):
    s, d = x.shape
    dff = w_down.shape[0]
    halo = 16
    nf = dff // tf
    row = lambda i, f: (0, 0)
    return pl.pallas_call(
        functools.partial(_ffn_kernel, halo=halo),
        grid=(s // tm, nf),
        in_specs=[
            pl.BlockSpec((tm, d), lambda i, f: (i, 0)),
            pl.BlockSpec((halo, d), lambda i, f: (jnp.maximum(i * (tm // halo) - 1, 0), 0)),
            pl.BlockSpec((1, d), row),
            pl.BlockSpec((1, d), row),
            pl.BlockSpec((1, d), row),
            pl.BlockSpec((d, tf), lambda i, f: (0, f)),
            pl.BlockSpec((d, tf), lambda i, f: (0, nf + f)),
            pl.BlockSpec((FFN_CONV, tf), lambda i, f: (0, f)),
            pl.BlockSpec((1, tf), lambda i, f: (0, f)),
            pl.BlockSpec((tf, d), lambda i, f: (f, 0)),
            pl.BlockSpec((1, d), row),
            pl.BlockSpec((1, d), row),
        ],
        out_specs=pl.BlockSpec((tm, d), lambda i, f: (i, 0)),
        out_shape=jax.ShapeDtypeStruct((s, d), F32),
        scratch_shapes=[
            pltpu.VMEM((halo + tm, d), BF16),
            pltpu.VMEM((halo + tm, tf), F32),
            pltpu.VMEM((tm, d), F32),
        ],
        compiler_params=_cparams(("parallel", "arbitrary")),
        name="conv_ffn",
    )(x, x, sc, sh, gate, w_up, w_up, conv_w, conv_b, w_down, ln_g, ln_b)


def _token_mixer(x, sc, sh, gate, w_in, cmp_pe, cmp_w1, cmp_w2, ml_conv_w, ml_conv_b, ml_gate_b,
                 ml_norm_g, w_br_nsa, w_br_ml, w_o, ln_g, ln_b):
    s, d = x.shape
    wb, wf = _split_w_in(w_in)
    projb, projf = _inproj(x, sc, sh, wb, wf, min(INPROJ_TM, s), INPROJ_TN)
    kcv = _compress(projf, cmp_pe, cmp_w1, cmp_w2)
    y_nsa = _nsa_attention(projb, projf, kcv)
    ifo = _F_SMALL + 2 * 128
    gates_col = projf[:, ifo:ifo + _ML_IF] + ml_gate_b[None, :]
    y_ml = _mlstm(projb, projf, gates_col, gates_col.T, ml_conv_w, ml_conv_b[None, :],
                  ml_norm_g[None, :], cpb=min(MLSTM_CHUNKS_PER_STEP, s // ML_CHUNK))
    return _merge_outproj(y_nsa, y_ml, projf, x, w_br_nsa.astype(BF16), w_br_ml.astype(BF16),
                          w_o.astype(BF16), gate, ln_g, ln_b, tm=min(MERGE_TM, s))


def _forward(x, c, w_ada, b_ada, w_in, cmp_pe, cmp_w1, cmp_w2, ml_conv_w, ml_conv_b, ml_gate_b,
             ml_norm_g, w_br_nsa, w_br_ml, w_o, w_up, ffn_conv_w, ffn_conv_b, w_down, ln_g, ln_b):
    b, s, d = x.shape
    assert b == 1 and d == D_MODEL
    depth = w_ada.shape[0]
    mod = _modulation(c, w_ada, b_ada)
    xs = x[0]
    for l in range(depth):
        sh1, sc1, g1, sh2, sc2, g2 = [mod[l, :, k * d:(k + 1) * d] for k in range(6)]
        xs = _token_mixer(xs, sc1, sh1, g1, w_in[l], cmp_pe[l], cmp_w1[l], cmp_w2[l], ml_conv_w[l],
                          ml_conv_b[l], ml_gate_b[l], ml_norm_g[l], w_br_nsa[l], w_br_ml[l], w_o[l],
                          ln_g[l, 0][None, :], ln_b[l, 0][None, :])
        xs = _conv_ffn(xs, sc2, sh2, g2, w_up[l].astype(BF16), ffn_conv_w[l], ffn_conv_b[l][None, :],
                       w_down[l].astype(BF16), ln_g[l, 1][None, :], ln_b[l, 1][None, :],
                       tm=min(FFN_TM, s), tf=FFN_TF)
    return xs[None]


def kernel(x, c, w_ada, b_ada, w_in, cmp_pe, cmp_w1, cmp_w2, ml_conv_w, ml_conv_b, ml_gate_b, ml_norm_g, w_br_nsa, w_br_ml, w_o, w_up, ffn_conv_w, ffn_conv_b, w_down, ln_g, ln_b):
    return _forward(x, c, w_ada, b_ada, w_in, cmp_pe, cmp_w1, cmp_w2, ml_conv_w, ml_conv_b,
                    ml_gate_b, ml_norm_g, w_br_nsa, w_br_ml, w_o, w_up, ffn_conv_w, ffn_conv_b,
                    w_down, ln_g, ln_b)
```

```python
import functools
import math

import jax
import jax.numpy as jnp
from jax import lax
from jax.experimental import pallas as pl
from jax.experimental.pallas import tpu as pltpu

F32 = jnp.float32
BF16 = jnp.bfloat16

D_MODEL = 2048
DEPTH = 2
NSA_HEADS = 8
NSA_GROUPS = 2
NSA_HPG = NSA_HEADS // NSA_GROUPS
NSA_HD = 128
CMP_LEN = 32
CMP_STRIDE = 16
CMP_HID = 256
SLC_LEN = 64
SLC_TOPK = 16
WINDOW = 512
Q_BLOCK = 128
ML_HEADS = 4
ML_DQK = 128
ML_DV = 256
ML_CHUNK = 64
ML_CONV = 4
D_FF = 5632
FFN_CONV = 3
ALPHA = (2 * DEPTH) ** 0.25
LN_EPS = 1e-5
NEG_INF = -1e30

V7X_VMEM_BYTES = 64 * 1024 * 1024
VMEM_LIMIT = V7X_VMEM_BYTES - 8 * 1024 * 1024

INPROJ_TM, INPROJ_TN = 1024, 1024
MERGE_TM = 256
FFN_TM, FFN_TF = 512, 512
MLSTM_CHUNKS_PER_STEP = 4

_NSA_Q = NSA_HEADS * NSA_HD
_NSA_KV = 3 * 2 * NSA_GROUPS * NSA_HD
_NSA_G = 3 * NSA_HEADS
_ML_QK = 2 * ML_HEADS * ML_DQK
_ML_V = ML_HEADS * ML_DV
_ML_IF = 2 * ML_HEADS
_CMP_COLS = 2 * NSA_GROUPS * NSA_HD
_F_MERGE = 0
_F_CMP = 2 * D_MODEL
_F_MLQK = _F_CMP + _CMP_COLS
_F_MLO = _F_MLQK + _ML_QK
_F_SMALL = _F_MLO + _ML_V
_F_COLS = _F_SMALL + 4 * 128
_B_Q = 0
_B_KV = _NSA_Q
_B_MLV = _B_KV + 8 * NSA_HD
_B_COLS = _B_MLV + _ML_V

SEL_TILE = 512
SEL_STEP = 4096
SEL_BIG = 2.0 ** 100
NSA_Q_BLOCKS_PER_STEP = 2
SEL_EXP2_GUARD = 64.0


def _cparams(sem):
    return pltpu.CompilerParams(dimension_semantics=sem, vmem_limit_bytes=VMEM_LIMIT)


def _ln_rows(x):
    mu = jnp.mean(x, axis=-1, keepdims=True)
    xc = x - mu
    var = jnp.mean(xc * xc, axis=-1, keepdims=True)
    return xc * lax.rsqrt(var + LN_EPS)


def _sigmoid(x):
    return 1.0 / (1.0 + jnp.exp(-x))


def _silu(x):
    return x * _sigmoid(x)


def _gelu_tanh(x):
    c = math.sqrt(2.0 / math.pi)
    return x * (0.5 * (1.0 + jnp.tanh(c * (x + 0.044715 * (x * x * x)))))


def _log_sigmoid(x):
    return jnp.minimum(x, 0.0) - jnp.log1p(jnp.exp(-jnp.abs(x)))


def _dot(a, b):
    return jnp.dot(a, b, preferred_element_type=F32)


def _dot_nt(a, b):
    return lax.dot_general(a, b, (((1,), (1,)), ((), ())), preferred_element_type=F32)


def _masked_softmax(raw, mask, scale):
    raw = jnp.where(mask, raw, NEG_INF)
    m = jnp.max(raw, axis=-1, keepdims=True)
    e = jnp.exp2((raw - m) * (scale * math.log2(math.e)))
    den = jnp.sum(e, axis=-1, keepdims=True)
    return e * jnp.where(m > 0.5 * NEG_INF, 1.0 / den, 0.0)


def _mod_kernel(c_ref, w_ref, b_ref, o_ref):
    ca = _silu(c_ref[...])
    o = _dot(ca.astype(BF16), w_ref[0].astype(BF16))
    o_ref[0] = o[0:1] + b_ref[0]


def _modulation(c, w_ada, b_ada):
    depth, d, n = w_ada.shape
    tn = INPROJ_TN
    c8 = jnp.broadcast_to(c, (8, d))
    return pl.pallas_call(
        _mod_kernel,
        grid=(depth, n // tn),
        in_specs=[
            pl.BlockSpec((8, d), lambda l, j: (0, 0)),
            pl.BlockSpec((1, d, tn), lambda l, j: (l, 0, j)),
            pl.BlockSpec((1, 1, tn), lambda l, j: (l, 0, j)),
        ],
        out_specs=pl.BlockSpec((1, 1, tn), lambda l, j: (l, 0, j)),
        out_shape=jax.ShapeDtypeStruct((depth, 1, n), F32),
        compiler_params=_cparams(("parallel", "parallel")),
        name="adaln_mod",
    )(c8, w_ada, b_ada.reshape(depth, 1, n))


def _inproj_kernel(x_ref, sc_ref, sh_ref, w_ref, o_ref, h_scr):
    @pl.when(pl.program_id(1) == 0)
    def _():
        h = _ln_rows(x_ref[...]) * (1.0 + sc_ref[...]) + sh_ref[...]
        h_scr[...] = h.astype(BF16)

    o_ref[...] = _dot(h_scr[...], w_ref[...]).astype(o_ref.dtype)


def _inproj(x, sc, sh, w, out_dtype, tm, tn):
    s, d = x.shape
    n = w.shape[1]
    return pl.pallas_call(
        _inproj_kernel,
        grid=(s // tm, n // tn),
        in_specs=[
            pl.BlockSpec((tm, d), lambda i, j: (i, 0)),
            pl.BlockSpec((1, d), lambda i, j: (0, 0)),
            pl.BlockSpec((1, d), lambda i, j: (0, 0)),
            pl.BlockSpec((d, tn), lambda i, j: (0, j)),
        ],
        out_specs=pl.BlockSpec((tm, tn), lambda i, j: (i, j)),
        out_shape=jax.ShapeDtypeStruct((s, n), out_dtype),
        scratch_shapes=[pltpu.VMEM((tm, d), BF16)],
        compiler_params=_cparams(("parallel", "arbitrary")),
        name="in_proj",
    )(x, sc, sh, w)


def _split_w_in(w):
    d = w.shape[0]
    o = 0
    q = w[:, o:o + _NSA_Q]; o += _NSA_Q
    kv = w[:, o:o + _NSA_KV]; o += _NSA_KV
    g = w[:, o:o + _NSA_G]; o += _NSA_G
    mlqk = w[:, o:o + _ML_QK]; o += _ML_QK
    mlv = w[:, o:o + _ML_V]; o += _ML_V
    mlif = w[:, o:o + _ML_IF]; o += _ML_IF
    mlo = w[:, o:o + _ML_V]; o += _ML_V
    merge = w[:, o:o + 2 * D_MODEL]
    per_g = 3 * NSA_HPG
    z = lambda n: jnp.zeros((d, n), w.dtype)
    wb = jnp.concatenate([q, kv[:, _CMP_COLS:], mlv], axis=1)
    wf = jnp.concatenate(
        [merge, kv[:, :_CMP_COLS], mlqk, mlo,
         g[:, :per_g], z(128 - per_g), g[:, per_g:], z(128 - per_g), mlif, z(128 - _ML_IF), z(128)],
        axis=1)
    return wb.astype(BF16), wf.astype(BF16)


def _compress_kernel(a_ref, pe_ref, w1_ref, w2_ref, o_ref):
    n = o_ref.shape[1]
    hd = NSA_HD

    def half_sum(lo):
        acc = jnp.zeros((n, CMP_HID), F32)
        for l in range(0, CMP_STRIDE, 2):
            x = jnp.concatenate(
                [a_ref[pl.ds(l + u, n, stride=CMP_STRIDE), :] + pe_ref[0, lo + l + u:lo + l + u + 1, :]
                 for u in range(2)], axis=1)
            w = w1_ref[0, (lo + l) * hd:(lo + l + 2) * hd, :]
            acc = acc + _dot(x.astype(BF16), w.astype(BF16))
        return acc

    pre = half_sum(0) + pltpu.roll(half_sum(CMP_STRIDE), n - 1, 0)
    g = _gelu_tanh(pre)
    o_ref[0] = _dot(g.astype(BF16), w2_ref[0].astype(BF16)).astype(o_ref.dtype)


def _compress(projf, pe, w1, w2):
    s = projf.shape[0]
    n = s // CMP_STRIDE
    four = 2 * NSA_GROUPS
    return pl.pallas_call(
        _compress_kernel,
        grid=(four,),
        in_specs=[
            pl.BlockSpec((s, NSA_HD), lambda j: (0, _F_CMP // NSA_HD + j)),
            pl.BlockSpec((1, CMP_LEN, NSA_HD), lambda j: (j // NSA_GROUPS, 0, 0)),
            pl.BlockSpec((1, CMP_LEN * NSA_HD, CMP_HID), lambda j: (j // NSA_GROUPS, 0, 0)),
            pl.BlockSpec((1, CMP_HID, NSA_HD), lambda j: (j // NSA_GROUPS, 0, 0)),
        ],
        out_specs=pl.BlockSpec((1, n, NSA_HD), lambda j: (j, 0, 0)),
        out_shape=jax.ShapeDtypeStruct((four, n, NSA_HD), BF16),
        compiler_params=_cparams(("parallel",)),
        name="nsa_compress",
    )(projf, pe, w1, w2)


def _nsa_kernel(q_ref, kc_ref, vc_ref, ks_ref, vs_ref, kw_ref, vw_ref, gate_ref, ovt_ref, eye_ref,
                o_ref, nsel_scr, npast_scr, *, step, nq):
    pid = pl.program_id(1)
    rows = NSA_HPG * Q_BLOCK
    scale = NSA_HD ** -0.5
    c2 = scale * math.log2(math.e)
    row_q = lax.broadcasted_iota(jnp.int32, (rows, 1), 0) & (Q_BLOCK - 1)
    lane_q = lax.broadcasted_iota(jnp.int32, (1, Q_BLOCK), 1)
    blk = []
    for u in range(nq):
        i = pid * nq + u
        qt = q_ref[u * Q_BLOCK:(u + 1) * Q_BLOCK, :]
        qs = jnp.concatenate([qt[:, h * NSA_HD:(h + 1) * NSA_HD] for h in range(NSA_HPG)], axis=0)
        blk.append((i, qs, i * Q_BLOCK + row_q))
    i_last = blk[-1][0]

    ncp = kc_ref.shape[1]

    def cmp_branch(width):
        outs = []
        cidx = lax.broadcasted_iota(jnp.int32, (rows, width), 1)
        ovt = ovt_ref[:, :width]
        for _, qs, t in blk:
            last_c = (t - (CMP_LEN - 1)) // CMP_STRIDE
            p_c = _masked_softmax(_dot_nt(qs, kc_ref[0, :width, :]), cidx <= last_c, scale)
            o = _dot(p_c.astype(BF16), vc_ref[0, :width, :])
            psum = (p_c[0:Q_BLOCK] + p_c[Q_BLOCK:2 * Q_BLOCK]
                    + p_c[2 * Q_BLOCK:3 * Q_BLOCK] + p_c[3 * Q_BLOCK:4 * Q_BLOCK])
            outs += [o, _dot_nt(ovt, psum.astype(BF16))]
        return tuple(outs)

    n_vis = (i_last * Q_BLOCK + Q_BLOCK - CMP_LEN) // CMP_STRIDE + 1
    branch = lambda: cmp_branch(ncp)
    for width in (ncp // 2, ncp // 4):
        if width % 128 == 0:
            branch = (lambda w, other: lambda: lax.cond(n_vis <= w, lambda: cmp_branch(w), other))(width, branch)
    cmp_out = branch()
    o_cs, imps = cmp_out[0::2], cmp_out[1::2]

    wsl = imps[0].shape[0]
    bid = lax.broadcasted_iota(jnp.int32, (wsl, Q_BLOCK), 0)
    bidf = bid.astype(F32)
    taken = -1e9
    nsel0 = jnp.full((wsl, Q_BLOCK), -SEL_BIG, F32)
    score0s = []
    for (i, _, _), imp in zip(blk, imps):
        t_row = i * Q_BLOCK + lane_q
        cur = t_row // SLC_LEN
        score0s.append(jnp.where(bid == 0, 3e6, jnp.where(bid == cur, 2e6, jnp.where(
            bid == cur - 1, 1e6, jnp.where(bid * SLC_LEN <= t_row, imp, -1.0 - bidf)))))

    def take_max(c, lanes=None):
        sc, ns = c
        hit = sc == jnp.max(sc, axis=0, keepdims=True)
        if lanes is not None:
            hit = hit & lanes
        return jnp.where(hit, taken, sc), jnp.where(hit, 0.0, ns)

    cs = [(jnp.where(s0 >= 1e6, taken, s0), jnp.where(s0 >= 1e6, 0.0, nsel0)) for s0 in score0s]
    n_min = SLC_TOPK - 3
    for _ in range(n_min):
        cs = [take_max(c) for c in cs]

    def early_rounds():
        cur = lane_q // SLC_LEN
        n_free = SLC_TOPK - (1 + jnp.where(cur >= 1, 1, 0) + jnp.where(cur >= 2, 1, 0))
        e = cs[0]
        for r in range(n_min + 1, SLC_TOPK):
            e = take_max(e, n_free >= r)
        return e

    cs[0] = lax.cond(pid == 0, early_rounds, lambda: cs[0])
    nsels = tuple(c[1] for c in cs)

    def topk_ties():
        def one(_, c):
            sc, ns = c
            mx = jnp.max(sc, axis=0, keepdims=True)
            first = jnp.min(jnp.where(sc == mx, bidf, float(wsl)), axis=0, keepdims=True)
            hit = bidf == first
            return jnp.where(hit, taken, sc), jnp.where(hit, 0.0, ns)
        return tuple(lax.fori_loop(0, SLC_TOPK, one, (s0, nsel0))[1] for s0 in score0s)

    n_taken = functools.reduce(jnp.maximum, [jnp.sum(jnp.where(ns == 0.0, 1.0, 0.0), axis=0, keepdims=True)
                                             for ns in nsels])
    nsels = lax.cond(jnp.max(n_taken) > SLC_TOPK, topk_ties, lambda: nsels)
    for u, ((i, _, _), ns) in enumerate(zip(blk, nsels)):
        blk_d = (i * Q_BLOCK) // SLC_LEN
        nsel_scr[u] = ns
        npast_scr[u] = jnp.where(bid < blk_d, ns, -SEL_BIG)

    def keys_aug(k0, width, sel_scr):
        b0 = k0 // SLC_LEN
        bias = jnp.concatenate(
            [jnp.broadcast_to(sel_scr[pl.ds(b0 + b, 1), :], (SLC_LEN, Q_BLOCK))
             for b in range(width // SLC_LEN)], axis=0)
        return jnp.concatenate([ks_ref[pl.ds(k0, width), :], bias.astype(BF16)], axis=1)

    def vals_aug(k0, width):
        ones_col = (lax.broadcasted_iota(jnp.int32, (width, NSA_HD), 1) == 0).astype(BF16)
        return jnp.concatenate([vs_ref[pl.ds(k0, width), :], ones_col], axis=1)

    qas, diag, o_ws = [], [], []
    wlen = WINDOW + Q_BLOCK
    for u, (i, qs, t) in enumerate(blk):
        qa = jnp.concatenate([qs, eye_ref[...]], axis=1)
        k_d = pl.multiple_of(i * Q_BLOCK, Q_BLOCK)
        s_d = _dot_nt(qa, keys_aug(k_d, Q_BLOCK, nsel_scr.at[u]))
        s_d = jnp.where(k_d + lane_q <= t, s_d, NEG_INF)
        m_d = jnp.max(s_d, axis=-1, keepdims=True)
        acc_d = _dot(jnp.exp2((s_d - m_d) * c2).astype(BF16), vals_aug(k_d, Q_BLOCK))
        qas.append(qa)
        diag.append((m_d, acc_d))
        ks0 = pl.multiple_of(jnp.maximum(i - WINDOW // Q_BLOCK, 0) * Q_BLOCK, Q_BLOCK)
        s_w = _dot_nt(qs, kw_ref[pl.ds(ks0, wlen), :])
        kpos = ks0 + lax.broadcasted_iota(jnp.int32, (1, wlen), 1)
        p_w = _masked_softmax(s_w, (kpos <= t) & (kpos > t - WINDOW), scale)
        o_ws.append(_dot(p_w.astype(BF16), vw_ref[pl.ds(ks0, wlen), :]))

    gates = _sigmoid(gate_ref[...])
    for u, (i, qs, t) in enumerate(blk):
        qa = qas[u]
        past_scr = npast_scr.at[u]
        m_d, acc_d = diag[u]

        def exact_step(k0, carry):
            m, acc = carry
            k0 = pl.multiple_of(k0, SEL_TILE)
            s = _dot_nt(qa, keys_aug(k0, SEL_TILE, past_scr))
            m_new = jnp.maximum(m, jnp.max(s, axis=-1, keepdims=True))
            p = jnp.exp2((s - m_new) * c2)
            return m_new, jnp.exp2((m - m_new) * c2) * acc + _dot(p.astype(BF16), vals_aug(k0, SEL_TILE))

        def fast_step(k0, width, carry):
            m, acc, risk = carry
            k0 = pl.multiple_of(k0, SEL_TILE)
            s = _dot_nt(qa, keys_aug(k0, width, past_scr))
            p = jnp.exp2((s - m) * c2)
            mx = jnp.max(s, axis=-1, keepdims=True)
            m_new = jnp.maximum(m, mx)
            acc = (acc + _dot(p.astype(BF16), vals_aug(k0, width))) * jnp.exp2((m - m_new) * c2)
            return m_new, acc, jnp.maximum(risk, mx - m)

        n_past = i * Q_BLOCK
        n_steps = n_past // step
        carry = (m_d, acc_d, jnp.zeros((rows, 1), F32))
        carry = lax.fori_loop(0, n_steps, lambda j, c: fast_step(j * step, step, c), carry)
        rem = n_past - n_steps * step
        k_r = n_steps * step
        carry = lax.cond(
            rem > step // 2, lambda: fast_step(k_r, step, carry),
            lambda: lax.cond(rem > step // 4, lambda: fast_step(k_r, step // 2, carry),
                             lambda: lax.cond(rem > 0, lambda: fast_step(k_r, step // 4, carry),
                                              lambda: carry)))
        _, acc_s, risk = carry

        def redo_exact():
            n_tiles = (n_past + SEL_TILE - 1) // SEL_TILE
            return lax.fori_loop(0, n_tiles, lambda j, c: exact_step(j * SEL_TILE, c), (m_d, acc_d))[1]

        acc_s = lax.cond(jnp.max(risk) * c2 > SEL_EXP2_GUARD, redo_exact, lambda: acc_s)
        o_s = acc_s[:, :NSA_HD] / acc_s[:, NSA_HD:NSA_HD + 1]

        g_u = gates[u * Q_BLOCK:(u + 1) * Q_BLOCK]
        for h in range(NSA_HPG):
            r = slice(h * Q_BLOCK, (h + 1) * Q_BLOCK)
            out = (g_u[:, 3 * h:3 * h + 1] * o_cs[u][r] + g_u[:, 3 * h + 1:3 * h + 2] * o_s[r]
                   + g_u[:, 3 * h + 2:3 * h + 3] * o_ws[u][r])
            o_ref[u * Q_BLOCK:(u + 1) * Q_BLOCK, h * NSA_HD:(h + 1) * NSA_HD] = out.astype(o_ref.dtype)


def _nsa_attention(projb, projf, kcv, single_buffer=True):
    s = projb.shape[0]
    n_slc = s // SLC_LEN
    step = min(SEL_STEP, s)
    nq = NSA_Q_BLOCKS_PER_STEP
    qrows = nq * Q_BLOCK
    assert n_slc >= SLC_TOPK and s >= WINDOW + Q_BLOCK and s % step == 0 and step % (4 * SEL_TILE) == 0
    ncp = kcv.shape[1]
    wsl = max(n_slc, 128)
    hd = NSA_HD
    c_start = jnp.arange(ncp)[None, :] * CMP_STRIDE
    s_start = jnp.arange(wsl)[:, None] * SLC_LEN
    ovt = ((c_start < s_start + SLC_LEN) & (c_start + CMP_LEN > s_start)
           & (jnp.arange(wsl)[:, None] < n_slc)).astype(BF16)
    eye = jnp.tile(jnp.eye(Q_BLOCK, dtype=BF16), (NSA_HPG, 1))
    resident = dict(pipeline_mode=pl.Buffered(1)) if single_buffer else {}
    kvb = _B_KV // hd
    gb = _F_SMALL // 128
    return pl.pallas_call(
        functools.partial(_nsa_kernel, step=step, nq=nq),
        grid=(NSA_GROUPS, s // qrows),
        in_specs=[
            pl.BlockSpec((qrows, NSA_HPG * hd), lambda g, i: (i, g)),
            pl.BlockSpec((1, ncp, hd), lambda g, i: (g, 0, 0)),
            pl.BlockSpec((1, ncp, hd), lambda g, i: (NSA_GROUPS + g, 0, 0)),
            pl.BlockSpec((s, hd), lambda g, i: (0, kvb + g), **resident),
            pl.BlockSpec((s, hd), lambda g, i: (0, kvb + 2 + g), **resident),
            pl.BlockSpec((s, hd), lambda g, i: (0, kvb + 4 + g), **resident),
            pl.BlockSpec((s, hd), lambda g, i: (0, kvb + 6 + g), **resident),
            pl.BlockSpec((qrows, 128), lambda g, i: (i, gb + g)),
            pl.BlockSpec((wsl, ncp), lambda g, i: (0, 0)),
            pl.BlockSpec((NSA_HPG * Q_BLOCK, Q_BLOCK), lambda g, i: (0, 0)),
        ],
        out_specs=pl.BlockSpec((qrows, NSA_HPG * hd), lambda g, i: (i, g)),
        out_shape=jax.ShapeDtypeStruct((s, NSA_HEADS * hd), BF16),
        scratch_shapes=[pltpu.VMEM((nq, wsl, Q_BLOCK), F32), pltpu.VMEM((nq, wsl, Q_BLOCK), F32)],
        compiler_params=_cparams(("parallel", "arbitrary")),
        name="nsa_attention",
    )(projb, kcv, kcv, projb, projb, projb, projb, projf, ovt, eye)


def _mlstm_kernel(q_ref, k_ref, v_ref, o0_ref, o1_ref, gc_ref, gr_ref, cw_ref, cb_ref, ng_ref,
                  y_ref, extq, extk, c_st, n_st, m_st, *, cpb):
    step = pl.program_id(0)
    rows = cpb * ML_CHUNK
    half = ML_HEADS * ML_DQK
    L = ML_CHUNK

    @pl.when(step == 0)
    def _():
        extq[0:8, :] = jnp.zeros((8, half), F32)
        extk[0:8, :] = jnp.zeros((8, half), F32)
        c_st[...] = jnp.zeros_like(c_st)
        n_st[...] = jnp.zeros_like(n_st)
        m_st[...] = jnp.zeros_like(m_st)

    extq[8:8 + rows, :] = q_ref[...]
    extk[8:8 + rows, :] = k_ref[...]
    cw = cw_ref[...]
    cb = cb_ref[...]

    def conv(ext, lo):
        acc = cb[:, lo:lo + half]
        for j in range(ML_CONV):
            off = 8 - (ML_CONV - 1) + j
            acc = acc + cw[j:j + 1, lo:lo + half] * ext[off:off + rows, :]
        return _silu(acc)

    qa = conv(extq, 0)
    ka = conv(extk, half) * (ML_DQK ** -0.5)
    extq[0:8, :] = extq[rows:rows + 8, :]
    extk[0:8, :] = extk[rows:rows + 8, :]

    gcol = gc_ref[...]
    grow = gr_ref[...]
    lf_col_all = _log_sigmoid(gcol[:, ML_HEADS:])
    lf_row_all = _log_sigmoid(grow[ML_HEADS:, :])
    ri = lax.broadcasted_iota(jnp.int32, (L, L), 0)
    ci = lax.broadcasted_iota(jnp.int32, (L, L), 1)
    tri = ri >= ci
    ng = ng_ref[...]

    state = [(c_st[h], n_st[h], m_st[h][:, 0:1]) for h in range(ML_HEADS)]
    ones_blk = jnp.ones((L, 128), BF16)
    for c in range(cpb):
        r0 = c * L
        kt_c = ka[r0:r0 + L, :].T
        for h in range(ML_HEADS):
            qh = qa[r0:r0 + L, h * ML_DQK:(h + 1) * ML_DQK]
            kh = ka[r0:r0 + L, h * ML_DQK:(h + 1) * ML_DQK]
            vh = v_ref[r0:r0 + L, h * ML_DV:(h + 1) * ML_DV]
            ig_col = gcol[r0:r0 + L, h:h + 1]
            ig_row = grow[h:h + 1, r0:r0 + L]
            lf_col = lf_col_all[r0:r0 + L, h:h + 1]
            lf_row = lf_row_all[h:h + 1, r0:r0 + L]
            bcum_col = jnp.sum(jnp.where(tri, lf_row, 0.0), axis=1, keepdims=True)
            bcum_row = jnp.sum(jnp.where(ri <= ci, lf_col, 0.0), axis=0, keepdims=True)
            b_last = jnp.sum(lf_row, axis=1, keepdims=True)
            dmat = jnp.where(tri, bcum_col - bcum_row + ig_row, NEG_INF)
            a_loc = jnp.max(dmat, axis=-1, keepdims=True)
            qb = qh.astype(BF16)
            s_loc = _dot_nt(qb, kh.astype(BF16)) * jnp.exp(dmat - a_loc)
            sv = _dot(s_loc.astype(BF16), vh)
            s_sum = _dot(s_loc.astype(BF16), ones_blk)[:, 0:1]
            a_col = b_last - bcum_col + ig_col
            a_row = b_last - bcum_row + ig_row
            a_max = jnp.max(a_row, axis=-1, keepdims=True)
            k_sum = jnp.sum(kh * jnp.exp(a_col - a_max), axis=0, keepdims=True)
            kwt = kt_c[h * ML_DQK:(h + 1) * ML_DQK, :] * jnp.exp(a_row - a_max)
            kv = _dot(kwt.astype(BF16), vh)

            ct, nrow, m_old = state[h]
            inter = bcum_col + m_old
            m_t = jnp.maximum(inter, a_loc)
            w_inter = jnp.exp(inter - m_t)
            w_loc = jnp.exp(a_loc - m_t)
            num = w_inter * _dot(qb, ct.astype(BF16)) + w_loc * sv
            qn = _dot_nt(qb, jnp.broadcast_to(nrow, (8, ML_DQK)).astype(BF16))[:, 0:1]
            den = w_inter * qn + w_loc * s_sum
            hout = num / jnp.maximum(jnp.abs(den), jnp.exp(-m_t))

            m_new = jnp.maximum(b_last + m_old, a_max)
            decay = jnp.exp(b_last + m_old - m_new)
            g_new = jnp.exp(a_max - m_new)
            state[h] = (decay * ct + g_new * kv, decay * nrow + g_new * k_sum, m_new)

            o_ref = o0_ref if h < ML_HEADS // 2 else o1_ref
            oc = (h % (ML_HEADS // 2)) * ML_DV
            og = _sigmoid(o_ref[r0:r0 + L, oc:oc + ML_DV])
            yn = _ln_rows(hout) * ng[:, h * ML_DV:(h + 1) * ML_DV]
            y_ref[r0:r0 + L, h * ML_DV:(h + 1) * ML_DV] = (yn * og).astype(y_ref.dtype)

    for h in range(ML_HEADS):
        c_st[h], n_st[h] = state[h][0], state[h][1]
        m_st[h] = jnp.broadcast_to(state[h][2], (1, 128))


def _mlstm(projb, projf, gates_col, gates_row, conv_w, conv_b, norm_g, cpb):
    s = projb.shape[0]
    rows = cpb * ML_CHUNK
    half = ML_HEADS * ML_DQK
    dv = ML_HEADS * ML_DV
    qblk = _F_MLQK // half
    oblk = _F_MLO // half
    return pl.pallas_call(
        functools.partial(_mlstm_kernel, cpb=cpb),
        grid=(s // rows,),
        in_specs=[
            pl.BlockSpec((rows, half), lambda j: (j, qblk)),
            pl.BlockSpec((rows, half), lambda j: (j, qblk + 1)),
            pl.BlockSpec((rows, dv), lambda j: (j, _B_MLV // dv)),
            pl.BlockSpec((rows, half), lambda j: (j, oblk)),
            pl.BlockSpec((rows, half), lambda j: (j, oblk + 1)),
            pl.BlockSpec((rows, 2 * ML_HEADS), lambda j: (j, 0)),
            pl.BlockSpec((2 * ML_HEADS, rows), lambda j: (0, j)),
            pl.BlockSpec((ML_CONV, 2 * half), lambda j: (0, 0)),
            pl.BlockSpec((1, 2 * half), lambda j: (0, 0)),
            pl.BlockSpec((1, dv), lambda j: (0, 0)),
        ],
        out_specs=pl.BlockSpec((rows, dv), lambda j: (j, 0)),
        out_shape=jax.ShapeDtypeStruct((s, dv), BF16),
        scratch_shapes=[
            pltpu.VMEM((rows + 8, half), F32),
            pltpu.VMEM((rows + 8, half), F32),
            pltpu.VMEM((ML_HEADS, ML_DQK, ML_DV), F32),
            pltpu.VMEM((ML_HEADS, 1, ML_DQK), F32),
            pltpu.VMEM((ML_HEADS, 1, 128), F32),
        ],
        compiler_params=_cparams(("arbitrary",)),
        name="mlstm",
    )(projf, projf, projb, projf, projf, gates_col, gates_row, conv_w, conv_b, norm_g)


def _merge_kernel(yn_ref, ym_ref, g0_ref, g1_ref, x_ref, wn_ref, wm_ref, wo_ref, gate_ref,
                  lg_ref, lb_ref, o_ref):
    a = _dot(yn_ref[...], wn_ref[...])
    b = _dot(ym_ref[...], wm_ref[...])
    merged = _sigmoid(g0_ref[...]) * a + _sigmoid(g1_ref[...]) * b
    y = _dot(merged.astype(BF16), wo_ref[...])
    z = ALPHA * x_ref[...] + gate_ref[...] * y
    o_ref[...] = _ln_rows(z) * lg_ref[...] + lb_ref[...]


def _merge_outproj(y_nsa, y_ml, projf, x, wn, wm, wo, gate, ln_g, ln_b, tm, single_buffer=True):
    s, d = x.shape
    resident = dict(pipeline_mode=pl.Buffered(1)) if single_buffer else {}
    mb = _F_MERGE // d
    row = lambda i: (0, 0)
    return pl.pallas_call(
        _merge_kernel,
        grid=(s // tm,),
        in_specs=[
            pl.BlockSpec((tm, y_nsa.shape[1]), lambda i: (i, 0)),
            pl.BlockSpec((tm, y_ml.shape[1]), lambda i: (i, 0)),
            pl.BlockSpec((tm, d), lambda i: (i, mb)),
            pl.BlockSpec((tm, d), lambda i: (i, mb + 1)),
            pl.BlockSpec((tm, d), lambda i: (i, 0)),
            pl.BlockSpec(wn.shape, row, **resident),
            pl.BlockSpec(wm.shape, row, **resident),
            pl.BlockSpec(wo.shape, row, **resident),
            pl.BlockSpec((1, d), row),
            pl.BlockSpec((1, d), row),
            pl.BlockSpec((1, d), row),
        ],
        out_specs=pl.BlockSpec((tm, d), lambda i: (i, 0)),
        out_shape=jax.ShapeDtypeStruct((s, d), F32),
        compiler_params=_cparams(("parallel",)),
        name="merge_outproj",
    )(y_nsa, y_ml, projf, projf, x, wn, wm, wo, gate, ln_g, ln_b)


def _ffn_kernel(x_ref, xh_ref, sc_ref, sh_ref, gate_ref, wa_ref, wg_ref, cw_ref, cb_ref, wd_ref,
                lg_ref, lb_ref, o_ref, h_scr, a_scr, acc, *, halo):
    i = pl.program_id(0)
    f = pl.program_id(1)
    tm = x_ref.shape[0]

    @pl.when(f == 0)
    def _():
        mod = lambda v: (_ln_rows(v) * (1.0 + sc_ref[...]) + sh_ref[...]).astype(BF16)
        h_scr[0:halo, :] = mod(xh_ref[...])
        h_scr[halo:halo + tm, :] = mod(x_ref[...])
        acc[...] = jnp.zeros_like(acc)

    hx = h_scr[...]
    a_ext = _dot(hx, wa_ref[...])
    rid = lax.broadcasted_iota(jnp.int32, (halo + tm, 1), 0)
    a_scr[...] = jnp.where((rid >= halo) | (i > 0), a_ext, 0.0)
    cw = cw_ref[...]
    conv = cb_ref[...]
    for j in range(FFN_CONV):
        off = halo - (FFN_CONV - 1) + j
        conv = conv + cw[j:j + 1, :] * a_scr[off:off + tm, :]
    g = _dot(h_scr[halo:halo + tm, :], wg_ref[...])
    act = (_silu(conv) * g).astype(BF16)
    acc[...] += _dot(act, wd_ref[...])

    @pl.when(f == pl.num_programs(1) - 1)
    def _():
        z = ALPHA * x_ref[...] + gate_ref[...] * acc[...]
        o_ref[...] = _ln_rows(z) * lg_ref[...] + lb_ref[...]


def _conv_ffn(x, sc, sh, gate, w_up, conv_w, conv_b, w_down, ln_g, ln_b, tm, tf):
    s, d = x.shape
    dff = w_down.shape[0]
    halo = 16
    nf = dff // tf
    row = lambda i, f: (0, 0)
    return pl.pallas_call(
        functools.partial(_ffn_kernel, halo=halo),
        grid=(s // tm, nf),
        in_specs=[
            pl.BlockSpec((tm, d), lambda i, f: (i, 0)),
            pl.BlockSpec((halo, d), lambda i, f: (jnp.maximum(i * (tm // halo) - 1, 0), 0)),
            pl.BlockSpec((1, d), row),
            pl.BlockSpec((1, d), row),
            pl.BlockSpec((1, d), row),
            pl.BlockSpec((d, tf), lambda i, f: (0, f)),
            pl.BlockSpec((d, tf), lambda i, f: (0, nf + f)),
            pl.BlockSpec((FFN_CONV, tf), lambda i, f: (0, f)),
            pl.BlockSpec((1, tf), lambda i, f: (0, f)),
            pl.BlockSpec((tf, d), lambda i, f: (f, 0)),
            pl.BlockSpec((1, d), row),
            pl.BlockSpec((1, d), row),
        ],
        out_specs=pl.BlockSpec((tm, d), lambda i, f: (i, 0)),
        out_shape=jax.ShapeDtypeStruct((s, d), F32),
        scratch_shapes=[
            pltpu.VMEM((halo + tm, d), BF16),
            pltpu.VMEM((halo + tm, tf), F32),
            pltpu.VMEM((tm, d), F32),
        ],
        compiler_params=_cparams(("parallel", "arbitrary")),
        name="conv_ffn",
    )(x, x, sc, sh, gate, w_up, w_up, conv_w, conv_b, w_down, ln_g, ln_b)


def _token_mixer(x, sc, sh, gate, w_in, cmp_pe, cmp_w1, cmp_w2, ml_conv_w, ml_conv_b, ml_gate_b,
                 ml_norm_g, w_br_nsa, w_br_ml, w_o, ln_g, ln_b):
    s, d = x.shape
    wb, wf = _split_w_in(w_in)
    tm = min(INPROJ_TM, s)
    projb = _inproj(x, sc, sh, wb, BF16, tm, INPROJ_TN)
    projf = _inproj(x, sc, sh, wf, F32, tm, INPROJ_TN)
    kcv = _compress(projf, cmp_pe, cmp_w1, cmp_w2)
    y_nsa = _nsa_attention(projb, projf, kcv)
    ifo = _F_SMALL + 2 * 128
    gates_col = projf[:, ifo:ifo + _ML_IF] + ml_gate_b[None, :]
    y_ml = _mlstm(projb, projf, gates_col, gates_col.T, ml_conv_w, ml_conv_b[None, :],
                  ml_norm_g[None, :], cpb=min(MLSTM_CHUNKS_PER_STEP, s // ML_CHUNK))
    return _merge_outproj(y_nsa, y_ml, projf, x, w_br_nsa.astype(BF16), w_br_ml.astype(BF16),
                          w_o.astype(BF16), gate, ln_g, ln_b, tm=min(MERGE_TM, s))


def _forward(x, c, w_ada, b_ada, w_in, cmp_pe, cmp_w1, cmp_w2, ml_conv_w, ml_conv_b, ml_gate_b,
             ml_norm_g, w_br_nsa, w_br_ml, w_o, w_up, ffn_conv_w, ffn_conv_b, w_down, ln_g, ln_b):
    b, s, d = x.shape
    assert b == 1 and d == D_MODEL
    depth = w_ada.shape[0]
    mod = _modulation(c, w_ada, b_ada)
    xs = x[0]
    for l in range(depth):
        sh1, sc1, g1, sh2, sc2, g2 = [mod[l, :, k * d:(k + 1) * d] for k in range(6)]
        xs = _token_mixer(xs, sc1, sh1, g1, w_in[l], cmp_pe[l], cmp_w1[l], cmp_w2[l], ml_conv_w[l],
                          ml_conv_b[l], ml_gate_b[l], ml_norm_g[l], w_br_nsa[l], w_br_ml[l], w_o[l],
                          ln_g[l, 0][None, :], ln_b[l, 0][None, :])
        xs = _conv_ffn(xs, sc2, sh2, g2, w_up[l].astype(BF16), ffn_conv_w[l], ffn_conv_b[l][None, :],
                       w_down[l].astype(BF16), ln_g[l, 1][None, :], ln_b[l, 1][None, :],
                       tm=min(FFN_TM, s), tf=FFN_TF)
    return xs[None]


def kernel(x, c, w_ada, b_ada, w_in, cmp_pe, cmp_w1, cmp_w2, ml_conv_w, ml_conv_b, ml_gate_b, ml_norm_g, w_br_nsa, w_br_ml, w_o, w_up, ffn_conv_w, ffn_conv_b, w_down, ln_g, ln_b):
    return _forward(x, c, w_ada, b_ada, w_in, cmp_pe, cmp_w1, cmp_w2, ml_conv_w, ml_conv_b,
                    ml_gate_b, ml_norm_g, w_br_nsa, w_br_ml, w_o, w_up, ffn_conv_w, ffn_conv_b,
                    w_down, ln_g, ln_b)
```

```python
import functools
import math

import jax
import jax.numpy as jnp
from jax import lax
from jax.experimental import pallas as pl
from jax.experimental.pallas import tpu as pltpu

F32 = jnp.float32
BF16 = jnp.bfloat16

D_MODEL = 2048
DEPTH = 2
NSA_HEADS = 8
NSA_GROUPS = 2
NSA_HPG = NSA_HEADS // NSA_GROUPS
NSA_HD = 128
CMP_LEN = 32
CMP_STRIDE = 16
CMP_HID = 256
SLC_LEN = 64
SLC_TOPK = 16
WINDOW = 512
Q_BLOCK = 128
ML_HEADS = 4
ML_DQK = 128
ML_DV = 256
ML_CHUNK = 64
ML_CONV = 4
D_FF = 5632
FFN_CONV = 3
ALPHA = (2 * DEPTH) ** 0.25
LN_EPS = 1e-5
NEG_INF = -1e30

V7X_VMEM_BYTES = 64 * 1024 * 1024
VMEM_LIMIT = V7X_VMEM_BYTES - 8 * 1024 * 1024

INPROJ_TM, INPROJ_TN = 1024, 1024
MERGE_TM = 256
FFN_TM, FFN_TF = 512, 512
MLSTM_CHUNKS_PER_STEP = 4

_NSA_Q = NSA_HEADS * NSA_HD
_NSA_KV = 3 * 2 * NSA_GROUPS * NSA_HD
_NSA_G = 3 * NSA_HEADS
_ML_QK = 2 * ML_HEADS * ML_DQK
_ML_V = ML_HEADS * ML_DV
_ML_IF = 2 * ML_HEADS
_CMP_COLS = 2 * NSA_GROUPS * NSA_HD
_F_MERGE = 0
_F_CMP = 2 * D_MODEL
_F_MLQK = _F_CMP + _CMP_COLS
_F_MLO = _F_MLQK + _ML_QK
_F_SMALL = _F_MLO + _ML_V
_F_COLS = _F_SMALL + 4 * 128
_B_Q = 0
_B_KV = _NSA_Q
_B_MLV = _B_KV + 8 * NSA_HD
_B_COLS = _B_MLV + _ML_V

SEL_TILE = 512
SEL_STEP = 4096
SEL_BIG = 2.0 ** 100
NSA_Q_BLOCKS_PER_STEP = 2
SEL_EXP2_GUARD = 64.0


def _cparams(sem):
    return pltpu.CompilerParams(dimension_semantics=sem, vmem_limit_bytes=VMEM_LIMIT)


def _ln_rows(x):
    mu = jnp.mean(x, axis=-1, keepdims=True)
    xc = x - mu
    var = jnp.mean(xc * xc, axis=-1, keepdims=True)
    return xc * lax.rsqrt(var + LN_EPS)


def _sigmoid(x):
    return 1.0 / (1.0 + jnp.exp(-x))


def _silu(x):
    return x * _sigmoid(x)


def _gelu_tanh(x):
    c = math.sqrt(2.0 / math.pi)
    return x * (0.5 * (1.0 + jnp.tanh(c * (x + 0.044715 * (x * x * x)))))


def _log_sigmoid(x):
    return jnp.minimum(x, 0.0) - jnp.log1p(jnp.exp(-jnp.abs(x)))


def _dot(a, b):
    return jnp.dot(a, b, preferred_element_type=F32)


def _dot_nt(a, b):
    return lax.dot_general(a, b, (((1,), (1,)), ((), ())), preferred_element_type=F32)


def _masked_softmax(raw, mask, scale):
    raw = jnp.where(mask, raw, NEG_INF)
    m = jnp.max(raw, axis=-1, keepdims=True)
    e = jnp.exp2((raw - m) * (scale * math.log2(math.e)))
    den = jnp.sum(e, axis=-1, keepdims=True)
    return e * jnp.where(m > 0.5 * NEG_INF, 1.0 / den, 0.0)


def _mod_kernel(c_ref, w_ref, b_ref, o_ref):
    ca = _silu(c_ref[...])
    o = _dot(ca.astype(BF16), w_ref[0].astype(BF16))
    o_ref[0] = o[0:1] + b_ref[0]


def _modulation(c, w_ada, b_ada):
    depth, d, n = w_ada.shape
    tn = INPROJ_TN
    c8 = jnp.broadcast_to(c, (8, d))
    return pl.pallas_call(
        _mod_kernel,
        grid=(depth, n // tn),
        in_specs=[
            pl.BlockSpec((8, d), lambda l, j: (0, 0)),
            pl.BlockSpec((1, d, tn), lambda l, j: (l, 0, j)),
            pl.BlockSpec((1, 1, tn), lambda l, j: (l, 0, j)),
        ],
        out_specs=pl.BlockSpec((1, 1, tn), lambda l, j: (l, 0, j)),
        out_shape=jax.ShapeDtypeStruct((depth, 1, n), F32),
        compiler_params=_cparams(("parallel", "parallel")),
        name="adaln_mod",
    )(c8, w_ada, b_ada.reshape(depth, 1, n))


def _inproj_kernel(x_ref, sc_ref, sh_ref, w_ref, o_ref, h_scr):
    @pl.when(pl.program_id(1) == 0)
    def _():
        h = _ln_rows(x_ref[...]) * (1.0 + sc_ref[...]) + sh_ref[...]
        h_scr[...] = h.astype(BF16)

    o_ref[...] = _dot(h_scr[...], w_ref[...]).astype(o_ref.dtype)


def _inproj(x, sc, sh, w, out_dtype, tm, tn):
    s, d = x.shape
    n = w.shape[1]
    return pl.pallas_call(
        _inproj_kernel,
        grid=(s // tm, n // tn),
        in_specs=[
            pl.BlockSpec((tm, d), lambda i, j: (i, 0)),
            pl.BlockSpec((1, d), lambda i, j: (0, 0)),
            pl.BlockSpec((1, d), lambda i, j: (0, 0)),
            pl.BlockSpec((d, tn), lambda i, j: (0, j)),
        ],
        out_specs=pl.BlockSpec((tm, tn), lambda i, j: (i, j)),
        out_shape=jax.ShapeDtypeStruct((s, n), out_dtype),
        scratch_shapes=[pltpu.VMEM((tm, d), BF16)],
        compiler_params=_cparams(("parallel", "arbitrary")),
        name="in_proj",
    )(x, sc, sh, w)


def _split_w_in(w):
    d = w.shape[0]
    o = 0
    q = w[:, o:o + _NSA_Q]; o += _NSA_Q
    kv = w[:, o:o + _NSA_KV]; o += _NSA_KV
    g = w[:, o:o + _NSA_G]; o += _NSA_G
    mlqk = w[:, o:o + _ML_QK]; o += _ML_QK
    mlv = w[:, o:o + _ML_V]; o += _ML_V
    mlif = w[:, o:o + _ML_IF]; o += _ML_IF
    mlo = w[:, o:o + _ML_V]; o += _ML_V
    merge = w[:, o:o + 2 * D_MODEL]
    per_g = 3 * NSA_HPG
    z = lambda n: jnp.zeros((d, n), w.dtype)
    wb = jnp.concatenate([q, kv[:, _CMP_COLS:], mlv], axis=1)
    wf = jnp.concatenate(
        [merge, kv[:, :_CMP_COLS], mlqk, mlo,
         g[:, :per_g], z(128 - per_g), g[:, per_g:], z(128 - per_g), mlif, z(128 - _ML_IF), z(128)],
        axis=1)
    return wb.astype(BF16), wf.astype(BF16)


def _compress_kernel(a_ref, pe_ref, w1_ref, w2_ref, o_ref):
    n = o_ref.shape[1]
    hd = NSA_HD

    def half_sum(lo):
        acc = jnp.zeros((n, CMP_HID), F32)
        for l in range(0, CMP_STRIDE, 2):
            x = jnp.concatenate(
                [a_ref[pl.ds(l + u, n, stride=CMP_STRIDE), :] + pe_ref[0, lo + l + u:lo + l + u + 1, :]
                 for u in range(2)], axis=1)
            w = w1_ref[0, (lo + l) * hd:(lo + l + 2) * hd, :]
            acc = acc + _dot(x.astype(BF16), w.astype(BF16))
        return acc

    pre = half_sum(0) + pltpu.roll(half_sum(CMP_STRIDE), n - 1, 0)
    g = _gelu_tanh(pre)
    o_ref[0] = _dot(g.astype(BF16), w2_ref[0].astype(BF16)).astype(o_ref.dtype)


def _compress(projf, pe, w1, w2):
    s = projf.shape[0]
    n = s // CMP_STRIDE
    four = 2 * NSA_GROUPS
    return pl.pallas_call(
        _compress_kernel,
        grid=(four,),
        in_specs=[
            pl.BlockSpec((s, NSA_HD), lambda j: (0, _F_CMP // NSA_HD + j)),
            pl.BlockSpec((1, CMP_LEN, NSA_HD), lambda j: (j // NSA_GROUPS, 0, 0)),
            pl.BlockSpec((1, CMP_LEN * NSA_HD, CMP_HID), lambda j: (j // NSA_GROUPS, 0, 0)),
            pl.BlockSpec((1, CMP_HID, NSA_HD), lambda j: (j // NSA_GROUPS, 0, 0)),
        ],
        out_specs=pl.BlockSpec((1, n, NSA_HD), lambda j: (j, 0, 0)),
        out_shape=jax.ShapeDtypeStruct((four, n, NSA_HD), BF16),
        compiler_params=_cparams(("parallel",)),
        name="nsa_compress",
    )(projf, pe, w1, w2)


def _nsa_kernel(q_ref, kc_ref, vc_ref, ks_ref, vs_ref, kw_ref, vw_ref, gate_ref, ovt_ref, eye_ref,
                o_ref, nsel_scr, npast_scr, *, step, nq):
    pid = pl.program_id(1)
    rows = NSA_HPG * Q_BLOCK
    scale = NSA_HD ** -0.5
    c2 = scale * math.log2(math.e)
    row_q = lax.broadcasted_iota(jnp.int32, (rows, 1), 0) & (Q_BLOCK - 1)
    lane_q = lax.broadcasted_iota(jnp.int32, (1, Q_BLOCK), 1)
    blk = []
    for u in range(nq):
        i = pid * nq + u
        qt = q_ref[u * Q_BLOCK:(u + 1) * Q_BLOCK, :]
        qs = jnp.concatenate([qt[:, h * NSA_HD:(h + 1) * NSA_HD] for h in range(NSA_HPG)], axis=0)
        blk.append((i, qs, i * Q_BLOCK + row_q))
    i_last = blk[-1][0]

    ncp = kc_ref.shape[1]

    def cmp_branch(width):
        outs = []
        cidx = lax.broadcasted_iota(jnp.int32, (rows, width), 1)
        ovt = ovt_ref[:, :width]
        for _, qs, t in blk:
            last_c = (t - (CMP_LEN - 1)) // CMP_STRIDE
            p_c = _masked_softmax(_dot_nt(qs, kc_ref[0, :width, :]), cidx <= last_c, scale)
            o = _dot(p_c.astype(BF16), vc_ref[0, :width, :])
            psum = (p_c[0:Q_BLOCK] + p_c[Q_BLOCK:2 * Q_BLOCK]
                    + p_c[2 * Q_BLOCK:3 * Q_BLOCK] + p_c[3 * Q_BLOCK:4 * Q_BLOCK])
            p_hi = psum.astype(BF16)
            p_lo = (psum - p_hi.astype(F32)).astype(BF16)
            outs += [o, _dot_nt(ovt, p_hi) + _dot_nt(ovt, p_lo)]
        return tuple(outs)

    n_vis = (i_last * Q_BLOCK + Q_BLOCK - CMP_LEN) // CMP_STRIDE + 1
    branch = lambda: cmp_branch(ncp)
    for width in (ncp // 2, ncp // 4):
        if width % 128 == 0:
            branch = (lambda w, other: lambda: lax.cond(n_vis <= w, lambda: cmp_branch(w), other))(width, branch)
    cmp_out = branch()
    o_cs, imps = cmp_out[0::2], cmp_out[1::2]

    wsl = imps[0].shape[0]
    bid = lax.broadcasted_iota(jnp.int32, (wsl, Q_BLOCK), 0)
    bidf = bid.astype(F32)
    taken = -1e9
    nsel0 = jnp.full((wsl, Q_BLOCK), -SEL_BIG, F32)
    score0s = []
    for (i, _, _), imp in zip(blk, imps):
        t_row = i * Q_BLOCK + lane_q
        cur = t_row // SLC_LEN
        score0s.append(jnp.where(bid == 0, 3e6, jnp.where(bid == cur, 2e6, jnp.where(
            bid == cur - 1, 1e6, jnp.where(bid * SLC_LEN <= t_row, imp, -1.0 - bidf)))))

    def take_max(c, lanes=None):
        sc, ns = c
        hit = sc == jnp.max(sc, axis=0, keepdims=True)
        if lanes is not None:
            hit = hit & lanes
        return jnp.where(hit, taken, sc), jnp.where(hit, 0.0, ns)

    cs = [(jnp.where(s0 >= 1e6, taken, s0), jnp.where(s0 >= 1e6, 0.0, nsel0)) for s0 in score0s]
    n_min = SLC_TOPK - 3
    for _ in range(n_min):
        cs = [take_max(c) for c in cs]

    def early_rounds():
        cur = lane_q // SLC_LEN
        n_free = SLC_TOPK - (1 + jnp.where(cur >= 1, 1, 0) + jnp.where(cur >= 2, 1, 0))
        e = cs[0]
        for r in range(n_min + 1, SLC_TOPK):
            e = take_max(e, n_free >= r)
        return e

    cs[0] = lax.cond(pid == 0, early_rounds, lambda: cs[0])
    nsels = tuple(c[1] for c in cs)

    def topk_ties():
        def one(_, c):
            sc, ns = c
            mx = jnp.max(sc, axis=0, keepdims=True)
            first = jnp.min(jnp.where(sc == mx, bidf, float(wsl)), axis=0, keepdims=True)
            hit = bidf == first
            return jnp.where(hit, taken, sc), jnp.where(hit, 0.0, ns)
        return tuple(lax.fori_loop(0, SLC_TOPK, one, (s0, nsel0))[1] for s0 in score0s)

    n_taken = functools.reduce(jnp.maximum, [jnp.sum(jnp.where(ns == 0.0, 1.0, 0.0), axis=0, keepdims=True)
                                             for ns in nsels])
    nsels = lax.cond(jnp.max(n_taken) > SLC_TOPK, topk_ties, lambda: nsels)
    for u, ((i, _, _), ns) in enumerate(zip(blk, nsels)):
        blk_d = (i * Q_BLOCK) // SLC_LEN
        nsel_scr[u] = ns
        npast_scr[u] = jnp.where(bid < blk_d, ns, -SEL_BIG)

    def keys_aug(k0, width, sel_scr):
        b0 = k0 // SLC_LEN
        bias = jnp.concatenate(
            [jnp.broadcast_to(sel_scr[pl.ds(b0 + b, 1), :], (SLC_LEN, Q_BLOCK))
             for b in range(width // SLC_LEN)], axis=0)
        return jnp.concatenate([ks_ref[pl.ds(k0, width), :], bias.astype(BF16)], axis=1)

    def vals_aug(k0, width):
        ones_col = (lax.broadcasted_iota(jnp.int32, (width, NSA_HD), 1) == 0).astype(BF16)
        return jnp.concatenate([vs_ref[pl.ds(k0, width), :], ones_col], axis=1)

    qas, diag, o_ws = [], [], []
    wlen = WINDOW + Q_BLOCK
    for u, (i, qs, t) in enumerate(blk):
        qa = jnp.concatenate([qs, eye_ref[...]], axis=1)
        k_d = pl.multiple_of(i * Q_BLOCK, Q_BLOCK)
        s_d = _dot_nt(qa, keys_aug(k_d, Q_BLOCK, nsel_scr.at[u]))
        s_d = jnp.where(k_d + lane_q <= t, s_d, NEG_INF)
        m_d = jnp.max(s_d, axis=-1, keepdims=True)
        acc_d = _dot(jnp.exp2((s_d - m_d) * c2).astype(BF16), vals_aug(k_d, Q_BLOCK))
        qas.append(qa)
        diag.append((m_d, acc_d))
        ks0 = pl.multiple_of(jnp.maximum(i - WINDOW // Q_BLOCK, 0) * Q_BLOCK, Q_BLOCK)
        s_w = _dot_nt(qs, kw_ref[pl.ds(ks0, wlen), :])
        kpos = ks0 + lax.broadcasted_iota(jnp.int32, (1, wlen), 1)
        p_w = _masked_softmax(s_w, (kpos <= t) & (kpos > t - WINDOW), scale)
        o_ws.append(_dot(p_w.astype(BF16), vw_ref[pl.ds(ks0, wlen), :]))

    gates = _sigmoid(gate_ref[...])
    for u, (i, qs, t) in enumerate(blk):
        qa = qas[u]
        past_scr = npast_scr.at[u]
        m_d, acc_d = diag[u]

        def exact_step(k0, carry):
            m, acc = carry
            k0 = pl.multiple_of(k0, SEL_TILE)
            s = _dot_nt(qa, keys_aug(k0, SEL_TILE, past_scr))
            m_new = jnp.maximum(m, jnp.max(s, axis=-1, keepdims=True))
            p = jnp.exp2((s - m_new) * c2)
            return m_new, jnp.exp2((m - m_new) * c2) * acc + _dot(p.astype(BF16), vals_aug(k0, SEL_TILE))

        def fast_step(k0, width, carry):
            m, acc, risk = carry
            k0 = pl.multiple_of(k0, SEL_TILE)
            s = _dot_nt(qa, keys_aug(k0, width, past_scr))
            p = jnp.exp2((s - m) * c2)
            mx = jnp.max(s, axis=-1, keepdims=True)
            m_new = jnp.maximum(m, mx)
            acc = (acc + _dot(p.astype(BF16), vals_aug(k0, width))) * jnp.exp2((m - m_new) * c2)
            return m_new, acc, jnp.maximum(risk, mx - m)

        n_past = i * Q_BLOCK
        n_steps = n_past // step
        carry = (m_d, acc_d, jnp.zeros((rows, 1), F32))
        carry = lax.fori_loop(0, n_steps, lambda j, c: fast_step(j * step, step, c), carry)
        rem = n_past - n_steps * step
        k_r = n_steps * step
        carry = lax.cond(
            rem > step // 2, lambda: fast_step(k_r, step, carry),
            lambda: lax.cond(rem > step // 4, lambda: fast_step(k_r, step // 2, carry),
                             lambda: lax.cond(rem > 0, lambda: fast_step(k_r, step // 4, carry),
                                              lambda: carry)))
        _, acc_s, risk = carry

        def redo_exact():
            n_tiles = (n_past + SEL_TILE - 1) // SEL_TILE
            return lax.fori_loop(0, n_tiles, lambda j, c: exact_step(j * SEL_TILE, c), (m_d, acc_d))[1]

        acc_s = lax.cond(jnp.max(risk) * c2 > SEL_EXP2_GUARD, redo_exact, lambda: acc_s)
        o_s = acc_s[:, :NSA_HD] / acc_s[:, NSA_HD:NSA_HD + 1]

        g_u = gates[u * Q_BLOCK:(u + 1) * Q_BLOCK]
        for h in range(NSA_HPG):
            r = slice(h * Q_BLOCK, (h + 1) * Q_BLOCK)
            out = (g_u[:, 3 * h:3 * h + 1] * o_cs[u][r] + g_u[:, 3 * h + 1:3 * h + 2] * o_s[r]
                   + g_u[:, 3 * h + 2:3 * h + 3] * o_ws[u][r])
            o_ref[u * Q_BLOCK:(u + 1) * Q_BLOCK, h * NSA_HD:(h + 1) * NSA_HD] = out.astype(o_ref.dtype)


def _nsa_attention(projb, projf, kcv, single_buffer=True):
    s = projb.shape[0]
    n_slc = s // SLC_LEN
    step = min(SEL_STEP, s)
    nq = NSA_Q_BLOCKS_PER_STEP
    qrows = nq * Q_BLOCK
    assert n_slc >= SLC_TOPK and s >= WINDOW + Q_BLOCK and s % step == 0 and step % (4 * SEL_TILE) == 0
    ncp = kcv.shape[1]
    wsl = max(n_slc, 128)
    hd = NSA_HD
    c_start = jnp.arange(ncp)[None, :] * CMP_STRIDE
    s_start = jnp.arange(wsl)[:, None] * SLC_LEN
    ovt = ((c_start < s_start + SLC_LEN) & (c_start + CMP_LEN > s_start)
           & (jnp.arange(wsl)[:, None] < n_slc)).astype(BF16)
    eye = jnp.tile(jnp.eye(Q_BLOCK, dtype=BF16), (NSA_HPG, 1))
    resident = dict(pipeline_mode=pl.Buffered(1)) if single_buffer else {}
    kvb = _B_KV // hd
    gb = _F_SMALL // 128
    return pl.pallas_call(
        functools.partial(_nsa_kernel, step=step, nq=nq),
        grid=(NSA_GROUPS, s // qrows),
        in_specs=[
            pl.BlockSpec((qrows, NSA_HPG * hd), lambda g, i: (i, g)),
            pl.BlockSpec((1, ncp, hd), lambda g, i: (g, 0, 0)),
            pl.BlockSpec((1, ncp, hd), lambda g, i: (NSA_GROUPS + g, 0, 0)),
            pl.BlockSpec((s, hd), lambda g, i: (0, kvb + g), **resident),
            pl.BlockSpec((s, hd), lambda g, i: (0, kvb + 2 + g), **resident),
            pl.BlockSpec((s, hd), lambda g, i: (0, kvb + 4 + g), **resident),
            pl.BlockSpec((s, hd), lambda g, i: (0, kvb + 6 + g), **resident),
            pl.BlockSpec((qrows, 128), lambda g, i: (i, gb + g)),
            pl.BlockSpec((wsl, ncp), lambda g, i: (0, 0)),
            pl.BlockSpec((NSA_HPG * Q_BLOCK, Q_BLOCK), lambda g, i: (0, 0)),
        ],
        out_specs=pl.BlockSpec((qrows, NSA_HPG * hd), lambda g, i: (i, g)),
        out_shape=jax.ShapeDtypeStruct((s, NSA_HEADS * hd), BF16),
        scratch_shapes=[pltpu.VMEM((nq, wsl, Q_BLOCK), F32), pltpu.VMEM((nq, wsl, Q_BLOCK), F32)],
        compiler_params=_cparams(("parallel", "arbitrary")),
        name="nsa_attention",
    )(projb, kcv, kcv, projb, projb, projb, projb, projf, ovt, eye)


def _mlstm_kernel(q_ref, k_ref, v_ref, o0_ref, o1_ref, gc_ref, gr_ref, cw_ref, cb_ref, ng_ref,
                  y_ref, extq, extk, c_st, n_st, m_st, *, cpb):
    step = pl.program_id(0)
    rows = cpb * ML_CHUNK
    half = ML_HEADS * ML_DQK
    L = ML_CHUNK

    @pl.when(step == 0)
    def _():
        extq[0:8, :] = jnp.zeros((8, half), F32)
        extk[0:8, :] = jnp.zeros((8, half), F32)
        c_st[...] = jnp.zeros_like(c_st)
        n_st[...] = jnp.zeros_like(n_st)
        m_st[...] = jnp.zeros_like(m_st)

    extq[8:8 + rows, :] = q_ref[...]
    extk[8:8 + rows, :] = k_ref[...]
    cw = cw_ref[...]
    cb = cb_ref[...]

    def conv(ext, lo):
        acc = cb[:, lo:lo + half]
        for j in range(ML_CONV):
            off = 8 - (ML_CONV - 1) + j
            acc = acc + cw[j:j + 1, lo:lo + half] * ext[off:off + rows, :]
        return _silu(acc)

    qa = conv(extq, 0)
    ka = conv(extk, half) * (ML_DQK ** -0.5)
    extq[0:8, :] = extq[rows:rows + 8, :]
    extk[0:8, :] = extk[rows:rows + 8, :]

    gcol = gc_ref[...]
    grow = gr_ref[...]
    lf_col_all = _log_sigmoid(gcol[:, ML_HEADS:])
    lf_row_all = _log_sigmoid(grow[ML_HEADS:, :])
    ri = lax.broadcasted_iota(jnp.int32, (L, L), 0)
    ci = lax.broadcasted_iota(jnp.int32, (L, L), 1)
    tri = ri >= ci
    ng = ng_ref[...]

    state = [(c_st[h], n_st[h], m_st[h][:, 0:1]) for h in range(ML_HEADS)]
    ones_blk = jnp.ones((L, 128), BF16)
    for c in range(cpb):
        r0 = c * L
        kt_c = ka[r0:r0 + L, :].T
        for h in range(ML_HEADS):
            qh = qa[r0:r0 + L, h * ML_DQK:(h + 1) * ML_DQK]
            kh = ka[r0:r0 + L, h * ML_DQK:(h + 1) * ML_DQK]
            vh = v_ref[r0:r0 + L, h * ML_DV:(h + 1) * ML_DV]
            ig_col = gcol[r0:r0 + L, h:h + 1]
            ig_row = grow[h:h + 1, r0:r0 + L]
            lf_col = lf_col_all[r0:r0 + L, h:h + 1]
            lf_row = lf_row_all[h:h + 1, r0:r0 + L]
            bcum_col = jnp.sum(jnp.where(tri, lf_row, 0.0), axis=1, keepdims=True)
            bcum_row = jnp.sum(jnp.where(ri <= ci, lf_col, 0.0), axis=0, keepdims=True)
            b_last = jnp.sum(lf_row, axis=1, keepdims=True)
            dmat = jnp.where(tri, bcum_col - bcum_row + ig_row, NEG_INF)
            a_loc = jnp.max(dmat, axis=-1, keepdims=True)
            qb = qh.astype(BF16)
            s_loc = _dot_nt(qb, kh.astype(BF16)) * jnp.exp(dmat - a_loc)
            sv = _dot(s_loc.astype(BF16), vh)
            s_sum = _dot(s_loc.astype(BF16), ones_blk)[:, 0:1]
            a_col = b_last - bcum_col + ig_col
            a_row = b_last - bcum_row + ig_row
            a_max = jnp.max(a_row, axis=-1, keepdims=True)
            k_sum = jnp.sum(kh * jnp.exp(a_col - a_max), axis=0, keepdims=True)
            kwt = kt_c[h * ML_DQK:(h + 1) * ML_DQK, :] * jnp.exp(a_row - a_max)
            kv = _dot(kwt.astype(BF16), vh)

            ct, nrow, m_old = state[h]
            inter = bcum_col + m_old
            m_t = jnp.maximum(inter, a_loc)
            w_inter = jnp.exp(inter - m_t)
            w_loc = jnp.exp(a_loc - m_t)
            num = w_inter * _dot(qb, ct.astype(BF16)) + w_loc * sv
            qn = _dot_nt(qb, jnp.broadcast_to(nrow, (8, ML_DQK)).astype(BF16))[:, 0:1]
            den = w_inter * qn + w_loc * s_sum
            hout = num / jnp.maximum(jnp.abs(den), jnp.exp(-m_t))

            m_new = jnp.maximum(b_last + m_old, a_max)
            decay = jnp.exp(b_last + m_old - m_new)
            g_new = jnp.exp(a_max - m_new)
            state[h] = (decay * ct + g_new * kv, decay * nrow + g_new * k_sum, m_new)

            o_ref = o0_ref if h < ML_HEADS // 2 else o1_ref
            oc = (h % (ML_HEADS // 2)) * ML_DV
            og = _sigmoid(o_ref[r0:r0 + L, oc:oc + ML_DV])
            yn = _ln_rows(hout) * ng[:, h * ML_DV:(h + 1) * ML_DV]
            y_ref[r0:r0 + L, h * ML_DV:(h + 1) * ML_DV] = (yn * og).astype(y_ref.dtype)

    for h in range(ML_HEADS):
        c_st[h], n_st[h] = state[h][0], state[h][1]
        m_st[h] = jnp.broadcast_to(state[h][2], (1, 128))


def _mlstm(projb, projf, gates_col, gates_row, conv_w, conv_b, norm_g, cpb):
    s = projb.shape[0]
    rows = cpb * ML_CHUNK
    half = ML_HEADS * ML_DQK
    dv = ML_HEADS * ML_DV
    qblk = _F_MLQK // half
    oblk = _F_MLO // half
    return pl.pallas_call(
        functools.partial(_mlstm_kernel, cpb=cpb),
        grid=(s // rows,),
        in_specs=[
            pl.BlockSpec((rows, half), lambda j: (j, qblk)),
            pl.BlockSpec((rows, half), lambda j: (j, qblk + 1)),
            pl.BlockSpec((rows, dv), lambda j: (j, _B_MLV // dv)),
            pl.BlockSpec((rows, half), lambda j: (j, oblk)),
            pl.BlockSpec((rows, half), lambda j: (j, oblk + 1)),
            pl.BlockSpec((rows, 2 * ML_HEADS), lambda j: (j, 0)),
            pl.BlockSpec((2 * ML_HEADS, rows), lambda j: (0, j)),
            pl.BlockSpec((ML_CONV, 2 * half), lambda j: (0, 0)),
            pl.BlockSpec((1, 2 * half), lambda j: (0, 0)),
            pl.BlockSpec((1, dv), lambda j: (0, 0)),
        ],
        out_specs=pl.BlockSpec((rows, dv), lambda j: (j, 0)),
        out_shape=jax.ShapeDtypeStruct((s, dv), BF16),
        scratch_shapes=[
            pltpu.VMEM((rows + 8, half), F32),
            pltpu.VMEM((rows + 8, half), F32),
            pltpu.VMEM((ML_HEADS, ML_DQK, ML_DV), F32),
            pltpu.VMEM((ML_HEADS, 1, ML_DQK), F32),
            pltpu.VMEM((ML_HEADS, 1, 128), F32),
        ],
        compiler_params=_cparams(("arbitrary",)),
        name="mlstm",
    )(projf, projf, projb, projf, projf, gates_col, gates_row, conv_w, conv_b, norm_g)


def _merge_kernel(yn_ref, ym_ref, g0_ref, g1_ref, x_ref, wn_ref, wm_ref, wo_ref, gate_ref,
                  lg_ref, lb_ref, o_ref):
    a = _dot(yn_ref[...], wn_ref[...])
    b = _dot(ym_ref[...], wm_ref[...])
    merged = _sigmoid(g0_ref[...]) * a + _sigmoid(g1_ref[...]) * b
    y = _dot(merged.astype(BF16), wo_ref[...])
    z = ALPHA * x_ref[...] + gate_ref[...] * y
    o_ref[...] = _ln_rows(z) * lg_ref[...] + lb_ref[...]


def _merge_outproj(y_nsa, y_ml, projf, x, wn, wm, wo, gate, ln_g, ln_b, tm, single_buffer=True):
    s, d = x.shape
    resident = dict(pipeline_mode=pl.Buffered(1)) if single_buffer else {}
    mb = _F_MERGE // d
    row = lambda i: (0, 0)
    return pl.pallas_call(
        _merge_kernel,
        grid=(s // tm,),
        in_specs=[
            pl.BlockSpec((tm, y_nsa.shape[1]), lambda i: (i, 0)),
            pl.BlockSpec((tm, y_ml.shape[1]), lambda i: (i, 0)),
            pl.BlockSpec((tm, d), lambda i: (i, mb)),
            pl.BlockSpec((tm, d), lambda i: (i, mb + 1)),
            pl.BlockSpec((tm, d), lambda i: (i, 0)),
            pl.BlockSpec(wn.shape, row, **resident),
            pl.BlockSpec(wm.shape, row, **resident),
            pl.BlockSpec(wo.shape, row, **resident),
            pl.BlockSpec((1, d), row),
            pl.BlockSpec((1, d), row),
            pl.BlockSpec((1, d), row),
        ],
        out_specs=pl.BlockSpec((tm, d), lambda i: (i, 0)),
        out_shape=jax.ShapeDtypeStruct((s, d), F32),
        compiler_params=_cparams(("parallel",)),
        name="merge_outproj",
    )(y_nsa, y_ml, projf, projf, x, wn, wm, wo, gate, ln_g, ln_b)


def _ffn_kernel(x_ref, xh_ref, sc_ref, sh_ref, gate_ref, wa_ref, wg_ref, cw_ref, cb_ref, wd_ref,
                lg_ref, lb_ref, o_ref, h_scr, a_scr, acc, *, halo):
    i = pl.program_id(0)
    f = pl.program_id(1)
    tm = x_ref.shape[0]

    @pl.when(f == 0)
    def _():
        mod = lambda v: (_ln_rows(v) * (1.0 + sc_ref[...]) + sh_ref[...]).astype(BF16)
        h_scr[0:halo, :] = mod(xh_ref[...])
        h_scr[halo:halo + tm, :] = mod(x_ref[...])
        acc[...] = jnp.zeros_like(acc)

    hx = h_scr[...]
    a_ext = _dot(hx, wa_ref[...])
    rid = lax.broadcasted_iota(jnp.int32, (halo + tm, 1), 0)
    a_scr[...] = jnp.where((rid >= halo) | (i > 0), a_ext, 0.0)
    cw = cw_ref[...]
    conv = cb_ref[...]
    for j in range(FFN_CONV):
        off = halo - (FFN_CONV - 1) + j
        conv = conv + cw[j:j + 1, :] * a_scr[off:off + tm, :]
    g = _dot(h_scr[halo:halo + tm, :], wg_ref[...])
    act = (_silu(conv) * g).astype(BF16)
    acc[...] += _dot(act, wd_ref[...])

    @pl.when(f == pl.num_programs(1) - 1)
    def _():
        z = ALPHA * x_ref[...] + gate_ref[...] * acc[...]
        o_ref[...] = _ln_rows(z) * lg_ref[...] + lb_ref[...]


def _conv_ffn(x, sc, sh, gate, w_up, conv_w, conv_b, w_down, ln_g, ln_b, tm, tf):
    s, d = x.shape
    dff = w_down.shape[0]
    halo = 16
    nf = dff // tf
    row = lambda i, f: (0, 0)
    return pl.pallas_call(
        functools.partial(_ffn_kernel, halo=halo),
        grid=(s // tm, nf),
        in_specs=[
            pl.BlockSpec((tm, d), lambda i, f: (i, 0)),
            pl.BlockSpec((halo, d), lambda i, f: (jnp.maximum(i * (tm // halo) - 1, 0), 0)),
            pl.BlockSpec((1, d), row),
            pl.BlockSpec((1, d), row),
            pl.BlockSpec((1, d), row),
            pl.BlockSpec((d, tf), lambda i, f: (0, f)),
            pl.BlockSpec((d, tf), lambda i, f: (0, nf + f)),
            pl.BlockSpec((FFN_CONV, tf), lambda i, f: (0, f)),
            pl.BlockSpec((1, tf), lambda i, f: (0, f)),
            pl.BlockSpec((tf, d), lambda i, f: (f, 0)),
            pl.BlockSpec((1, d), row),
            pl.BlockSpec((1, d), row),
        ],
        out_specs=pl.BlockSpec((tm, d), lambda i, f: (i, 0)),
        out_shape=jax.ShapeDtypeStruct((s, d), F32),
        scratch_shapes=[
            pltpu.VMEM((halo + tm, d), BF16),
            pltpu.VMEM((halo + tm, tf), F32),
            pltpu.VMEM((tm, d), F32),
        ],
        compiler_params=_cparams(("parallel", "arbitrary")),
        name="conv_ffn",
    )(x, x, sc, sh, gate, w_up, w_up, conv_w, conv_b, w_down, ln_g, ln_b)


def _token_mixer(x, sc, sh, gate, w_in, cmp_pe, cmp_w1, cmp_w2, ml_conv_w, ml_conv_b, ml_gate_b,
                 ml_norm_g, w_br_nsa, w_br_ml, w_o, ln_g, ln_b):
    s, d = x.shape
    wb, wf = _split_w_in(w_in)
    tm = min(INPROJ_TM, s)
    projb = _inproj(x, sc, sh, wb, BF16, tm, INPROJ_TN)
    projf = _inproj(x, sc, sh, wf, F32, tm, INPROJ_TN)
    kcv = _compress(projf, cmp_pe, cmp_w1, cmp_w2)
    y_nsa = _nsa_attention(projb, projf, kcv)
    ifo = _F_SMALL + 2 * 128
    gates_col = projf[:, ifo:ifo + _ML_IF] + ml_gate_b[None, :]
    y_ml = _mlstm(projb, projf, gates_col, gates_col.T, ml_conv_w, ml_conv_b[None, :],
                  ml_norm_g[None, :], cpb=min(MLSTM_CHUNKS_PER_STEP, s // ML_CHUNK))
    return _merge_outproj(y_nsa, y_ml, projf, x, w_br_nsa.astype(BF16), w_br_ml.astype(BF16),
                          w_o.astype(BF16), gate, ln_g, ln_b, tm=min(MERGE_TM, s))


def _forward(x, c, w_ada, b_ada, w_in, cmp_pe, cmp_w1, cmp_w2, ml_conv_w, ml_conv_b, ml_gate_b,
             ml_norm_g, w_br_nsa, w_br_ml, w_o, w_up, ffn_conv_w, ffn_conv_b, w_down, ln_g, ln_b):
    b, s, d = x.shape
    assert b == 1 and d == D_MODEL
    depth = w_ada.shape[0]
    mod = _modulation(c, w_ada, b_ada)
    xs = x[0]
    for l in range(depth):
        sh1, sc1, g1, sh2, sc2, g2 = [mod[l, :, k * d:(k + 1) * d] for k in range(6)]
        xs = _token_mixer(xs, sc1, sh1, g1, w_in[l], cmp_pe[l], cmp_w1[l], cmp_w2[l], ml_conv_w[l],
                          ml_conv_b[l], ml_gate_b[l], ml_norm_g[l], w_br_nsa[l], w_br_ml[l], w_o[l],
                          ln_g[l, 0][None, :], ln_b[l, 0][None, :])
        xs = _conv_ffn(xs, sc2, sh2, g2, w_up[l].astype(BF16), ffn_conv_w[l], ffn_conv_b[l][None, :],
                       w_down[l].astype(BF16), ln_g[l, 1][None, :], ln_b[l, 1][None, :],
                       tm=min(FFN_TM, s), tf=FFN_TF)
    return xs[None]


def kernel(x, c, w_ada, b_ada, w_in, cmp_pe, cmp_w1, cmp_w2, ml_conv_w, ml_conv_b, ml_gate_b, ml_norm_g, w_br_nsa, w_br_ml, w_o, w_up, ffn_conv_w, ffn_conv_b, w_down, ln_g, ln_b):
    return _forward(x, c, w_ada, b_ada, w_in, cmp_pe, cmp_w1, cmp_w2, ml_conv_w, ml_conv_b,
                    ml_gate_b, ml_norm_g, w_br_nsa, w_br_ml, w_o, w_up, ffn_conv_w, ffn_conv_b,
                    w_down, ln_g, ln_b)
```

```python
import functools
import math

import jax
import jax.numpy as jnp
from jax import lax
from jax.experimental import pallas as pl
from jax.experimental.pallas import tpu as pltpu

F32 = jnp.float32
BF16 = jnp.bfloat16

D_MODEL = 2048
DEPTH = 2
NSA_HEADS = 8
NSA_GROUPS = 2
NSA_HPG = NSA_HEADS // NSA_GROUPS
NSA_HD = 128
CMP_LEN = 32
CMP_STRIDE = 16
CMP_HID = 256
SLC_LEN = 64
SLC_TOPK = 16
WINDOW = 512
Q_BLOCK = 128
ML_HEADS = 4
ML_DQK = 128
ML_DV = 256
ML_CHUNK = 64
ML_CONV = 4
D_FF = 5632
FFN_CONV = 3
ALPHA = (2 * DEPTH) ** 0.25
LN_EPS = 1e-5
NEG_INF = -1e30

V7X_VMEM_BYTES = 64 * 1024 * 1024
VMEM_LIMIT = V7X_VMEM_BYTES - 8 * 1024 * 1024

INPROJ_TM, INPROJ_TN = 1024, 1024
MERGE_TM = 256
FFN_TM, FFN_TF = 512, 512
MLSTM_CHUNKS_PER_STEP = 4

_NSA_Q = NSA_HEADS * NSA_HD
_NSA_KV = 3 * 2 * NSA_GROUPS * NSA_HD
_NSA_G = 3 * NSA_HEADS
_ML_QK = 2 * ML_HEADS * ML_DQK
_ML_V = ML_HEADS * ML_DV
_ML_IF = 2 * ML_HEADS
_CMP_COLS = 2 * NSA_GROUPS * NSA_HD
_F_MERGE = 0
_F_CMP = 2 * D_MODEL
_F_MLQK = _F_CMP + _CMP_COLS
_F_MLO = _F_MLQK + _ML_QK
_F_SMALL = _F_MLO + _ML_V
_F_COLS = _F_SMALL + 4 * 128
_B_Q = 0
_B_KV = _NSA_Q
_B_MLV = _B_KV + 8 * NSA_HD
_B_COLS = _B_MLV + _ML_V

SEL_TILE = 512
SEL_STEP = 4096
SEL_BIG = 2.0 ** 100
NSA_Q_BLOCKS_PER_STEP = 2
SEL_EXP2_GUARD = 64.0


def _cparams(sem):
    return pltpu.CompilerParams(dimension_semantics=sem, vmem_limit_bytes=VMEM_LIMIT)


def _ln_rows(x):
    mu = jnp.mean(x, axis=-1, keepdims=True)
    xc = x - mu
    var = jnp.mean(xc * xc, axis=-1, keepdims=True)
    return xc * lax.rsqrt(var + LN_EPS)


def _sigmoid(x):
    return 1.0 / (1.0 + jnp.exp(-x))


def _silu(x):
    return x * _sigmoid(x)


def _gelu_tanh(x):
    c = math.sqrt(2.0 / math.pi)
    return x * (0.5 * (1.0 + jnp.tanh(c * (x + 0.044715 * (x * x * x)))))


def _log_sigmoid(x):
    return jnp.minimum(x, 0.0) - jnp.log1p(jnp.exp(-jnp.abs(x)))


def _dot(a, b):
    return jnp.dot(a, b, preferred_element_type=F32)


def _dot_nt(a, b):
    return lax.dot_general(a, b, (((1,), (1,)), ((), ())), preferred_element_type=F32)


def _masked_softmax(raw, mask, scale):
    raw = jnp.where(mask, raw, NEG_INF)
    m = jnp.max(raw, axis=-1, keepdims=True)
    e = jnp.exp2((raw - m) * (scale * math.log2(math.e)))
    den = jnp.sum(e, axis=-1, keepdims=True)
    return e * jnp.where(m > 0.5 * NEG_INF, 1.0 / den, 0.0)


def _mod_kernel(c_ref, w_ref, b_ref, o_ref):
    ca = _silu(c_ref[...])
    o = _dot(ca.astype(BF16), w_ref[0].astype(BF16))
    o_ref[0] = o[0:1] + b_ref[0]


def _modulation(c, w_ada, b_ada):
    depth, d, n = w_ada.shape
    tn = INPROJ_TN
    c8 = jnp.broadcast_to(c, (8, d))
    return pl.pallas_call(
        _mod_kernel,
        grid=(depth, n // tn),
        in_specs=[
            pl.BlockSpec((8, d), lambda l, j: (0, 0)),
            pl.BlockSpec((1, d, tn), lambda l, j: (l, 0, j)),
            pl.BlockSpec((1, 1, tn), lambda l, j: (l, 0, j)),
        ],
        out_specs=pl.BlockSpec((1, 1, tn), lambda l, j: (l, 0, j)),
        out_shape=jax.ShapeDtypeStruct((depth, 1, n), F32),
        compiler_params=_cparams(("parallel", "parallel")),
        name="adaln_mod",
    )(c8, w_ada, b_ada.reshape(depth, 1, n))


def _inproj_kernel(x_ref, sc_ref, sh_ref, w_ref, o_ref, h_scr):
    @pl.when(pl.program_id(1) == 0)
    def _():
        h = _ln_rows(x_ref[...]) * (1.0 + sc_ref[...]) + sh_ref[...]
        h_scr[...] = h.astype(BF16)

    o_ref[...] = _dot(h_scr[...], w_ref[...]).astype(o_ref.dtype)


def _inproj(x, sc, sh, w, out_dtype, tm, tn):
    s, d = x.shape
    n = w.shape[1]
    return pl.pallas_call(
        _inproj_kernel,
        grid=(s // tm, n // tn),
        in_specs=[
            pl.BlockSpec((tm, d), lambda i, j: (i, 0)),
            pl.BlockSpec((1, d), lambda i, j: (0, 0)),
            pl.BlockSpec((1, d), lambda i, j: (0, 0)),
            pl.BlockSpec((d, tn), lambda i, j: (0, j)),
        ],
        out_specs=pl.BlockSpec((tm, tn), lambda i, j: (i, j)),
        out_shape=jax.ShapeDtypeStruct((s, n), out_dtype),
        scratch_shapes=[pltpu.VMEM((tm, d), BF16)],
        compiler_params=_cparams(("parallel", "arbitrary")),
        name="in_proj",
    )(x, sc, sh, w)


def _split_w_in(w):
    d = w.shape[0]
    o = 0
    q = w[:, o:o + _NSA_Q]; o += _NSA_Q
    kv = w[:, o:o + _NSA_KV]; o += _NSA_KV
    g = w[:, o:o + _NSA_G]; o += _NSA_G
    mlqk = w[:, o:o + _ML_QK]; o += _ML_QK
    mlv = w[:, o:o + _ML_V]; o += _ML_V
    mlif = w[:, o:o + _ML_IF]; o += _ML_IF
    mlo = w[:, o:o + _ML_V]; o += _ML_V
    merge = w[:, o:o + 2 * D_MODEL]
    per_g = 3 * NSA_HPG
    z = lambda n: jnp.zeros((d, n), w.dtype)
    wb = jnp.concatenate([q, kv[:, _CMP_COLS:], mlv], axis=1)
    wf = jnp.concatenate(
        [merge, kv[:, :_CMP_COLS], mlqk, mlo,
         g[:, :per_g], z(128 - per_g), g[:, per_g:], z(128 - per_g), mlif, z(128 - _ML_IF), z(128)],
        axis=1)
    return wb.astype(BF16), wf.astype(BF16)


def _compress_kernel(a_ref, pe_ref, w1_ref, w2_ref, o_ref):
    n = o_ref.shape[1]
    hd = NSA_HD

    def half_sum(lo):
        acc = jnp.zeros((n, CMP_HID), F32)
        for l in range(0, CMP_STRIDE, 2):
            x = jnp.concatenate(
                [a_ref[pl.ds(l + u, n, stride=CMP_STRIDE), :] + pe_ref[0, lo + l + u:lo + l + u + 1, :]
                 for u in range(2)], axis=1)
            w = w1_ref[0, (lo + l) * hd:(lo + l + 2) * hd, :]
            acc = acc + _dot(x.astype(BF16), w.astype(BF16))
        return acc

    pre = half_sum(0) + pltpu.roll(half_sum(CMP_STRIDE), n - 1, 0)
    g = _gelu_tanh(pre)
    o_ref[0] = _dot(g.astype(BF16), w2_ref[0].astype(BF16)).astype(o_ref.dtype)


def _compress(projf, pe, w1, w2):
    s = projf.shape[0]
    n = s // CMP_STRIDE
    four = 2 * NSA_GROUPS
    return pl.pallas_call(
        _compress_kernel,
        grid=(four,),
        in_specs=[
            pl.BlockSpec((s, NSA_HD), lambda j: (0, _F_CMP // NSA_HD + j)),
            pl.BlockSpec((1, CMP_LEN, NSA_HD), lambda j: (j // NSA_GROUPS, 0, 0)),
            pl.BlockSpec((1, CMP_LEN * NSA_HD, CMP_HID), lambda j: (j // NSA_GROUPS, 0, 0)),
            pl.BlockSpec((1, CMP_HID, NSA_HD), lambda j: (j // NSA_GROUPS, 0, 0)),
        ],
        out_specs=pl.BlockSpec((1, n, NSA_HD), lambda j: (j, 0, 0)),
        out_shape=jax.ShapeDtypeStruct((four, n, NSA_HD), BF16),
        compiler_params=_cparams(("parallel",)),
        name="nsa_compress",
    )(projf, pe, w1, w2)


def _nsa_kernel(q_ref, kc_ref, vc_ref, ks_ref, vs_ref, kw_ref, vw_ref, gate_ref, ovt_ref, eye_ref,
                o_ref, nsel_scr, npast_scr, *, step, nq):
    pid = pl.program_id(1)
    rows = NSA_HPG * Q_BLOCK
    scale = NSA_HD ** -0.5
    c2 = scale * math.log2(math.e)
    row_q = lax.broadcasted_iota(jnp.int32, (rows, 1), 0) & (Q_BLOCK - 1)
    lane_q = lax.broadcasted_iota(jnp.int32, (1, Q_BLOCK), 1)
    blk = []
    for u in range(nq):
        i = pid * nq + u
        qt = q_ref[u * Q_BLOCK:(u + 1) * Q_BLOCK, :]
        qs = jnp.concatenate([qt[:, h * NSA_HD:(h + 1) * NSA_HD] for h in range(NSA_HPG)], axis=0)
        blk.append((i, qs, i * Q_BLOCK + row_q))
    i_last = blk[-1][0]

    ncp = kc_ref.shape[1]

    def cmp_branch(width):
        outs = []
        cidx = lax.broadcasted_iota(jnp.int32, (rows, width), 1)
        ovt = ovt_ref[:, :width]
        for _, qs, t in blk:
            last_c = (t - (CMP_LEN - 1)) // CMP_STRIDE
            p_c = _masked_softmax(_dot_nt(qs, kc_ref[0, :width, :]), cidx <= last_c, scale)
            o = _dot(p_c.astype(BF16), vc_ref[0, :width, :])
            psum = (p_c[0:Q_BLOCK] + p_c[Q_BLOCK:2 * Q_BLOCK]
                    + p_c[2 * Q_BLOCK:3 * Q_BLOCK] + p_c[3 * Q_BLOCK:4 * Q_BLOCK])
            p_hi = psum.astype(BF16)
            p_lo = (psum - p_hi.astype(F32)).astype(BF16)
            outs += [o, _dot_nt(ovt, p_hi) + _dot_nt(ovt, p_lo)]
        return tuple(outs)

    n_vis = (i_last * Q_BLOCK + Q_BLOCK - CMP_LEN) // CMP_STRIDE + 1
    branch = lambda: cmp_branch(ncp)
    for width in (ncp // 2, ncp // 4):
        if width % 128 == 0:
            branch = (lambda w, other: lambda: lax.cond(n_vis <= w, lambda: cmp_branch(w), other))(width, branch)
    cmp_out = branch()
    o_cs, imps = cmp_out[0::2], cmp_out[1::2]

    wsl = imps[0].shape[0]
    bid = lax.broadcasted_iota(jnp.int32, (wsl, Q_BLOCK), 0)
    bidf = bid.astype(F32)
    taken = -1e9
    nsel0 = jnp.full((wsl, Q_BLOCK), -SEL_BIG, F32)
    score0s = []
    for (i, _, _), imp in zip(blk, imps):
        t_row = i * Q_BLOCK + lane_q
        cur = t_row // SLC_LEN
        score0s.append(jnp.where(bid == 0, 3e6, jnp.where(bid == cur, 2e6, jnp.where(
            bid == cur - 1, 1e6, jnp.where(bid * SLC_LEN <= t_row, imp, -1.0 - bidf)))))

    def take_max(c, lanes=None):
        sc, ns = c
        hit = sc == jnp.max(sc, axis=0, keepdims=True)
        if lanes is not None:
            hit = hit & lanes
        return jnp.where(hit, taken, sc), jnp.where(hit, 0.0, ns)

    cs = [(jnp.where(s0 >= 1e6, taken, s0), jnp.where(s0 >= 1e6, 0.0, nsel0)) for s0 in score0s]
    n_min = SLC_TOPK - 3
    for _ in range(n_min):
        cs = [take_max(c) for c in cs]

    def early_rounds():
        cur = lane_q // SLC_LEN
        n_free = SLC_TOPK - (1 + jnp.where(cur >= 1, 1, 0) + jnp.where(cur >= 2, 1, 0))
        e = cs[0]
        for r in range(n_min + 1, SLC_TOPK):
            e = take_max(e, n_free >= r)
        return e

    cs[0] = lax.cond(pid == 0, early_rounds, lambda: cs[0])
    nsels = tuple(c[1] for c in cs)

    def topk_ties():
        def one(_, c):
            sc, ns = c
            mx = jnp.max(sc, axis=0, keepdims=True)
            first = jnp.min(jnp.where(sc == mx, bidf, float(wsl)), axis=0, keepdims=True)
            hit = bidf == first
            return jnp.where(hit, taken, sc), jnp.where(hit, 0.0, ns)
        return tuple(lax.fori_loop(0, SLC_TOPK, one, (s0, nsel0))[1] for s0 in score0s)

    n_taken = functools.reduce(jnp.maximum, [jnp.sum(jnp.where(ns == 0.0, 1.0, 0.0), axis=0, keepdims=True)
                                             for ns in nsels])
    nsels = lax.cond(jnp.max(n_taken) > SLC_TOPK, topk_ties, lambda: nsels)
    for u, ((i, _, _), ns) in enumerate(zip(blk, nsels)):
        blk_d = (i * Q_BLOCK) // SLC_LEN
        nsel_scr[u] = ns
        npast_scr[u] = jnp.where(bid < blk_d, ns, -SEL_BIG)

    def keys_aug(k0, width, sel_scr):
        b0 = k0 // SLC_LEN
        bias = jnp.concatenate(
            [jnp.broadcast_to(sel_scr[pl.ds(b0 + b, 1), :], (SLC_LEN, Q_BLOCK))
             for b in range(width // SLC_LEN)], axis=0)
        return jnp.concatenate([ks_ref[pl.ds(k0, width), :], bias.astype(BF16)], axis=1)

    def vals_aug(k0, width):
        ones_col = (lax.broadcasted_iota(jnp.int32, (width, NSA_HD), 1) == 0).astype(BF16)
        return jnp.concatenate([vs_ref[pl.ds(k0, width), :], ones_col], axis=1)

    qas, diag, o_ws = [], [], []
    wlen = WINDOW + Q_BLOCK
    for u, (i, qs, t) in enumerate(blk):
        qa = jnp.concatenate([qs, eye_ref[...]], axis=1)
        k_d = pl.multiple_of(i * Q_BLOCK, Q_BLOCK)
        s_d = _dot_nt(qa, keys_aug(k_d, Q_BLOCK, nsel_scr.at[u]))
        s_d = jnp.where(k_d + lane_q <= t, s_d, NEG_INF)
        m_d = jnp.max(s_d, axis=-1, keepdims=True)
        acc_d = _dot(jnp.exp2((s_d - m_d) * c2).astype(BF16), vals_aug(k_d, Q_BLOCK))
        qas.append(qa)
        diag.append((m_d, acc_d))
        ks0 = pl.multiple_of(jnp.maximum(i - WINDOW // Q_BLOCK, 0) * Q_BLOCK, Q_BLOCK)
        s_w = _dot_nt(qs, kw_ref[pl.ds(ks0, wlen), :])
        kpos = ks0 + lax.broadcasted_iota(jnp.int32, (1, wlen), 1)
        p_w = _masked_softmax(s_w, (kpos <= t) & (kpos > t - WINDOW), scale)
        o_ws.append(_dot(p_w.astype(BF16), vw_ref[pl.ds(ks0, wlen), :]))

    gates = _sigmoid(gate_ref[...])
    for u, (i, qs, t) in enumerate(blk):
        qa = qas[u]
        past_scr = npast_scr.at[u]
        m_d, acc_d = diag[u]

        def exact_step(k0, carry):
            m, acc = carry
            k0 = pl.multiple_of(k0, SEL_TILE)
            s = _dot_nt(qa, keys_aug(k0, SEL_TILE, past_scr))
            m_new = jnp.maximum(m, jnp.max(s, axis=-1, keepdims=True))
            p = jnp.exp2((s - m_new) * c2)
            return m_new, jnp.exp2((m - m_new) * c2) * acc + _dot(p.astype(BF16), vals_aug(k0, SEL_TILE))

        def fast_step(k0, width, carry):
            m, acc, risk = carry
            k0 = pl.multiple_of(k0, SEL_TILE)
            s = _dot_nt(qa, keys_aug(k0, width, past_scr))
            p = jnp.exp2((s - m) * c2)
            mx = jnp.max(s, axis=-1, keepdims=True)
            m_new = jnp.maximum(m, mx)
            acc = (acc + _dot(p.astype(BF16), vals_aug(k0, width))) * jnp.exp2((m - m_new) * c2)
            return m_new, acc, jnp.maximum(risk, mx - m)

        n_past = i * Q_BLOCK
        n_steps = n_past // step
        carry = (m_d, acc_d, jnp.zeros((rows, 1), F32))
        carry = lax.fori_loop(0, n_steps, lambda j, c: fast_step(j * step, step, c), carry)
        rem = n_past - n_steps * step
        k_r = n_steps * step
        carry = lax.cond(
            rem > step // 2, lambda: fast_step(k_r, step, carry),
            lambda: lax.cond(rem > 0, lambda: fast_step(k_r, step // 2, carry), lambda: carry))
        _, acc_s, risk = carry

        def redo_exact():
            n_tiles = (n_past + SEL_TILE - 1) // SEL_TILE
            return lax.fori_loop(0, n_tiles, lambda j, c: exact_step(j * SEL_TILE, c), (m_d, acc_d))[1]

        acc_s = lax.cond(jnp.max(risk) * c2 > SEL_EXP2_GUARD, redo_exact, lambda: acc_s)
        o_s = acc_s[:, :NSA_HD] / acc_s[:, NSA_HD:NSA_HD + 1]

        g_u = gates[u * Q_BLOCK:(u + 1) * Q_BLOCK]
        for h in range(NSA_HPG):
            r = slice(h * Q_BLOCK, (h + 1) * Q_BLOCK)
            out = (g_u[:, 3 * h:3 * h + 1] * o_cs[u][r] + g_u[:, 3 * h + 1:3 * h + 2] * o_s[r]
                   + g_u[:, 3 * h + 2:3 * h + 3] * o_ws[u][r])
            o_ref[u * Q_BLOCK:(u + 1) * Q_BLOCK, h * NSA_HD:(h + 1) * NSA_HD] = out.astype(o_ref.dtype)


def _nsa_attention(projb, projf, kcv, single_buffer=True):
    s = projb.shape[0]
    n_slc = s // SLC_LEN
    step = min(SEL_STEP, s)
    nq = NSA_Q_BLOCKS_PER_STEP
    qrows = nq * Q_BLOCK
    assert n_slc >= SLC_TOPK and s >= WINDOW + Q_BLOCK and s % step == 0 and step % (4 * SEL_TILE) == 0
    ncp = kcv.shape[1]
    wsl = max(n_slc, 128)
    hd = NSA_HD
    c_start = jnp.arange(ncp)[None, :] * CMP_STRIDE
    s_start = jnp.arange(wsl)[:, None] * SLC_LEN
    ovt = ((c_start < s_start + SLC_LEN) & (c_start + CMP_LEN > s_start)
           & (jnp.arange(wsl)[:, None] < n_slc)).astype(BF16)
    eye = jnp.tile(jnp.eye(Q_BLOCK, dtype=BF16), (NSA_HPG, 1))
    resident = dict(pipeline_mode=pl.Buffered(1)) if single_buffer else {}
    kvb = _B_KV // hd
    gb = _F_SMALL // 128
    return pl.pallas_call(
        functools.partial(_nsa_kernel, step=step, nq=nq),
        grid=(NSA_GROUPS, s // qrows),
        in_specs=[
            pl.BlockSpec((qrows, NSA_HPG * hd), lambda g, i: (i, g)),
            pl.BlockSpec((1, ncp, hd), lambda g, i: (g, 0, 0)),
            pl.BlockSpec((1, ncp, hd), lambda g, i: (NSA_GROUPS + g, 0, 0)),
            pl.BlockSpec((s, hd), lambda g, i: (0, kvb + g), **resident),
            pl.BlockSpec((s, hd), lambda g, i: (0, kvb + 2 + g), **resident),
            pl.BlockSpec((s, hd), lambda g, i: (0, kvb + 4 + g), **resident),
            pl.BlockSpec((s, hd), lambda g, i: (0, kvb + 6 + g), **resident),
            pl.BlockSpec((qrows, 128), lambda g, i: (i, gb + g)),
            pl.BlockSpec((wsl, ncp), lambda g, i: (0, 0)),
            pl.BlockSpec((NSA_HPG * Q_BLOCK, Q_BLOCK), lambda g, i: (0, 0)),
        ],
        out_specs=pl.BlockSpec((qrows, NSA_HPG * hd), lambda g, i: (i, g)),
        out_shape=jax.ShapeDtypeStruct((s, NSA_HEADS * hd), BF16),
        scratch_shapes=[pltpu.VMEM((nq, wsl, Q_BLOCK), F32), pltpu.VMEM((nq, wsl, Q_BLOCK), F32)],
        compiler_params=_cparams(("parallel", "arbitrary")),
        name="nsa_attention",
    )(projb, kcv, kcv, projb, projb, projb, projb, projf, ovt, eye)


def _mlstm_kernel(q_ref, k_ref, v_ref, o0_ref, o1_ref, gc_ref, gr_ref, cw_ref, cb_ref, ng_ref,
                  y_ref, extq, extk, c_st, n_st, m_st, *, cpb):
    step = pl.program_id(0)
    rows = cpb * ML_CHUNK
    half = ML_HEADS * ML_DQK
    L = ML_CHUNK

    @pl.when(step == 0)
    def _():
        extq[0:8, :] = jnp.zeros((8, half), F32)
        extk[0:8, :] = jnp.zeros((8, half), F32)
        c_st[...] = jnp.zeros_like(c_st)
        n_st[...] = jnp.zeros_like(n_st)
        m_st[...] = jnp.zeros_like(m_st)

    extq[8:8 + rows, :] = q_ref[...]
    extk[8:8 + rows, :] = k_ref[...]
    cw = cw_ref[...]
    cb = cb_ref[...]

    def conv(ext, lo):
        acc = cb[:, lo:lo + half]
        for j in range(ML_CONV):
            off = 8 - (ML_CONV - 1) + j
            acc = acc + cw[j:j + 1, lo:lo + half] * ext[off:off + rows, :]
        return _silu(acc)

    qa = conv(extq, 0)
    ka = conv(extk, half) * (ML_DQK ** -0.5)
    extq[0:8, :] = extq[rows:rows + 8, :]
    extk[0:8, :] = extk[rows:rows + 8, :]

    gcol = gc_ref[...]
    grow = gr_ref[...]
    lf_col_all = _log_sigmoid(gcol[:, ML_HEADS:])
    lf_row_all = _log_sigmoid(grow[ML_HEADS:, :])
    ri = lax.broadcasted_iota(jnp.int32, (L, L), 0)
    ci = lax.broadcasted_iota(jnp.int32, (L, L), 1)
    tri = ri >= ci
    ng = ng_ref[...]

    state = [(c_st[h], n_st[h], m_st[h][:, 0:1]) for h in range(ML_HEADS)]
    ones_blk = jnp.ones((L, 128), BF16)
    for c in range(cpb):
        r0 = c * L
        kt_c = ka[r0:r0 + L, :].T
        for h in range(ML_HEADS):
            qh = qa[r0:r0 + L, h * ML_DQK:(h + 1) * ML_DQK]
            kh = ka[r0:r0 + L, h * ML_DQK:(h + 1) * ML_DQK]
            vh = v_ref[r0:r0 + L, h * ML_DV:(h + 1) * ML_DV]
            ig_col = gcol[r0:r0 + L, h:h + 1]
            ig_row = grow[h:h + 1, r0:r0 + L]
            lf_col = lf_col_all[r0:r0 + L, h:h + 1]
            lf_row = lf_row_all[h:h + 1, r0:r0 + L]
            bcum_col = jnp.sum(jnp.where(tri, lf_row, 0.0), axis=1, keepdims=True)
            bcum_row = jnp.sum(jnp.where(ri <= ci, lf_col, 0.0), axis=0, keepdims=True)
            b_last = jnp.sum(lf_row, axis=1, keepdims=True)
            dmat = jnp.where(tri, bcum_col - bcum_row + ig_row, NEG_INF)
            a_loc = jnp.max(dmat, axis=-1, keepdims=True)
            qb = qh.astype(BF16)
            s_loc = _dot_nt(qb, kh.astype(BF16)) * jnp.exp(dmat - a_loc)
            sv = _dot(s_loc.astype(BF16), vh)
            s_sum = _dot(s_loc.astype(BF16), ones_blk)[:, 0:1]
            a_col = b_last - bcum_col + ig_col
            a_row = b_last - bcum_row + ig_row
            a_max = jnp.max(a_row, axis=-1, keepdims=True)
            k_sum = jnp.sum(kh * jnp.exp(a_col - a_max), axis=0, keepdims=True)
            kwt = kt_c[h * ML_DQK:(h + 1) * ML_DQK, :] * jnp.exp(a_row - a_max)
            kv = _dot(kwt.astype(BF16), vh)

            ct, nrow, m_old = state[h]
            inter = bcum_col + m_old
            m_t = jnp.maximum(inter, a_loc)
            w_inter = jnp.exp(inter - m_t)
            w_loc = jnp.exp(a_loc - m_t)
            num = w_inter * _dot(qb, ct.astype(BF16)) + w_loc * sv
            qn = _dot_nt(qb, jnp.broadcast_to(nrow, (8, ML_DQK)).astype(BF16))[:, 0:1]
            den = w_inter * qn + w_loc * s_sum
            hout = num / jnp.maximum(jnp.abs(den), jnp.exp(-m_t))

            m_new = jnp.maximum(b_last + m_old, a_max)
            decay = jnp.exp(b_last + m_old - m_new)
            g_new = jnp.exp(a_max - m_new)
            state[h] = (decay * ct + g_new * kv, decay * nrow + g_new * k_sum, m_new)

            o_ref = o0_ref if h < ML_HEADS // 2 else o1_ref
            oc = (h % (ML_HEADS // 2)) * ML_DV
            og = _sigmoid(o_ref[r0:r0 + L, oc:oc + ML_DV])
            yn = _ln_rows(hout) * ng[:, h * ML_DV:(h + 1) * ML_DV]
            y_ref[r0:r0 + L, h * ML_DV:(h + 1) * ML_DV] = (yn * og).astype(y_ref.dtype)

    for h in range(ML_HEADS):
        c_st[h], n_st[h] = state[h][0], state[h][1]
        m_st[h] = jnp.broadcast_to(state[h][2], (1, 128))


def _mlstm(projb, projf, gates_col, gates_row, conv_w, conv_b, norm_g, cpb):
    s = projb.shape[0]
    rows = cpb * ML_CHUNK
    half = ML_HEADS * ML_DQK
    dv = ML_HEADS * ML_DV
    qblk = _F_MLQK // half
    oblk = _F_MLO // half
    return pl.pallas_call(
        functools.partial(_mlstm_kernel, cpb=cpb),
        grid=(s // rows,),
        in_specs=[
            pl.BlockSpec((rows, half), lambda j: (j, qblk)),
            pl.BlockSpec((rows, half), lambda j: (j, qblk + 1)),
            pl.BlockSpec((rows, dv), lambda j: (j, _B_MLV // dv)),
            pl.BlockSpec((rows, half), lambda j: (j, oblk)),
            pl.BlockSpec((rows, half), lambda j: (j, oblk + 1)),
            pl.BlockSpec((rows, 2 * ML_HEADS), lambda j: (j, 0)),
            pl.BlockSpec((2 * ML_HEADS, rows), lambda j: (0, j)),
            pl.BlockSpec((ML_CONV, 2 * half), lambda j: (0, 0)),
            pl.BlockSpec((1, 2 * half), lambda j: (0, 0)),
            pl.BlockSpec((1, dv), lambda j: (0, 0)),
        ],
        out_specs=pl.BlockSpec((rows, dv), lambda j: (j, 0)),
        out_shape=jax.ShapeDtypeStruct((s, dv), BF16),
        scratch_shapes=[
            pltpu.VMEM((rows + 8, half), F32),
            pltpu.VMEM((rows + 8, half), F32),
            pltpu.VMEM((ML_HEADS, ML_DQK, ML_DV), F32),
            pltpu.VMEM((ML_HEADS, 1, ML_DQK), F32),
            pltpu.VMEM((ML_HEADS, 1, 128), F32),
        ],
        compiler_params=_cparams(("arbitrary",)),
        name="mlstm",
    )(projf, projf, projb, projf, projf, gates_col, gates_row, conv_w, conv_b, norm_g)


def _merge_kernel(yn_ref, ym_ref, g0_ref, g1_ref, x_ref, wn_ref, wm_ref, wo_ref, gate_ref,
                  lg_ref, lb_ref, o_ref):
    a = _dot(yn_ref[...], wn_ref[...])
    b = _dot(ym_ref[...], wm_ref[...])
    merged = _sigmoid(g0_ref[...]) * a + _sigmoid(g1_ref[...]) * b
    y = _dot(merged.astype(BF16), wo_ref[...])
    z = ALPHA * x_ref[...] + gate_ref[...] * y
    o_ref[...] = _ln_rows(z) * lg_ref[...] + lb_ref[...]


def _merge_outproj(y_nsa, y_ml, projf, x, wn, wm, wo, gate, ln_g, ln_b, tm, single_buffer=True):
    s, d = x.shape
    resident = dict(pipeline_mode=pl.Buffered(1)) if single_buffer else {}
    mb = _F_MERGE // d
    row = lambda i: (0, 0)
    return pl.pallas_call(
        _merge_kernel,
        grid=(s // tm,),
        in_specs=[
            pl.BlockSpec((tm, y_nsa.shape[1]), lambda i: (i, 0)),
            pl.BlockSpec((tm, y_ml.shape[1]), lambda i: (i, 0)),
            pl.BlockSpec((tm, d), lambda i: (i, mb)),
            pl.BlockSpec((tm, d), lambda i: (i, mb + 1)),
            pl.BlockSpec((tm, d), lambda i: (i, 0)),
            pl.BlockSpec(wn.shape, row, **resident),
            pl.BlockSpec(wm.shape, row, **resident),
            pl.BlockSpec(wo.shape, row, **resident),
            pl.BlockSpec((1, d), row),
            pl.BlockSpec((1, d), row),
            pl.BlockSpec((1, d), row),
        ],
        out_specs=pl.BlockSpec((tm, d), lambda i: (i, 0)),
        out_shape=jax.ShapeDtypeStruct((s, d), F32),
        compiler_params=_cparams(("parallel",)),
        name="merge_outproj",
    )(y_nsa, y_ml, projf, projf, x, wn, wm, wo, gate, ln_g, ln_b)


def _ffn_kernel(x_ref, xh_ref, sc_ref, sh_ref, gate_ref, wa_ref, wg_ref, cw_ref, cb_ref, wd_ref,
                lg_ref, lb_ref, o_ref, h_scr, a_scr, acc, *, halo):
    i = pl.program_id(0)
    f = pl.program_id(1)
    tm = x_ref.shape[0]

    @pl.when(f == 0)
    def _():
        mod = lambda v: (_ln_rows(v) * (1.0 + sc_ref[...]) + sh_ref[...]).astype(BF16)
        h_scr[0:halo, :] = mod(xh_ref[...])
        h_scr[halo:halo + tm, :] = mod(x_ref[...])
        acc[...] = jnp.zeros_like(acc)

    hx = h_scr[...]
    a_ext = _dot(hx, wa_ref[...])
    rid = lax.broadcasted_iota(jnp.int32, (halo + tm, 1), 0)
    a_scr[...] = jnp.where((rid >= halo) | (i > 0), a_ext, 0.0)
    cw = cw_ref[...]
    conv = cb_ref[...]
    for j in range(FFN_CONV):
        off = halo - (FFN_CONV - 1) + j
        conv = conv + cw[j:j + 1, :] * a_scr[off:off + tm, :]
    g = _dot(h_scr[halo:halo + tm, :], wg_ref[...])
    act = (_silu(conv) * g).astype(BF16)
    acc[...] += _dot(act, wd_ref[...])

    @pl.when(f == pl.num_programs(1) - 1)
    def _():
        z = ALPHA * x_ref[...] + gate_ref[...] * acc[...]
        o_ref[...] = _ln_rows(z) * lg_ref[...] + lb_ref[...]


def _conv_ffn(x, sc, sh, gate, w_up, conv_w, conv_b, w_down, ln_g, ln_b, tm, tf):
    s, d = x.shape
    dff = w_down.shape[0]
    halo = 16
    nf = dff // tf
    row = lambda i, f: (0, 0)
    return pl.pallas_call(
        functools.partial(_ffn_kernel, halo=halo),
        grid=(s // tm, nf),
        in_specs=[
            pl.BlockSpec((tm, d), lambda i, f: (i, 0)),
            pl.BlockSpec((halo, d), lambda i, f: (jnp.maximum(i * (tm // halo) - 1, 0), 0)),
            pl.BlockSpec((1, d), row),
            pl.BlockSpec((1, d), row),
            pl.BlockSpec((1, d), row),
            pl.BlockSpec((d, tf), lambda i, f: (0, f)),
            pl.BlockSpec((d, tf), lambda i, f: (0, nf + f)),
            pl.BlockSpec((FFN_CONV, tf), lambda i, f: (0, f)),
            pl.BlockSpec((1, tf), lambda i, f: (0, f)),
            pl.BlockSpec((tf, d), lambda i, f: (f, 0)),
            pl.BlockSpec((1, d), row),
            pl.BlockSpec((1, d), row),
        ],
        out_specs=pl.BlockSpec((tm, d), lambda i, f: (i, 0)),
        out_shape=jax.ShapeDtypeStruct((s, d), F32),
        scratch_shapes=[
            pltpu.VMEM((halo + tm, d), BF16),
            pltpu.VMEM((halo + tm, tf), F32),
            pltpu.VMEM((tm, d), F32),
        ],
        compiler_params=_cparams(("parallel", "arbitrary")),
        name="conv_ffn",
    )(x, x, sc, sh, gate, w_up, w_up, conv_w, conv_b, w_down, ln_g, ln_b)


def _token_mixer(x, sc, sh, gate, w_in, cmp_pe, cmp_w1, cmp_w2, ml_conv_w, ml_conv_b, ml_gate_b,
                 ml_norm_g, w_br_nsa, w_br_ml, w_o, ln_g, ln_b):
    s, d = x.shape
    wb, wf = _split_w_in(w_in)
    tm = min(INPROJ_TM, s)
    projb = _inproj(x, sc, sh, wb, BF16, tm, INPROJ_TN)
    projf = _inproj(x, sc, sh, wf, F32, tm, INPROJ_TN)
    kcv = _compress(projf, cmp_pe, cmp_w1, cmp_w2)
    y_nsa = _nsa_attention(projb, projf, kcv)
    ifo = _F_SMALL + 2 * 128
    gates_col = projf[:, ifo:ifo + _ML_IF] + ml_gate_b[None, :]
    y_ml = _mlstm(projb, projf, gates_col, gates_col.T, ml_conv_w, ml_conv_b[None, :],
                  ml_norm_g[None, :], cpb=min(MLSTM_CHUNKS_PER_STEP, s // ML_CHUNK))
    return _merge_outproj(y_nsa, y_ml, projf, x, w_br_nsa.astype(BF16), w_br_ml.astype(BF16),
                          w_o.astype(BF16), gate, ln_g, ln_b, tm=min(MERGE_TM, s))


def _forward(x, c, w_ada, b_ada, w_in, cmp_pe, cmp_w1, cmp_w2, ml_conv_w, ml_conv_b, ml_gate_b,
             ml_norm_g, w_br_nsa, w_br_ml, w_o, w_up, ffn_conv_w, ffn_conv_b, w_down, ln_g, ln_b):
    b, s, d = x.shape
    assert b == 1 and d == D_MODEL
    depth = w_ada.shape[0]
    mod = _modulation(c, w_ada, b_ada)
    xs = x[0]
    for l in range(depth):
        sh1, sc1, g1, sh2, sc2, g2 = [mod[l, :, k * d:(k + 1) * d] for k in range(6)]
        xs = _token_mixer(xs, sc1, sh1, g1, w_in[l], cmp_pe[l], cmp_w1[l], cmp_w2[l], ml_conv_w[l],
                          ml_conv_b[l], ml_gate_b[l], ml_norm_g[l], w_br_nsa[l], w_br_ml[l], w_o[l],
                          ln_g[l, 0][None, :], ln_b[l, 0][None, :])
        xs = _conv_ffn(xs, sc2, sh2, g2, w_up[l].astype(BF16), ffn_conv_w[l], ffn_conv_b[l][None, :],
                       w_down[l].astype(BF16), ln_g[l, 1][None, :], ln_b[l, 1][None, :],
                       tm=min(FFN_TM, s), tf=FFN_TF)
    return xs[None]


def kernel(x, c, w_ada, b_ada, w_in, cmp_pe, cmp_w1, cmp_w2, ml_conv_w, ml_conv_b, ml_gate_b, ml_norm_g, w_br_nsa, w_br_ml, w_o, w_up, ffn_conv_w, ffn_conv_b, w_down, ln_g, ln_b):
    return _forward(x, c, w_ada, b_ada, w_in, cmp_pe, cmp_w1, cmp_w2, ml_conv_w, ml_conv_b,
                    ml_gate_b, ml_norm_g, w_br_nsa, w_br_ml, w_o, w_up, ffn_conv_w, ffn_conv_b,
                    w_down, ln_g, ln_b)
```

```python
import functools
import math

import jax
import jax.numpy as jnp
from jax import lax
from jax.experimental import pallas as pl
from jax.experimental.pallas import tpu as pltpu

F32 = jnp.float32
BF16 = jnp.bfloat16

D_MODEL = 2048
DEPTH = 2
NSA_HEADS = 8
NSA_GROUPS = 2
NSA_HPG = NSA_HEADS // NSA_GROUPS
NSA_HD = 128
CMP_LEN = 32
CMP_STRIDE = 16
CMP_HID = 256
SLC_LEN = 64
SLC_TOPK = 16
WINDOW = 512
Q_BLOCK = 128
ML_HEADS = 4
ML_DQK = 128
ML_DV = 256
ML_CHUNK = 64
ML_CONV = 4
D_FF = 5632
FFN_CONV = 3
ALPHA = (2 * DEPTH) ** 0.25
LN_EPS = 1e-5
NEG_INF = -1e30

V7X_VMEM_BYTES = 64 * 1024 * 1024
VMEM_LIMIT = V7X_VMEM_BYTES - 8 * 1024 * 1024

INPROJ_TM, INPROJ_TN = 1024, 1024
MERGE_TM = 256
FFN_TM, FFN_TF = 512, 512
MLSTM_CHUNKS_PER_STEP = 4

_NSA_Q = NSA_HEADS * NSA_HD
_NSA_KV = 3 * 2 * NSA_GROUPS * NSA_HD
_NSA_G = 3 * NSA_HEADS
_ML_QK = 2 * ML_HEADS * ML_DQK
_ML_V = ML_HEADS * ML_DV
_ML_IF = 2 * ML_HEADS
_CMP_COLS = 2 * NSA_GROUPS * NSA_HD
_F_MERGE = 0
_F_CMP = 2 * D_MODEL
_F_MLQK = _F_CMP + _CMP_COLS
_F_MLO = _F_MLQK + _ML_QK
_F_SMALL = _F_MLO + _ML_V
_F_COLS = _F_SMALL + 4 * 128
_B_Q = 0
_B_KV = _NSA_Q
_B_MLV = _B_KV + 8 * NSA_HD
_B_COLS = _B_MLV + _ML_V

SEL_TILE = 512
SEL_STEP = 4096
SEL_BIG = 2.0 ** 100
NSA_Q_BLOCKS_PER_STEP = 2
SEL_EXP2_GUARD = 64.0


def _cparams(sem):
    return pltpu.CompilerParams(dimension_semantics=sem, vmem_limit_bytes=VMEM_LIMIT)


def _ln_rows(x):
    mu = jnp.mean(x, axis=-1, keepdims=True)
    xc = x - mu
    var = jnp.mean(xc * xc, axis=-1, keepdims=True)
    return xc * lax.rsqrt(var + LN_EPS)


def _sigmoid(x):
    return 1.0 / (1.0 + jnp.exp(-x))


def _silu(x):
    return x * _sigmoid(x)


def _gelu_tanh(x):
    c = math.sqrt(2.0 / math.pi)
    return x * (0.5 * (1.0 + jnp.tanh(c * (x + 0.044715 * (x * x * x)))))


def _log_sigmoid(x):
    return jnp.minimum(x, 0.0) - jnp.log1p(jnp.exp(-jnp.abs(x)))


def _dot(a, b):
    return jnp.dot(a, b, preferred_element_type=F32)


def _dot_nt(a, b):
    return lax.dot_general(a, b, (((1,), (1,)), ((), ())), preferred_element_type=F32)


def _masked_softmax(raw, mask, scale):
    raw = jnp.where(mask, raw, NEG_INF)
    m = jnp.max(raw, axis=-1, keepdims=True)
    e = jnp.exp2((raw - m) * (scale * math.log2(math.e)))
    den = jnp.sum(e, axis=-1, keepdims=True)
    return e * jnp.where(m > 0.5 * NEG_INF, 1.0 / den, 0.0)


def _mod_kernel(c_ref, w_ref, b_ref, o_ref):
    ca = _silu(c_ref[...])
    o = _dot(ca.astype(BF16), w_ref[0].astype(BF16))
    o_ref[0] = o[0:1] + b_ref[0]


def _modulation(c, w_ada, b_ada):
    depth, d, n = w_ada.shape
    tn = INPROJ_TN
    c8 = jnp.broadcast_to(c, (8, d))
    return pl.pallas_call(
        _mod_kernel,
        grid=(depth, n // tn),
        in_specs=[
            pl.BlockSpec((8, d), lambda l, j: (0, 0)),
            pl.BlockSpec((1, d, tn), lambda l, j: (l, 0, j)),
            pl.BlockSpec((1, 1, tn), lambda l, j: (l, 0, j)),
        ],
        out_specs=pl.BlockSpec((1, 1, tn), lambda l, j: (l, 0, j)),
        out_shape=jax.ShapeDtypeStruct((depth, 1, n), F32),
        compiler_params=_cparams(("parallel", "parallel")),
        name="adaln_mod",
    )(c8, w_ada, b_ada.reshape(depth, 1, n))


def _inproj_kernel(x_ref, sc_ref, sh_ref, w_ref, o_ref, h_scr):
    @pl.when(pl.program_id(1) == 0)
    def _():
        h = _ln_rows(x_ref[...]) * (1.0 + sc_ref[...]) + sh_ref[...]
        h_scr[...] = h.astype(BF16)

    o_ref[...] = _dot(h_scr[...], w_ref[...]).astype(o_ref.dtype)


def _inproj(x, sc, sh, w, out_dtype, tm, tn):
    s, d = x.shape
    n = w.shape[1]
    return pl.pallas_call(
        _inproj_kernel,
        grid=(s // tm, n // tn),
        in_specs=[
            pl.BlockSpec((tm, d), lambda i, j: (i, 0)),
            pl.BlockSpec((1, d), lambda i, j: (0, 0)),
            pl.BlockSpec((1, d), lambda i, j: (0, 0)),
            pl.BlockSpec((d, tn), lambda i, j: (0, j)),
        ],
        out_specs=pl.BlockSpec((tm, tn), lambda i, j: (i, j)),
        out_shape=jax.ShapeDtypeStruct((s, n), out_dtype),
        scratch_shapes=[pltpu.VMEM((tm, d), BF16)],
        compiler_params=_cparams(("parallel", "arbitrary")),
        name="in_proj",
    )(x, sc, sh, w)


def _split_w_in(w):
    d = w.shape[0]
    o = 0
    q = w[:, o:o + _NSA_Q]; o += _NSA_Q
    kv = w[:, o:o + _NSA_KV]; o += _NSA_KV
    g = w[:, o:o + _NSA_G]; o += _NSA_G
    mlqk = w[:, o:o + _ML_QK]; o += _ML_QK
    mlv = w[:, o:o + _ML_V]; o += _ML_V
    mlif = w[:, o:o + _ML_IF]; o += _ML_IF
    mlo = w[:, o:o + _ML_V]; o += _ML_V
    merge = w[:, o:o + 2 * D_MODEL]
    per_g = 3 * NSA_HPG
    z = lambda n: jnp.zeros((d, n), w.dtype)
    wb = jnp.concatenate([q, kv[:, _CMP_COLS:], mlv], axis=1)
    wf = jnp.concatenate(
        [merge, kv[:, :_CMP_COLS], mlqk, mlo,
         g[:, :per_g], z(128 - per_g), g[:, per_g:], z(128 - per_g), mlif, z(128 - _ML_IF), z(128)],
        axis=1)
    return wb.astype(BF16), wf.astype(BF16)


def _compress_kernel(a_ref, pe_ref, w1_ref, w2_ref, o_ref):
    n = o_ref.shape[1]
    hd = NSA_HD

    def half_sum(lo):
        acc = jnp.zeros((n, CMP_HID), F32)
        for l in range(0, CMP_STRIDE, 2):
            x = jnp.concatenate(
                [a_ref[pl.ds(l + u, n, stride=CMP_STRIDE), :] + pe_ref[0, lo + l + u:lo + l + u + 1, :]
                 for u in range(2)], axis=1)
            w = w1_ref[0, (lo + l) * hd:(lo + l + 2) * hd, :]
            acc = acc + _dot(x.astype(BF16), w.astype(BF16))
        return acc

    pre = half_sum(0) + pltpu.roll(half_sum(CMP_STRIDE), n - 1, 0)
    g = _gelu_tanh(pre)
    o_ref[0] = _dot(g.astype(BF16), w2_ref[0].astype(BF16)).astype(o_ref.dtype)


def _compress(projf, pe, w1, w2):
    s = projf.shape[0]
    n = s // CMP_STRIDE
    four = 2 * NSA_GROUPS
    return pl.pallas_call(
        _compress_kernel,
        grid=(four,),
        in_specs=[
            pl.BlockSpec((s, NSA_HD), lambda j: (0, _F_CMP // NSA_HD + j)),
            pl.BlockSpec((1, CMP_LEN, NSA_HD), lambda j: (j // NSA_GROUPS, 0, 0)),
            pl.BlockSpec((1, CMP_LEN * NSA_HD, CMP_HID), lambda j: (j // NSA_GROUPS, 0, 0)),
            pl.BlockSpec((1, CMP_HID, NSA_HD), lambda j: (j // NSA_GROUPS, 0, 0)),
        ],
        out_specs=pl.BlockSpec((1, n, NSA_HD), lambda j: (j, 0, 0)),
        out_shape=jax.ShapeDtypeStruct((four, n, NSA_HD), BF16),
        compiler_params=_cparams(("parallel",)),
        name="nsa_compress",
    )(projf, pe, w1, w2)


def _nsa_kernel(q_ref, kc_ref, vc_ref, ks_ref, vs_ref, kw_ref, vw_ref, gate_ref, ovt_ref, eye_ref,
                o_ref, nsel_scr, npast_scr, *, step, nq):
    pid = pl.program_id(1)
    rows = NSA_HPG * Q_BLOCK
    scale = NSA_HD ** -0.5
    c2 = scale * math.log2(math.e)
    row_q = lax.broadcasted_iota(jnp.int32, (rows, 1), 0) & (Q_BLOCK - 1)
    lane_q = lax.broadcasted_iota(jnp.int32, (1, Q_BLOCK), 1)
    blk = []
    for u in range(nq):
        i = pid * nq + u
        qt = q_ref[u * Q_BLOCK:(u + 1) * Q_BLOCK, :]
        qs = jnp.concatenate([qt[:, h * NSA_HD:(h + 1) * NSA_HD] for h in range(NSA_HPG)], axis=0)
        blk.append((i, qs, i * Q_BLOCK + row_q))
    i_last = blk[-1][0]

    ncp = kc_ref.shape[1]

    def cmp_branch(width):
        outs = []
        cidx = lax.broadcasted_iota(jnp.int32, (rows, width), 1)
        ovt = ovt_ref[:, :width]
        for _, qs, t in blk:
            last_c = (t - (CMP_LEN - 1)) // CMP_STRIDE
            p_c = _masked_softmax(_dot_nt(qs, kc_ref[0, :width, :]), cidx <= last_c, scale)
            o = _dot(p_c.astype(BF16), vc_ref[0, :width, :])
            psum = (p_c[0:Q_BLOCK] + p_c[Q_BLOCK:2 * Q_BLOCK]
                    + p_c[2 * Q_BLOCK:3 * Q_BLOCK] + p_c[3 * Q_BLOCK:4 * Q_BLOCK])
            p_hi = psum.astype(BF16)
            p_lo = (psum - p_hi.astype(F32)).astype(BF16)
            outs += [o, _dot_nt(ovt, p_hi) + _dot_nt(ovt, p_lo)]
        return tuple(outs)

    n_vis = (i_last * Q_BLOCK + Q_BLOCK - CMP_LEN) // CMP_STRIDE + 1
    branch = lambda: cmp_branch(ncp)
    for width in (ncp // 2, ncp // 4):
        if width % 128 == 0:
            branch = (lambda w, other: lambda: lax.cond(n_vis <= w, lambda: cmp_branch(w), other))(width, branch)
    cmp_out = branch()
    o_cs, imps = cmp_out[0::2], cmp_out[1::2]

    wsl = imps[0].shape[0]
    bid = lax.broadcasted_iota(jnp.int32, (wsl, Q_BLOCK), 0)
    bidf = bid.astype(F32)
    taken = -1e9
    nsel0 = jnp.full((wsl, Q_BLOCK), -SEL_BIG, F32)
    score0s = []
    for (i, _, _), imp in zip(blk, imps):
        t_row = i * Q_BLOCK + lane_q
        cur = t_row // SLC_LEN
        score0s.append(jnp.where(bid == 0, 3e6, jnp.where(bid == cur, 2e6, jnp.where(
            bid == cur - 1, 1e6, jnp.where(bid * SLC_LEN <= t_row, imp, -1.0 - bidf)))))

    def take_max(c, lanes=None):
        sc, ns = c
        hit = sc == jnp.max(sc, axis=0, keepdims=True)
        if lanes is not None:
            hit = hit & lanes
        return jnp.where(hit, taken, sc), jnp.where(hit, 0.0, ns)

    cs = [(jnp.where(s0 >= 1e6, taken, s0), jnp.where(s0 >= 1e6, 0.0, nsel0)) for s0 in score0s]
    n_min = SLC_TOPK - 3
    for _ in range(n_min):
        cs = [take_max(c) for c in cs]

    def early_rounds():
        cur = lane_q // SLC_LEN
        n_free = SLC_TOPK - (1 + jnp.where(cur >= 1, 1, 0) + jnp.where(cur >= 2, 1, 0))
        e = cs[0]
        for r in range(n_min + 1, SLC_TOPK):
            e = take_max(e, n_free >= r)
        return e

    cs[0] = lax.cond(pid == 0, early_rounds, lambda: cs[0])
    nsels = tuple(c[1] for c in cs)

    def topk_ties():
        def one(_, c):
            sc, ns = c
            mx = jnp.max(sc, axis=0, keepdims=True)
            first = jnp.min(jnp.where(sc == mx, bidf, float(wsl)), axis=0, keepdims=True)
            hit = bidf == first
            return jnp.where(hit, taken, sc), jnp.where(hit, 0.0, ns)
        return tuple(lax.fori_loop(0, SLC_TOPK, one, (s0, nsel0))[1] for s0 in score0s)

    n_taken = functools.reduce(jnp.maximum, [jnp.sum(jnp.where(ns == 0.0, 1.0, 0.0), axis=0, keepdims=True)
                                             for ns in nsels])
    nsels = lax.cond(jnp.max(n_taken) > SLC_TOPK, topk_ties, lambda: nsels)
    for u, ((i, _, _), ns) in enumerate(zip(blk, nsels)):
        blk_d = (i * Q_BLOCK) // SLC_LEN
        nsel_scr[u] = ns
        npast_scr[u] = jnp.where(bid < blk_d, ns, -SEL_BIG)

    def keys_aug(k0, width, sel_scr):
        b0 = k0 // SLC_LEN
        bias = jnp.concatenate(
            [jnp.broadcast_to(sel_scr[pl.ds(b0 + b, 1), :], (SLC_LEN, Q_BLOCK))
             for b in range(width // SLC_LEN)], axis=0)
        return jnp.concatenate([ks_ref[pl.ds(k0, width), :], bias.astype(BF16)], axis=1)

    def vals_aug(k0, width):
        ones_col = (lax.broadcasted_iota(jnp.int32, (width, NSA_HD), 1) == 0).astype(BF16)
        return jnp.concatenate([vs_ref[pl.ds(k0, width), :], ones_col], axis=1)

    qas, diag, o_ws = [], [], []
    wlen = WINDOW + Q_BLOCK
    for u, (i, qs, t) in enumerate(blk):
        qa = jnp.concatenate([qs, eye_ref[...]], axis=1)
        k_d = pl.multiple_of(i * Q_BLOCK, Q_BLOCK)
        s_d = _dot_nt(qa, keys_aug(k_d, Q_BLOCK, nsel_scr.at[u]))
        s_d = jnp.where(k_d + lane_q <= t, s_d, NEG_INF)
        m_d = jnp.max(s_d, axis=-1, keepdims=True)
        acc_d = _dot(jnp.exp2((s_d - m_d) * c2).astype(BF16), vals_aug(k_d, Q_BLOCK))
        qas.append(qa)
        diag.append((m_d, acc_d))
        ks0 = pl.multiple_of(jnp.maximum(i - WINDOW // Q_BLOCK, 0) * Q_BLOCK, Q_BLOCK)
        s_w = _dot_nt(qs, kw_ref[pl.ds(ks0, wlen), :])
        kpos = ks0 + lax.broadcasted_iota(jnp.int32, (1, wlen), 1)
        p_w = _masked_softmax(s_w, (kpos <= t) & (kpos > t - WINDOW), scale)
        o_ws.append(_dot(p_w.astype(BF16), vw_ref[pl.ds(ks0, wlen), :]))

    gates = _sigmoid(gate_ref[...])
    for u, (i, qs, t) in enumerate(blk):
        qa = qas[u]
        past_scr = npast_scr.at[u]
        m_d, acc_d = diag[u]

        def exact_step(k0, carry):
            m, acc = carry
            k0 = pl.multiple_of(k0, SEL_TILE)
            s = _dot_nt(qa, keys_aug(k0, SEL_TILE, past_scr))
            m_new = jnp.maximum(m, jnp.max(s, axis=-1, keepdims=True))
            p = jnp.exp2((s - m_new) * c2)
            return m_new, jnp.exp2((m - m_new) * c2) * acc + _dot(p.astype(BF16), vals_aug(k0, SEL_TILE))

        def fast_step(k0, width, carry):
            m, acc, risk = carry
            k0 = pl.multiple_of(k0, SEL_TILE)
            s = _dot_nt(qa, keys_aug(k0, width, past_scr))
            p = jnp.exp2((s - m) * c2)
            mx = jnp.max(s, axis=-1, keepdims=True)
            m_new = jnp.maximum(m, mx)
            acc = (acc + _dot(p.astype(BF16), vals_aug(k0, width))) * jnp.exp2((m - m_new) * c2)
            return m_new, acc, jnp.maximum(risk, mx - m)

        n_past = i * Q_BLOCK
        n_steps = n_past // step
        rem = n_past - n_steps * step
        n_loop = n_steps + jnp.where(rem > step // 2, 1, 0)
        carry = (m_d, acc_d, jnp.zeros((rows, 1), F32))
        carry = lax.fori_loop(0, n_loop, lambda j, c: fast_step(j * step, step, c), carry)
        k_r = n_steps * step
        carry = lax.cond((rem > 0) & (rem <= step // 2), lambda: fast_step(k_r, step // 2, carry),
                         lambda: carry)
        _, acc_s, risk = carry

        def redo_exact():
            n_tiles = (n_past + SEL_TILE - 1) // SEL_TILE
            return lax.fori_loop(0, n_tiles, lambda j, c: exact_step(j * SEL_TILE, c), (m_d, acc_d))[1]

        acc_s = lax.cond(jnp.max(risk) * c2 > SEL_EXP2_GUARD, redo_exact, lambda: acc_s)
        o_s = acc_s[:, :NSA_HD] / acc_s[:, NSA_HD:NSA_HD + 1]

        g_u = gates[u * Q_BLOCK:(u + 1) * Q_BLOCK]
        for h in range(NSA_HPG):
            r = slice(h * Q_BLOCK, (h + 1) * Q_BLOCK)
            out = (g_u[:, 3 * h:3 * h + 1] * o_cs[u][r] + g_u[:, 3 * h + 1:3 * h + 2] * o_s[r]
                   + g_u[:, 3 * h + 2:3 * h + 3] * o_ws[u][r])
            o_ref[u * Q_BLOCK:(u + 1) * Q_BLOCK, h * NSA_HD:(h + 1) * NSA_HD] = out.astype(o_ref.dtype)


def _nsa_attention(projb, projf, kcv, single_buffer=True):
    s = projb.shape[0]
    n_slc = s // SLC_LEN
    step = min(SEL_STEP, s)
    nq = NSA_Q_BLOCKS_PER_STEP
    qrows = nq * Q_BLOCK
    assert n_slc >= SLC_TOPK and s >= WINDOW + Q_BLOCK and s % step == 0 and step % (4 * SEL_TILE) == 0
    ncp = kcv.shape[1]
    wsl = max(n_slc, 128)
    hd = NSA_HD
    c_start = jnp.arange(ncp)[None, :] * CMP_STRIDE
    s_start = jnp.arange(wsl)[:, None] * SLC_LEN
    ovt = ((c_start < s_start + SLC_LEN) & (c_start + CMP_LEN > s_start)
           & (jnp.arange(wsl)[:, None] < n_slc)).astype(BF16)
    eye = jnp.tile(jnp.eye(Q_BLOCK, dtype=BF16), (NSA_HPG, 1))
    resident = dict(pipeline_mode=pl.Buffered(1)) if single_buffer else {}
    kvb = _B_KV // hd
    gb = _F_SMALL // 128
    return pl.pallas_call(
        functools.partial(_nsa_kernel, step=step, nq=nq),
        grid=(NSA_GROUPS, s // qrows),
        in_specs=[
            pl.BlockSpec((qrows, NSA_HPG * hd), lambda g, i: (i, g)),
            pl.BlockSpec((1, ncp, hd), lambda g, i: (g, 0, 0)),
            pl.BlockSpec((1, ncp, hd), lambda g, i: (NSA_GROUPS + g, 0, 0)),
            pl.BlockSpec((s, hd), lambda g, i: (0, kvb + g), **resident),
            pl.BlockSpec((s, hd), lambda g, i: (0, kvb + 2 + g), **resident),
            pl.BlockSpec((s, hd), lambda g, i: (0, kvb + 4 + g), **resident),
            pl.BlockSpec((s, hd), lambda g, i: (0, kvb + 6 + g), **resident),
            pl.BlockSpec((qrows, 128), lambda g, i: (i, gb + g)),
            pl.BlockSpec((wsl, ncp), lambda g, i: (0, 0)),
            pl.BlockSpec((NSA_HPG * Q_BLOCK, Q_BLOCK), lambda g, i: (0, 0)),
        ],
        out_specs=pl.BlockSpec((qrows, NSA_HPG * hd), lambda g, i: (i, g)),
        out_shape=jax.ShapeDtypeStruct((s, NSA_HEADS * hd), BF16),
        scratch_shapes=[pltpu.VMEM((nq, wsl, Q_BLOCK), F32), pltpu.VMEM((nq, wsl, Q_BLOCK), F32)],
        compiler_params=_cparams(("parallel", "arbitrary")),
        name="nsa_attention",
    )(projb, kcv, kcv, projb, projb, projb, projb, projf, ovt, eye)


def _mlstm_kernel(q_ref, k_ref, v_ref, o0_ref, o1_ref, gc_ref, gr_ref, cw_ref, cb_ref, ng_ref,
                  y_ref, extq, extk, c_st, n_st, m_st, *, cpb):
    step = pl.program_id(0)
    rows = cpb * ML_CHUNK
    half = ML_HEADS * ML_DQK
    L = ML_CHUNK

    @pl.when(step == 0)
    def _():
        extq[0:8, :] = jnp.zeros((8, half), F32)
        extk[0:8, :] = jnp.zeros((8, half), F32)
        c_st[...] = jnp.zeros_like(c_st)
        n_st[...] = jnp.zeros_like(n_st)
        m_st[...] = jnp.zeros_like(m_st)

    extq[8:8 + rows, :] = q_ref[...]
    extk[8:8 + rows, :] = k_ref[...]
    cw = cw_ref[...]
    cb = cb_ref[...]

    def conv(ext, lo):
        acc = cb[:, lo:lo + half]
        for j in range(ML_CONV):
            off = 8 - (ML_CONV - 1) + j
            acc = acc + cw[j:j + 1, lo:lo + half] * ext[off:off + rows, :]
        return _silu(acc)

    qa = conv(extq, 0)
    ka = conv(extk, half) * (ML_DQK ** -0.5)
    extq[0:8, :] = extq[rows:rows + 8, :]
    extk[0:8, :] = extk[rows:rows + 8, :]

    gcol = gc_ref[...]
    grow = gr_ref[...]
    lf_col_all = _log_sigmoid(gcol[:, ML_HEADS:])
    lf_row_all = _log_sigmoid(grow[ML_HEADS:, :])
    ri = lax.broadcasted_iota(jnp.int32, (L, L), 0)
    ci = lax.broadcasted_iota(jnp.int32, (L, L), 1)
    tri = ri >= ci
    ng = ng_ref[...]

    state = [(c_st[h], n_st[h], m_st[h][:, 0:1]) for h in range(ML_HEADS)]
    ones_blk = jnp.ones((L, 128), BF16)
    for c in range(cpb):
        r0 = c * L
        kt_c = ka[r0:r0 + L, :].T
        for h in range(ML_HEADS):
            qh = qa[r0:r0 + L, h * ML_DQK:(h + 1) * ML_DQK]
            kh = ka[r0:r0 + L, h * ML_DQK:(h + 1) * ML_DQK]
            vh = v_ref[r0:r0 + L, h * ML_DV:(h + 1) * ML_DV]
            ig_col = gcol[r0:r0 + L, h:h + 1]
            ig_row = grow[h:h + 1, r0:r0 + L]
            lf_col = lf_col_all[r0:r0 + L, h:h + 1]
            lf_row = lf_row_all[h:h + 1, r0:r0 + L]
            bcum_col = jnp.sum(jnp.where(tri, lf_row, 0.0), axis=1, keepdims=True)
            bcum_row = jnp.sum(jnp.where(ri <= ci, lf_col, 0.0), axis=0, keepdims=True)
            b_last = jnp.sum(lf_row, axis=1, keepdims=True)
            dmat = jnp.where(tri, bcum_col - bcum_row + ig_row, NEG_INF)
            a_loc = jnp.max(dmat, axis=-1, keepdims=True)
            qb = qh.astype(BF16)
            s_loc = _dot_nt(qb, kh.astype(BF16)) * jnp.exp(dmat - a_loc)
            sv = _dot(s_loc.astype(BF16), vh)
            s_sum = _dot(s_loc.astype(BF16), ones_blk)[:, 0:1]
            a_col = b_last - bcum_col + ig_col
            a_row = b_last - bcum_row + ig_row
            a_max = jnp.max(a_row, axis=-1, keepdims=True)
            k_sum = jnp.sum(kh * jnp.exp(a_col - a_max), axis=0, keepdims=True)
            kwt = kt_c[h * ML_DQK:(h + 1) * ML_DQK, :] * jnp.exp(a_row - a_max)
            kv = _dot(kwt.astype(BF16), vh)

            ct, nrow, m_old = state[h]
            inter = bcum_col + m_old
            m_t = jnp.maximum(inter, a_loc)
            w_inter = jnp.exp(inter - m_t)
            w_loc = jnp.exp(a_loc - m_t)
            num = w_inter * _dot(qb, ct.astype(BF16)) + w_loc * sv
            qn = _dot_nt(qb, jnp.broadcast_to(nrow, (8, ML_DQK)).astype(BF16))[:, 0:1]
            den = w_inter * qn + w_loc * s_sum
            hout = num / jnp.maximum(jnp.abs(den), jnp.exp(-m_t))

            m_new = jnp.maximum(b_last + m_old, a_max)
            decay = jnp.exp(b_last + m_old - m_new)
            g_new = jnp.exp(a_max - m_new)
            state[h] = (decay * ct + g_new * kv, decay * nrow + g_new * k_sum, m_new)

            o_ref = o0_ref if h < ML_HEADS // 2 else o1_ref
            oc = (h % (ML_HEADS // 2)) * ML_DV
            og = _sigmoid(o_ref[r0:r0 + L, oc:oc + ML_DV])
            yn = _ln_rows(hout) * ng[:, h * ML_DV:(h + 1) * ML_DV]
            y_ref[r0:r0 + L, h * ML_DV:(h + 1) * ML_DV] = (yn * og).astype(y_ref.dtype)

    for h in range(ML_HEADS):
        c_st[h], n_st[h] = state[h][0], state[h][1]
        m_st[h] = jnp.broadcast_to(state[h][2], (1, 128))


def _mlstm(projb, projf, gates_col, gates_row, conv_w, conv_b, norm_g, cpb):
    s = projb.shape[0]
    rows = cpb * ML_CHUNK
    half = ML_HEADS * ML_DQK
    dv = ML_HEADS * ML_DV
    qblk = _F_MLQK // half
    oblk = _F_MLO // half
    return pl.pallas_call(
        functools.partial(_mlstm_kernel, cpb=cpb),
        grid=(s // rows,),
        in_specs=[
            pl.BlockSpec((rows, half), lambda j: (j, qblk)),
            pl.BlockSpec((rows, half), lambda j: (j, qblk + 1)),
            pl.BlockSpec((rows, dv), lambda j: (j, _B_MLV // dv)),
            pl.BlockSpec((rows, half), lambda j: (j, oblk)),
            pl.BlockSpec((rows, half), lambda j: (j, oblk + 1)),
            pl.BlockSpec((rows, 2 * ML_HEADS), lambda j: (j, 0)),
            pl.BlockSpec((2 * ML_HEADS, rows), lambda j: (0, j)),
            pl.BlockSpec((ML_CONV, 2 * half), lambda j: (0, 0)),
            pl.BlockSpec((1, 2 * half), lambda j: (0, 0)),
            pl.BlockSpec((1, dv), lambda j: (0, 0)),
        ],
        out_specs=pl.BlockSpec((rows, dv), lambda j: (j, 0)),
        out_shape=jax.ShapeDtypeStruct((s, dv), BF16),
        scratch_shapes=[
            pltpu.VMEM((rows + 8, half), F32),
            pltpu.VMEM((rows + 8, half), F32),
            pltpu.VMEM((ML_HEADS, ML_DQK, ML_DV), F32),
            pltpu.VMEM((ML_HEADS, 1, ML_DQK), F32),
            pltpu.VMEM((ML_HEADS, 1, 128), F32),
        ],
        compiler_params=_cparams(("arbitrary",)),
        name="mlstm",
    )(projf, projf, projb, projf, projf, gates_col, gates_row, conv_w, conv_b, norm_g)


def _merge_kernel(yn_ref, ym_ref, g0_ref, g1_ref, x_ref, wn_ref, wm_ref, wo_ref, gate_ref,
                  lg_ref, lb_ref, o_ref):
    a = _dot(yn_ref[...], wn_ref[...])
    b = _dot(ym_ref[...], wm_ref[...])
    merged = _sigmoid(g0_ref[...]) * a + _sigmoid(g1_ref[...]) * b
    y = _dot(merged.astype(BF16), wo_ref[...])
    z = ALPHA * x_ref[...] + gate_ref[...] * y
    o_ref[...] = _ln_rows(z) * lg_ref[...] + lb_ref[...]


def _merge_outproj(y_nsa, y_ml, projf, x, wn, wm, wo, gate, ln_g, ln_b, tm, single_buffer=True):
    s, d = x.shape
    resident = dict(pipeline_mode=pl.Buffered(1)) if single_buffer else {}
    mb = _F_MERGE // d
    row = lambda i: (0, 0)
    return pl.pallas_call(
        _merge_kernel,
        grid=(s // tm,),
        in_specs=[
            pl.BlockSpec((tm, y_nsa.shape[1]), lambda i: (i, 0)),
            pl.BlockSpec((tm, y_ml.shape[1]), lambda i: (i, 0)),
            pl.BlockSpec((tm, d), lambda i: (i, mb)),
            pl.BlockSpec((tm, d), lambda i: (i, mb + 1)),
            pl.BlockSpec((tm, d), lambda i: (i, 0)),
            pl.BlockSpec(wn.shape, row, **resident),
            pl.BlockSpec(wm.shape, row, **resident),
            pl.BlockSpec(wo.shape, row, **resident),
            pl.BlockSpec((1, d), row),
            pl.BlockSpec((1, d), row),
            pl.BlockSpec((1, d), row),
        ],
        out_specs=pl.BlockSpec((tm, d), lambda i: (i, 0)),
        out_shape=jax.ShapeDtypeStruct((s, d), F32),
        compiler_params=_cparams(("parallel",)),
        name="merge_outproj",
    )(y_nsa, y_ml, projf, projf, x, wn, wm, wo, gate, ln_g, ln_b)


def _ffn_kernel(x_ref, xh_ref, sc_ref, sh_ref, gate_ref, wa_ref, wg_ref, cw_ref, cb_ref, wd_ref,
                lg_ref, lb_ref, o_ref, h_scr, a_scr, acc, *, halo):
    i = pl.program_id(0)
    f = pl.program_id(1)
    tm = x_ref.shape[0]

    @pl.when(f == 0)
    def _():
        mod = lambda v: (_ln_rows(v) * (1.0 + sc_ref[...]) + sh_ref[...]).astype(BF16)
        h_scr[0:halo, :] = mod(xh_ref[...])
        h_scr[halo:halo + tm, :] = mod(x_ref[...])
        acc[...] = jnp.zeros_like(acc)

    hx = h_scr[...]
    a_ext = _dot(hx, wa_ref[...])
    rid = lax.broadcasted_iota(jnp.int32, (halo + tm, 1), 0)
    a_scr[...] = jnp.where((rid >= halo) | (i > 0), a_ext, 0.0)
    cw = cw_ref[...]
    conv = cb_ref[...]
    for j in range(FFN_CONV):
        off = halo - (FFN_CONV - 1) + j
        conv = conv + cw[j:j + 1, :] * a_scr[off:off + tm, :]
    g = _dot(h_scr[halo:halo + tm, :], wg_ref[...])
    act = (_silu(conv) * g).astype(BF16)
    acc[...] += _dot(act, wd_ref[...])

    @pl.when(f == pl.num_programs(1) - 1)
    def _():
        z = ALPHA * x_ref[...] + gate_ref[...] * acc[...]
        o_ref[...] = _ln_rows(z) * lg_ref[...] + lb_ref[...]


def _conv_ffn(x, sc, sh, gate, w_up, conv_w, conv_b, w_down, ln_g, ln_b, tm, tf):
    s, d = x.shape
    dff = w_down.shape[0]
    halo = 16
    nf = dff // tf
    row = lambda i, f: (0, 0)
    return pl.pallas_call(
        functools.partial(_ffn_kernel, halo=halo),
        grid=(s // tm, nf),
        in_specs=[
            pl.BlockSpec((tm, d), lambda i, f: (i, 0)),
            pl.BlockSpec((halo, d), lambda i, f: (jnp.maximum(i * (tm // halo) - 1, 0), 0)),
            pl.BlockSpec((1, d), row),
            pl.BlockSpec((1, d), row),
            pl.BlockSpec((1, d), row),
            pl.BlockSpec((d, tf), lambda i, f: (0, f)),
            pl.BlockSpec((d, tf), lambda i, f: (0, nf + f)),
            pl.BlockSpec((FFN_CONV, tf), lambda i, f: (0, f)),
            pl.BlockSpec((1, tf), lambda i, f: (0, f)),
            pl.BlockSpec((tf, d), lambda i, f: (f, 0)),
            pl.BlockSpec((1, d), row),
            pl.BlockSpec((1, d), row),
        ],
        out_specs=pl.BlockSpec((tm, d), lambda i, f: (i, 0)),
        out_shape=jax.ShapeDtypeStruct((s, d), F32),
        scratch_shapes=[
            pltpu.VMEM((halo + tm, d), BF16),
            pltpu.VMEM((halo + tm, tf), F32),
            pltpu.VMEM((tm, d), F32),
        ],
        compiler_params=_cparams(("parallel", "arbitrary")),
        name="conv_ffn",
    )(x, x, sc, sh, gate, w_up, w_up, conv_w, conv_b, w_down, ln_g, ln_b)


def _token_mixer(x, sc, sh, gate, w_in, cmp_pe, cmp_w1, cmp_w2, ml_conv_w, ml_conv_b, ml_gate_b,
                 ml_norm_g, w_br_nsa, w_br_ml, w_o, ln_g, ln_b):
    s, d = x.shape
    wb, wf = _split_w_in(w_in)
    tm = min(INPROJ_TM, s)
    projb = _inproj(x, sc, sh, wb, BF16, tm, INPROJ_TN)
    projf = _inproj(x, sc, sh, wf, F32, tm, INPROJ_TN)
    kcv = _compress(projf, cmp_pe, cmp_w1, cmp_w2)
    y_nsa = _nsa_attention(projb, projf, kcv)
    ifo = _F_SMALL + 2 * 128
    gates_col = projf[:, ifo:ifo + _ML_IF] + ml_gate_b[None, :]
    y_ml = _mlstm(projb, projf, gates_col, gates_col.T, ml_conv_w, ml_conv_b[None, :],
                  ml_norm_g[None, :], cpb=min(MLSTM_CHUNKS_PER_STEP, s // ML_CHUNK))
    return _merge_outproj(y_nsa, y_ml, projf, x, w_br_nsa.astype(BF16), w_br_ml.astype(BF16),
                          w_o.astype(BF16), gate, ln_g, ln_b, tm=min(MERGE_TM, s))


def _forward(x, c, w_ada, b_ada, w_in, cmp_pe, cmp_w1, cmp_w2, ml_conv_w, ml_conv_b, ml_gate_b,
             ml_norm_g, w_br_nsa, w_br_ml, w_o, w_up, ffn_conv_w, ffn_conv_b, w_down, ln_g, ln_b):
    b, s, d = x.shape
    assert b == 1 and d == D_MODEL
    depth = w_ada.shape[0]
    mod = _modulation(c, w_ada, b_ada)
    xs = x[0]
    for l in range(depth):
        sh1, sc1, g1, sh2, sc2, g2 = [mod[l, :, k * d:(k + 1) * d] for k in range(6)]
        xs = _token_mixer(xs, sc1, sh1, g1, w_in[l], cmp_pe[l], cmp_w1[l], cmp_w2[l], ml_conv_w[l],
                          ml_conv_b[l], ml_gate_b[l], ml_norm_g[l], w_br_nsa[l], w_br_ml[l], w_o[l],
                          ln_g[l, 0][None, :], ln_b[l, 0][None, :])
        xs = _conv_ffn(xs, sc2, sh2, g2, w_up[l].astype(BF16), ffn_conv_w[l], ffn_conv_b[l][None, :],
                       w_down[l].astype(BF16), ln_g[l, 1][None, :], ln_b[l, 1][None, :],
                       tm=min(FFN_TM, s), tf=FFN_TF)
    return xs[None]


def kernel(x, c, w_ada, b_ada, w_in, cmp_pe, cmp_w1, cmp_w2, ml_conv_w, ml_conv_b, ml_gate_b, ml_norm_g, w_br_nsa, w_br_ml, w_o, w_up, ffn_conv_w, ffn_conv_b, w_down, ln_g, ln_b):
    return _forward(x, c, w_ada, b_ada, w_in, cmp_pe, cmp_w1, cmp_w2, ml_conv_w, ml_conv_b,
                    ml_gate_b, ml_norm_g, w_br_nsa, w_br_ml, w_o, w_up, ffn_conv_w, ffn_conv_b,
                    w_down, ln_g, ln_b)
```
